```python
import numpy as np
import jax
import jax.numpy as jnp
from jax import lax

D_MODEL = 1024
BATCH = 4
SEQ = 8192
DEPTH = 2

HEAD_DIM = 64
NSA_HEADS = 8
NSA_KV_HEADS = 2
NSA_GROUP = NSA_HEADS // NSA_KV_HEADS
NSA_WIDTH = NSA_HEADS * HEAD_DIM
KV_WIDTH = NSA_KV_HEADS * HEAD_DIM
CMP_LEN = 32
CMP_STRIDE = 16
CMP_HIDDEN = 2 * HEAD_DIM
SEL_BLOCK = 64
N_SELECT = 16
WINDOW = 512
Q_BLOCK = 128
MASK_VALUE = -1e30
SEL_FORCE = 1e6
CONV_WIDTH = 256
CONV_K = 3
HGRN_HEADS = 4
HGRN_WIDTH = HGRN_HEADS * HEAD_DIM
HGRN_CHUNK = 64
LB_FLOOR = 1e-30
MIX_WIDTH = NSA_WIDTH + CONV_WIDTH + HGRN_WIDTH
ROPE_THETA = 10000.0
MEM_LEN = 256
CROSS_HEADS = 4
CROSS_WIDTH = CROSS_HEADS * HEAD_DIM
N_GROUPS = 4
EXPERTS_PER_GROUP = 8
N_EXPERTS = N_GROUPS * EXPERTS_PER_GROUP
EXPERT_TOP_K = 2
D_EXPERT = 512
RMS_EPS = 1e-6

IN_SIZES = (NSA_WIDTH, KV_WIDTH, KV_WIDTH, KV_WIDTH, KV_WIDTH, KV_WIDTH, KV_WIDTH, NSA_HEADS * 3,
            CONV_WIDTH, CONV_WIDTH, CONV_WIDTH, HGRN_WIDTH, HGRN_WIDTH, HGRN_WIDTH, HGRN_WIDTH)
IN_WIDTH = sum(IN_SIZES)
IN_OFFSETS = tuple(sum(IN_SIZES[:i + 1]) for i in range(len(IN_SIZES) - 1))

kernel_name = 'hymba_nsa_conv_hgrn2_hmoe'


def rmsnorm(x, g):
    x32 = x.astype(jnp.float32)
    y = x32 * lax.rsqrt(jnp.mean(x32 * x32, axis=-1, keepdims=True) + RMS_EPS)
    return (y * g.astype(jnp.float32)).astype(x.dtype)


def rope(x, pos):
    T = x.shape[1]
    half = x.shape[-1] // 2
    inv = ROPE_THETA ** (-jnp.arange(half, dtype=jnp.float32) / half)
    ang = pos.astype(jnp.float32)[:, None] * inv[None, :]
    bshape = (1, T) + (1,) * (x.ndim - 3) + (half,)
    cos = jnp.cos(ang).reshape(bshape)
    sin = jnp.sin(ang).reshape(bshape)
    x32 = x.astype(jnp.float32)
    x1, x2 = x32[..., :half], x32[..., half:]
    return jnp.concatenate([x1 * cos - x2 * sin, x2 * cos + x1 * sin], axis=-1).astype(x.dtype)


def masked_softmax(s, mask):
    s = jnp.where(mask, s.astype(jnp.float32), MASK_VALUE)
    m = jnp.max(s, axis=-1, keepdims=True)
    e = jnp.where(mask, jnp.exp(s - m), 0.0)
    den = jnp.sum(e, axis=-1, keepdims=True)
    return e / jnp.where(den > 0, den, 1.0)


def to_blocks(a):
    B, T = a.shape[:2]
    return jnp.moveaxis(a.reshape((B, T // Q_BLOCK, Q_BLOCK) + a.shape[2:]), 1, 0)


def nsa_compress(k, pe, w1, w2):
    B, T, H, dh = k.shape
    nc = (T - CMP_LEN) // CMP_STRIDE + 1
    idx = np.arange(nc)[:, None] * CMP_STRIDE + np.arange(CMP_LEN)[None, :]
    blk = k[:, idx] + pe[None, None, :, None, :]
    blk = jnp.swapaxes(blk, 2, 3).reshape(B, nc, H, CMP_LEN * dh)
    return jnp.dot(jax.nn.silu(jnp.dot(blk, w1)), w2)


def nsa_attention(q, q_rot, gate, kc, vc, ks, vs, kw, vw):
    B, T, H, G, dh = q.shape
    nc = kc.shape[1]
    ns = T // SEL_BLOCK
    n_sel = min(N_SELECT, ns)
    nblk = T // Q_BLOCK
    scale = dh ** -0.5
    cmp_end = jnp.arange(nc) * CMP_STRIDE + CMP_LEN - 1
    cs = np.arange(nc) * CMP_STRIDE
    ss = np.arange(ns) * SEL_BLOCK
    ov = np.clip(np.minimum(cs[:, None] + CMP_LEN, ss[None, :] + SEL_BLOCK)
                 - np.maximum(cs[:, None], ss[None, :]), 0, None) / CMP_LEN
    cmp_to_sel = jnp.asarray(ov, dtype=jnp.float32)
    ks_blk = jnp.swapaxes(ks, 1, 2).reshape(B, H, ns, SEL_BLOCK, dh)
    vs_blk = jnp.swapaxes(vs, 1, 2).reshape(B, H, ns, SEL_BLOCK, dh)
    kw_pad = jnp.pad(kw, ((0, 0), (WINDOW, 0), (0, 0), (0, 0)))
    vw_pad = jnp.pad(vw, ((0, 0), (WINDOW, 0), (0, 0), (0, 0)))
    bi = jnp.arange(B)[:, None, None, None]
    hi = jnp.arange(H)[None, :, None, None]
    j = jnp.arange(ns)

    def block(args):
        i, qb, qrb, gb = args
        t = i * Q_BLOCK + jnp.arange(Q_BLOCK)
        s_c = jnp.einsum('bqhgd,bchd->bhgqc', qb, kc) * scale
        p_c = masked_softmax(s_c, cmp_end[None, :] <= t[:, None])
        o_c = jnp.einsum('bhgqc,bchd->bqhgd', p_c.astype(vc.dtype), vc)
        imp = jnp.einsum('bhgqc,cj->bhqj', p_c, cmp_to_sel)
        cur = t // SEL_BLOCK
        forced = (j[None, :] == 0) | (j[None, :] == cur[:, None]) | (j[None, :] == cur[:, None] - 1)
        imp = jnp.where(j[None, :] > cur[:, None], -SEL_FORCE, jnp.where(forced, SEL_FORCE, imp))
        _, sel = lax.top_k(imp, n_sel)
        m = n_sel * SEL_BLOCK
        ks_g = ks_blk[bi, hi, sel].reshape(B, H, Q_BLOCK, m, dh)
        vs_g = vs_blk[bi, hi, sel].reshape(B, H, Q_BLOCK, m, dh)
        kpos_s = (sel[..., None] * SEL_BLOCK + jnp.arange(SEL_BLOCK)).reshape(B, H, 1, Q_BLOCK, m)
        s_s = jnp.einsum('bqhgd,bhqmd->bhgqm', qrb, ks_g) * scale
        p_s = masked_softmax(s_s, kpos_s <= t[:, None])
        o_s = jnp.einsum('bhgqm,bhqmd->bqhgd', p_s.astype(vs.dtype), vs_g)
        start = i * Q_BLOCK
        kwb = lax.dynamic_slice_in_dim(kw_pad, start, WINDOW + Q_BLOCK, axis=1)
        vwb = lax.dynamic_slice_in_dim(vw_pad, start, WINDOW + Q_BLOCK, axis=1)
        kpos_w = start - WINDOW + jnp.arange(WINDOW + Q_BLOCK)
        dist = t[:, None] - kpos_w[None, :]
        mask_w = (dist >= 0) & (dist < WINDOW) & (kpos_w[None, :] >= 0)
        s_w = jnp.einsum('bqhgd,bkhd->bhgqk', qrb, kwb) * scale
        p_w = masked_softmax(s_w, mask_w)
        o_w = jnp.einsum('bhgqk,bkhd->bqhgd', p_w.astype(vw.dtype), vwb)
        g = jax.nn.sigmoid(gb.astype(jnp.float32)).astype(qb.dtype)
        return g[..., 0:1] * o_c + g[..., 1:2] * o_s + g[..., 2:3] * o_w

    out = lax.map(block, (jnp.arange(nblk), to_blocks(q), to_blocks(q_rot), to_blocks(gate)))
    return jnp.moveaxis(out, 0, 1).reshape(B, T, H * G * dh)


def short_conv(b_gate, c_gate, h, w):
    u = c_gate * h
    y = lax.conv_general_dilated(u, w[:, None, :].astype(u.dtype), window_strides=(1,),
                                 padding=[(CONV_K - 1, 0)], dimension_numbers=('NWC', 'WIO', 'NWC'),
                                 feature_group_count=u.shape[-1])
    return b_gate * y


def hgrn2(q, f_logit, i_in, g, lb, norm_g):
    B, T, H, d = q.shape
    dt = q.dtype
    z = f_logit.astype(jnp.float32)
    lb32 = lb.astype(jnp.float32)
    log_f = jnp.logaddexp(jnp.log(jnp.maximum(lb32, LB_FLOOR)), jnp.log1p(-lb32) + jax.nn.log_sigmoid(z))
    k = (1.0 - lb32) * jax.nn.sigmoid(-z)
    qs = q.astype(jnp.float32) * d ** -0.5
    v = i_in.astype(jnp.float32)
    nch = T // HGRN_CHUNK

    def chunks(a):
        return a.reshape(B, nch, HGRN_CHUNK, H, d).transpose(1, 0, 3, 2, 4)

    causal = jnp.tril(jnp.ones((HGRN_CHUNK, HGRN_CHUNK), dtype=bool))[:, :, None]

    def step(S, inp):
        qc, kc, vc, lfc = inp
        b = jnp.cumsum(lfc, axis=2)
        diff = b[:, :, :, None, :] - b[:, :, None, :, :]
        decay = jnp.where(causal, jnp.exp(jnp.where(causal, diff, 0.0)), 0.0)
        a = jnp.einsum('bhtd,bhsd,bhtsd->bhts', qc, kc, decay)
        o = jnp.einsum('bhts,bhse->bhte', a, vc) + jnp.einsum('bhtd,bhde->bhte', qc * jnp.exp(b), S)
        b_last = b[:, :, -1:, :]
        S = jnp.exp(b_last[:, :, 0, :])[..., None] * S + jnp.einsum('bhsd,bhse->bhde', kc * jnp.exp(b_last - b), vc)
        return S, o

    S0 = jnp.zeros((B, H, d, d), jnp.float32)
    _, o = lax.scan(step, S0, (chunks(qs), chunks(k), chunks(v), chunks(log_f)))
    o = o.transpose(1, 0, 3, 2, 4).reshape(B, T, H, d)
    o = o * lax.rsqrt(jnp.mean(o * o, axis=-1, keepdims=True) + RMS_EPS) * norm_g.astype(jnp.float32)
    o = o * jax.nn.silu(g.astype(jnp.float32))
    return o.reshape(B, T, H * d).astype(dt)


def cross_attention(xn, mem_n, wq, wkv, wo):
    B, T, _ = xn.shape
    M = mem_n.shape[1]
    q = jnp.dot(xn, wq).reshape(B, T, CROSS_HEADS, HEAD_DIM)
    k, v = jnp.split(jnp.dot(mem_n, wkv).reshape(B, M, 2 * CROSS_HEADS, HEAD_DIM), 2, axis=2)
    s = jnp.einsum('bqhd,bkhd->bhqk', q, k) * HEAD_DIM ** -0.5
    p = jax.nn.softmax(s.astype(jnp.float32), axis=-1).astype(v.dtype)
    o = jnp.einsum('bhqk,bkhd->bqhd', p, v).reshape(B, T, CROSS_WIDTH)
    return jnp.dot(o, wo)


def hier_moe(xn, w_rg, b_rg, w_re, b_re, w_gu, w_dn):
    B, T, D = xn.shape
    n_tok = B * T
    xf = xn.reshape(n_tok, D)
    lg = jnp.dot(xf, w_rg).astype(jnp.float32) + b_rg.astype(jnp.float32)
    p_grp = jax.nn.softmax(lg, axis=-1)
    grp = jnp.argmax(lg, axis=-1)
    p_grp_sel = jnp.take_along_axis(p_grp, grp[:, None], axis=1)
    le = (jnp.dot(xf, w_re).astype(jnp.float32) + b_re.astype(jnp.float32)).reshape(n_tok, N_GROUPS, EXPERTS_PER_GROUP)
    le = jnp.take_along_axis(le, grp[:, None, None], axis=1)[:, 0]
    top_p, top_e = lax.top_k(jax.nn.softmax(le, axis=-1), EXPERT_TOP_K)
    gate = p_grp_sel * top_p / jnp.sum(top_p, axis=-1, keepdims=True)
    eid = (grp[:, None] * EXPERTS_PER_GROUP + top_e).reshape(-1)
    tok = jnp.arange(n_tok * EXPERT_TOP_K) // EXPERT_TOP_K
    order = jnp.argsort(eid)
    tok_s = tok[order]
    sizes = jnp.bincount(eid, length=N_EXPERTS).astype(jnp.int32)
    xs = xf[tok_s]
    gu = lax.ragged_dot(xs, w_gu, sizes)
    g_, u_ = jnp.split(gu, 2, axis=-1)
    ys = lax.ragged_dot(jax.nn.silu(g_) * u_, w_dn, sizes)
    ys = ys * gate.reshape(-1)[order][:, None].astype(ys.dtype)
    return jnp.zeros_like(xf).at[tok_s].add(ys).reshape(B, T, D)


def setup_inputs(seed: int = 0) -> dict:
    key = jax.random.key(seed)
    ks = jax.random.split(key, 24)
    f32 = jnp.float32

    def nrm(k, shape, scale):
        return jax.random.normal(k, shape, f32) * scale

    def gain(k, shape):
        return 1.0 + 0.1 * jax.random.normal(k, shape, f32)

    D = D_MODEL
    return {
        'x': nrm(ks[0], (BATCH, SEQ, D), 1.0),
        'mem': nrm(ks[1], (BATCH, MEM_LEN, D), 1.0),
        'w_in': nrm(ks[2], (DEPTH, D, IN_WIDTH), D ** -0.5),
        'conv_w': nrm(ks[3], (DEPTH, CONV_K, CONV_WIDTH), CONV_K ** -0.5),
        'cmp_pe': nrm(ks[4], (DEPTH, 2, CMP_LEN, HEAD_DIM), 0.02),
        'cmp_w1': nrm(ks[5], (DEPTH, 2, CMP_LEN * HEAD_DIM, CMP_HIDDEN), (CMP_LEN * HEAD_DIM) ** -0.5),
        'cmp_w2': nrm(ks[6], (DEPTH, 2, CMP_HIDDEN, HEAD_DIM), CMP_HIDDEN ** -0.5),
        'hgrn_lb_logits': nrm(ks[7], (DEPTH, HGRN_WIDTH), 1.0),
        'hgrn_norm': gain(ks[8], (DEPTH, HGRN_WIDTH)),
        'w_out': nrm(ks[9], (DEPTH, MIX_WIDTH, D), MIX_WIDTH ** -0.5),
        'cross_wq': nrm(ks[10], (DEPTH, D, CROSS_WIDTH), D ** -0.5),
        'cross_wkv': nrm(ks[11], (DEPTH, D, 2 * CROSS_WIDTH), D ** -0.5),
        'cross_wo': nrm(ks[12], (DEPTH, CROSS_WIDTH, D), CROSS_WIDTH ** -0.5),
        'router_group_w': nrm(ks[13], (DEPTH, D, N_GROUPS), D ** -0.5),
        'router_group_b': nrm(ks[14], (DEPTH, N_GROUPS), 0.01),
        'router_expert_w': nrm(ks[15], (DEPTH, D, N_EXPERTS), D ** -0.5),
        'router_expert_b': nrm(ks[16], (DEPTH, N_EXPERTS), 0.01),
        'expert_w_gate_up': nrm(ks[17], (DEPTH, N_EXPERTS, D, 2 * D_EXPERT), D ** -0.5),
        'expert_w_down': nrm(ks[18], (DEPTH, N_EXPERTS, D_EXPERT, D), D_EXPERT ** -0.5),
        'norm_mix': gain(ks[19], (DEPTH, D)),
        'norm_cross': gain(ks[20], (DEPTH, D)),
        'norm_mem': gain(ks[21], (DEPTH, D)),
        'norm_ffn': gain(ks[22], (DEPTH, D)),
        'norm_final': gain(ks[23], (D,)),
    }


def reference(x, mem, w_in, conv_w, cmp_pe, cmp_w1, cmp_w2, hgrn_lb_logits, hgrn_norm, w_out,
              cross_wq, cross_wkv, cross_wo, router_group_w, router_group_b, router_expert_w,
              router_expert_b, expert_w_gate_up, expert_w_down, norm_mix, norm_cross, norm_mem,
              norm_ffn, norm_final):
    B, T, _ = x.shape
    pos = jnp.arange(T)
    p_lb = jax.nn.softmax(hgrn_lb_logits.astype(jnp.float32), axis=0)
    lower_bounds = jnp.cumsum(p_lb, axis=0) - p_lb[0:1]
    h = x
    for l in range(DEPTH):
        xn = rmsnorm(h, norm_mix[l])
        (q, kc, vc, ksl, vsl, kwn, vwn, gl, cb, cc, ch,
         hq, hf, hi, hg) = jnp.split(jnp.dot(xn, w_in[l]), IN_OFFSETS, axis=-1)
        q = q.reshape(B, T, NSA_KV_HEADS, NSA_GROUP, HEAD_DIM)
        q_rot = rope(q, pos)
        kvh = (B, T, NSA_KV_HEADS, HEAD_DIM)
        kc = nsa_compress(kc.reshape(kvh), cmp_pe[l, 0], cmp_w1[l, 0], cmp_w2[l, 0])
        vc = nsa_compress(vc.reshape(kvh), cmp_pe[l, 1], cmp_w1[l, 1], cmp_w2[l, 1])
        ksl = rope(ksl.reshape(kvh), pos)
        kwn = rope(kwn.reshape(kvh), pos)
        gate = gl.reshape(B, T, NSA_KV_HEADS, NSA_GROUP, 3)
        o_nsa = nsa_attention(q, q_rot, gate, kc, vc, ksl, vsl.reshape(kvh), kwn, vwn.reshape(kvh))
        o_conv = short_conv(cb, cc, ch, conv_w[l])
        hh = (B, T, HGRN_HEADS, HEAD_DIM)
        o_hgrn = hgrn2(hq.reshape(hh), hf.reshape(hh), hi.reshape(hh), hg.reshape(hh),
                       lower_bounds[l].reshape(HGRN_HEADS, HEAD_DIM), hgrn_norm[l].reshape(HGRN_HEADS, HEAD_DIM))
        h = h + jnp.dot(jnp.concatenate([o_nsa, o_conv, o_hgrn], axis=-1), w_out[l])
        h = h + cross_attention(rmsnorm(h, norm_cross[l]), rmsnorm(mem, norm_mem[l]),
                                cross_wq[l], cross_wkv[l], cross_wo[l])
        h = h + hier_moe(rmsnorm(h, norm_ffn[l]), router_group_w[l], router_group_b[l],
                         router_expert_w[l], router_expert_b[l], expert_w_gate_up[l], expert_w_down[l])
    return rmsnorm(h, norm_final)
```

```python
import functools

import numpy as np
import jax
import jax.numpy as jnp
from jax import lax
from jax.experimental import pallas as pl
from jax.experimental.pallas import tpu as pltpu

F32 = jnp.float32
BF16 = jnp.bfloat16

D_MODEL = 1024
HEAD_DIM = 64
NSA_HEADS = 8
NSA_KV_HEADS = 2
NSA_GROUP = NSA_HEADS // NSA_KV_HEADS
NSA_WIDTH = NSA_HEADS * HEAD_DIM
KV_WIDTH = NSA_KV_HEADS * HEAD_DIM
CMP_LEN = 32
CMP_STRIDE = 16
CMP_HIDDEN = 2 * HEAD_DIM
SEL_BLOCK = 64
N_SELECT = 16
WINDOW = 512
Q_BLOCK = 128
MASK_VALUE = -1e30
SEL_FORCE = 1e6
CONV_WIDTH = 256
CONV_K = 3
HGRN_HEADS = 4
HGRN_WIDTH = HGRN_HEADS * HEAD_DIM
HGRN_CHUNK = 64
LB_FLOOR = 1e-30
ROPE_THETA = 10000.0
CROSS_HEADS = 4
CROSS_WIDTH = CROSS_HEADS * HEAD_DIM
N_GROUPS = 4
EXPERTS_PER_GROUP = 8
N_EXPERTS = N_GROUPS * EXPERTS_PER_GROUP
D_EXPERT = 512
RMS_EPS = 1e-6
QK_SCALE = HEAD_DIM ** -0.5

LANES = 128
SUBLANES = 8
VMEM_LIMIT_BYTES = 48 * 1024 * 1024

EXPERT_LANE0 = 32
INFO_E0, INFO_E1, INFO_R0, INFO_R1, INFO_G0, INFO_G1 = 0, 1, 2, 3, 4, 5


def _dot(a, b):
    return jnp.dot(a, b, preferred_element_type=F32)


def _dot_nt(a, b):
    return lax.dot_general(a, b, (((1,), (1,)), ((), ())), preferred_element_type=F32)


def _split_bf16(x):
    hi = x.astype(BF16)
    lo = (x - hi.astype(F32)).astype(BF16)
    return hi, lo


def _rmsnorm(x, g):
    return x * lax.rsqrt(jnp.mean(x * x, axis=-1, keepdims=True) + RMS_EPS) * g


def _params(*sem):
    return pltpu.CompilerParams(dimension_semantics=sem, vmem_limit_bytes=VMEM_LIMIT_BYTES)


def _full(shape):
    nd = len(shape)
    return pl.BlockSpec(shape, lambda *_: (0,) * nd)


def _rope(x, cos, sin_a, sin_b):
    width = x.shape[1]
    reps = width // LANES
    if reps > 1:
        cos = jnp.concatenate([cos] * reps, axis=1)
        sin_a = jnp.concatenate([sin_a] * reps, axis=1)
        sin_b = jnp.concatenate([sin_b] * reps, axis=1)
    half = HEAD_DIM // 2
    return x * cos + pltpu.roll(x, width - half, 1) * sin_a + pltpu.roll(x, half, 1) * sin_b


def _inproj_kernel(h_ref, g_ref, cos_ref, sa_ref, sb_ref, wq_ref, wkv_ref, wg_ref, wc_ref, wh_ref, cw_ref,
                   q_ref, qr_ref, kc_ref, vc_ref, ks_ref, vs_ref, kw_ref, vw_ref, gs_ref, oc_ref, hg_ref,
                   carry_ref):
    tm = h_ref.shape[1]

    @pl.when(pl.program_id(1) == 0)
    def _():
        carry_ref[...] = jnp.zeros_like(carry_ref)

    xb = _rmsnorm(h_ref[0], g_ref[...]).astype(BF16)
    cos, sa, sb = cos_ref[...], sa_ref[...], sb_ref[...]

    q = _dot(xb, wq_ref[...]) * QK_SCALE
    q_ref[0] = q.astype(BF16)
    qr_ref[0] = _rope(q, cos, sa, sb).astype(BF16)

    kv = _dot(xb, wkv_ref[...])
    kw_ = KV_WIDTH
    kc_ref[0] = kv[:, 0 * kw_:1 * kw_].astype(BF16)
    vc_ref[0] = kv[:, 1 * kw_:2 * kw_].astype(BF16)
    ks_ref[0] = _rope(kv[:, 2 * kw_:3 * kw_], cos, sa, sb).astype(BF16)
    vs_ref[0] = kv[:, 3 * kw_:4 * kw_].astype(BF16)
    kw_ref[0] = _rope(kv[:, 4 * kw_:5 * kw_], cos, sa, sb).astype(BF16)
    vw_ref[0] = kv[:, 5 * kw_:6 * kw_].astype(BF16)

    gl = _dot(xb, wg_ref[...])
    gs_ref[0] = 1.0 / (1.0 + jnp.exp(-gl))

    cv = _dot(xb, wc_ref[...])
    cb = cv[:, 0:CONV_WIDTH]
    u = cv[:, CONV_WIDTH:2 * CONV_WIDTH] * cv[:, 2 * CONV_WIDTH:3 * CONV_WIDTH]
    ext = jnp.concatenate([carry_ref[...], u], axis=0)
    u1 = pltpu.roll(ext, 1, 0)[SUBLANES:]
    u2 = pltpu.roll(ext, 2, 0)[SUBLANES:]
    cw = cw_ref[...]
    oc_ref[0] = (cb * (cw[0:1] * u2 + cw[1:2] * u1 + cw[2:3] * u)).astype(BF16)
    carry_ref[...] = u[tm - SUBLANES:tm]

    hg_ref[0] = _dot(xb, wh_ref[...])


def _inproj(h, gain, cos, sin_a, sin_b, wq, wkv, wg, wc, wh, cw, tm):
    B, T, D = h.shape
    nt = T // tm
    tok = lambda w: pl.BlockSpec((1, tm, w), lambda b, t: (b, t, 0))
    tab = pl.BlockSpec((tm, LANES), lambda b, t: (t, 0))
    out_w = [(NSA_WIDTH, BF16), (NSA_WIDTH, BF16)] + [(KV_WIDTH, BF16)] * 6 + \
            [(2 * LANES, F32), (CONV_WIDTH, BF16), (4 * HGRN_WIDTH, F32)]
    return pl.pallas_call(
        _inproj_kernel,
        grid=(B, nt),
        in_specs=[tok(D), _full(gain.shape), tab, tab, tab, _full(wq.shape), _full(wkv.shape), _full(wg.shape),
                  _full(wc.shape), _full(wh.shape), _full(cw.shape)],
        out_specs=[tok(w) for w, _ in out_w],
        out_shape=[jax.ShapeDtypeStruct((B, T, w), dt) for w, dt in out_w],
        scratch_shapes=[pltpu.VMEM((SUBLANES, CONV_WIDTH), F32)],
        compiler_params=_params("arbitrary", "arbitrary"),
        name="inproj",
    )(h, gain, cos, sin_a, sin_b, wq, wkv, wg, wc, wh, cw)


def _compress_kernel(x_ref, pe_ref, w1_ref, w2_ref, o_ref):
    x = x_ref[0, 0, 0]
    n = x.shape[0]
    half = x.shape[1]
    w1 = w1_ref[0]
    a = _dot(x, w1[:half])
    b = _dot(x, w1[half:])
    pe = jnp.broadcast_to(pe_ref[0], (SUBLANES, 2 * half)).astype(BF16)
    pe_term = _dot(pe, w1)[0:1]
    hid = a + pltpu.roll(b, n - 1, 0) + pe_term
    act = hid / (1.0 + jnp.exp(-hid))
    o_ref[0, 0, 0] = _dot(act.astype(BF16), w2_ref[0]).astype(BF16)


def _compress(xkv, pe, w1, w2):
    _, B, H, n, width = xkv.shape
    return pl.pallas_call(
        _compress_kernel,
        grid=(2, B, H),
        in_specs=[pl.BlockSpec((1, 1, 1, n, width), lambda s, b, h: (s, b, h, 0, 0)),
                  pl.BlockSpec((1, 1, 2 * width), lambda s, b, h: (s, 0, 0)),
                  pl.BlockSpec((1, 2 * width, CMP_HIDDEN), lambda s, b, h: (s, 0, 0)),
                  pl.BlockSpec((1, CMP_HIDDEN, HEAD_DIM), lambda s, b, h: (s, 0, 0))],
        out_specs=pl.BlockSpec((1, 1, 1, n, HEAD_DIM), lambda s, b, h: (s, b, h, 0, 0)),
        out_shape=jax.ShapeDtypeStruct((2, B, H, n, HEAD_DIM), BF16),
        compiler_params=_params("arbitrary", "arbitrary", "arbitrary"),
        name="compress",
    )(xkv, pe, w1, w2)


def _masked_softmax(s, mask):
    s = jnp.where(mask, s, MASK_VALUE)
    m = jnp.max(s, axis=1, keepdims=True)
    e = jnp.where(mask, jnp.exp(s - m), 0.0)
    den = jnp.sum(e, axis=1, keepdims=True)
    return e / jnp.where(den > 0, den, 1.0)


def _nsa_kernel(q_ref, qr_ref, kc_ref, vc_ref, ks_ref, vs_ref, kw_ref, vw_ref, gs_ref, c2s_ref, o_ref, *, key_tile):
    i = pl.program_id(2)
    qb, g_, dh = Q_BLOCK, NSA_GROUP, HEAD_DIM
    rows = g_ * qb
    ns = c2s_ref.shape[1]
    n_sel = min(N_SELECT, ns)

    def stack(ref):
        x = ref[0]
        return jnp.concatenate([x[:, g * dh:(g + 1) * dh] for g in range(g_)], axis=0)

    q4, qr4 = stack(q_ref), stack(qr_ref)
    t_row = i * qb + (lax.broadcasted_iota(jnp.int32, (rows, 1), 0) & (qb - 1))

    kc, vc = kc_ref[0, 0, 0], vc_ref[0, 0, 0]
    nc = kc.shape[0]
    c_end = lax.broadcasted_iota(jnp.int32, (1, nc), 1) * CMP_STRIDE + (CMP_LEN - 1)
    p_c = _masked_softmax(_dot_nt(q4, kc), c_end <= t_row)
    o_c = _dot(p_c.astype(BF16), vc)
    p_g = p_c[0:qb]
    for g in range(1, g_):
        p_g = p_g + p_c[g * qb:(g + 1) * qb]
    imp = _dot(p_g.astype(BF16), c2s_ref[...])

    t_q = i * qb + lax.broadcasted_iota(jnp.int32, (qb, 1), 0)
    cur = lax.shift_right_logical(t_q, 6)
    j = lax.broadcasted_iota(jnp.int32, (qb, ns), 1)
    forced = (j == 0) | (j == cur) | (j == cur - 1)
    work = jnp.where(j > cur, -SEL_FORCE, jnp.where(forced, SEL_FORCE, imp))
    jf = j.astype(F32)
    sel = jnp.zeros((qb, ns), F32)
    for _ in range(n_sel):
        mx = jnp.max(work, axis=1, keepdims=True)
        first = jnp.min(jnp.where(work == mx, jf, float(ns)), axis=1, keepdims=True)
        pick = jf == first
        sel = jnp.where(pick, 1.0, sel)
        work = jnp.where(pick, -jnp.inf, work)
    sel_bias = jnp.where(sel > 0, 0.0, MASK_VALUE).astype(BF16)

    blocks_per_tile = key_tile // SEL_BLOCK
    blk_row = lax.broadcasted_iota(jnp.int32, (ns, key_tile), 0)
    blk_col = lax.shift_right_logical(lax.broadcasted_iota(jnp.int32, (ns, key_tile), 1), 6)
    key_col = lax.broadcasted_iota(jnp.int32, (1, key_tile), 1)

    def sel_step(kt, carry):
        m, l, acc = carry
        start = pl.multiple_of(kt * key_tile, key_tile)
        k = ks_ref[0, 0, pl.ds(start, key_tile), :]
        v = vs_ref[0, 0, pl.ds(start, key_tile), :]
        expand = jnp.where(blk_row == blk_col + kt * blocks_per_tile, 1.0, 0.0).astype(BF16)
        bias = _dot(sel_bias, expand)
        s = _dot_nt(qr4, k) + jnp.concatenate([bias] * g_, axis=0)
        s = jnp.where(start + key_col <= t_row, s, MASK_VALUE)
        m_new = jnp.maximum(m, jnp.max(s, axis=1, keepdims=True))
        alpha = jnp.exp(m - m_new)
        p = jnp.exp(s - m_new)
        l = alpha * l + jnp.sum(p, axis=1, keepdims=True)
        acc = alpha * acc + _dot(p.astype(BF16), v)
        return m_new, l, acc

    n_tiles = (i * qb) // key_tile + 1
    init = (jnp.full((rows, 1), MASK_VALUE, F32), jnp.zeros((rows, 1), F32), jnp.zeros((rows, dh), F32))
    _, l_s, acc_s = lax.fori_loop(0, n_tiles, sel_step, init)
    o_s = acc_s / jnp.where(l_s > 0, l_s, 1.0)

    wk = WINDOW + qb
    w_start = pl.multiple_of(jnp.maximum(i * qb - WINDOW, 0), qb)
    kwin = kw_ref[0, 0, pl.ds(w_start, wk), :]
    vwin = vw_ref[0, 0, pl.ds(w_start, wk), :]
    dist = t_row - (w_start + lax.broadcasted_iota(jnp.int32, (1, wk), 1))
    p_w = _masked_softmax(_dot_nt(qr4, kwin), (dist >= 0) & (dist < WINDOW))
    o_w = _dot(p_w.astype(BF16), vwin)

    gs = gs_ref[0]
    outs = []
    for g in range(g_):
        r = slice(g * qb, (g + 1) * qb)
        outs.append(gs[:, 3 * g:3 * g + 1] * o_c[r] + gs[:, 3 * g + 1:3 * g + 2] * o_s[r]
                    + gs[:, 3 * g + 2:3 * g + 3] * o_w[r])
    o_ref[0] = jnp.concatenate(outs, axis=1).astype(o_ref.dtype)


def _nsa(q, qr, kvc, ks, vs, kw, vw, gs, c2s):
    B, T, _ = q.shape
    H = NSA_KV_HEADS
    nc = kvc.shape[3]
    key_tile = min(512, T)
    gw = NSA_GROUP * HEAD_DIM
    qspec = pl.BlockSpec((1, Q_BLOCK, gw), lambda b, h, i: (b, i, h))
    seq = pl.BlockSpec((1, 1, T, HEAD_DIM), lambda b, h, i: (b, h, 0, 0))
    cmp_spec = lambda s: pl.BlockSpec((1, 1, 1, nc, HEAD_DIM), lambda b, h, i: (s, b, h, 0, 0))
    return pl.pallas_call(
        functools.partial(_nsa_kernel, key_tile=key_tile),
        grid=(B, H, T // Q_BLOCK),
        in_specs=[qspec, qspec, cmp_spec(0), cmp_spec(1), seq, seq, seq, seq,
                  pl.BlockSpec((1, Q_BLOCK, LANES), lambda b, h, i: (b, i, h)), _full(c2s.shape)],
        out_specs=qspec,
        out_shape=jax.ShapeDtypeStruct((B, T, NSA_WIDTH), BF16),
        compiler_params=_params("arbitrary", "arbitrary", "arbitrary"),
        name="nsa",
    )(q, qr, kvc, kvc, ks, vs, kw, vw, gs, c2s)


def _hgrn_kernel(x_ref, lbp_ref, ng_ref, o_ref, st_ref):
    w = HGRN_WIDTH
    L = HGRN_CHUNK
    n_chunks = x_ref.shape[1] // L

    @pl.when(pl.program_id(1) == 0)
    def _():
        st_ref[...] = jnp.zeros_like(st_ref)

    log_lb, log_1m_lb, one_m_lb = lbp_ref[0:1], lbp_ref[1:2], lbp_ref[2:3]
    ng = ng_ref[...]
    ri = lax.broadcasted_iota(jnp.int32, (L, L), 0)
    ci = lax.broadcasted_iota(jnp.int32, (L, L), 1)
    tril = jnp.where(ci <= ri, 1.0, 0.0).astype(BF16)
    hr = lax.shift_right_logical(lax.broadcasted_iota(jnp.int32, (w, w), 0), 6)
    hc = lax.shift_right_logical(lax.broadcasted_iota(jnp.int32, (w, w), 1), 6)
    same_head = hr == hc
    head_ones = jnp.where(same_head, 1.0, 0.0).astype(BF16)
    row = lax.broadcasted_iota(jnp.int32, (L, 1), 0)

    def chunk(c, _):
        r0 = pl.multiple_of(c * L, L)
        q = x_ref[0, pl.ds(r0, L), 0 * w:1 * w] * QK_SCALE
        z = x_ref[0, pl.ds(r0, L), 1 * w:2 * w]
        v = x_ref[0, pl.ds(r0, L), 2 * w:3 * w]
        g = x_ref[0, pl.ds(r0, L), 3 * w:4 * w]

        log_sig = jnp.minimum(z, 0.0) - jnp.log(1.0 + jnp.exp(-jnp.abs(z)))
        bb = log_1m_lb + log_sig
        log_f = jnp.maximum(log_lb, bb) + jnp.log(1.0 + jnp.exp(-jnp.abs(log_lb - bb)))
        k = one_m_lb / (1.0 + jnp.exp(z))

        lf_hi, lf_lo = _split_bf16(log_f)
        b = _dot(tril, lf_hi) + _dot(tril, lf_lo)
        b_last = b[L - 1:L]
        st = st_ref[...]
        o = _dot_nt((q * jnp.exp(b)).astype(BF16), st.astype(BF16))
        for d in range(L):
            if d == 0:
                wgt = q * k
                vs = v
            else:
                ok = row >= d
                diff = jnp.where(ok, b - pltpu.roll(b, d, 0), 0.0)
                wgt = jnp.where(ok, q * pltpu.roll(k, d, 0) * jnp.exp(diff), 0.0)
                vs = pltpu.roll(v, d, 0)
            o = o + _dot(wgt.astype(BF16), head_ones) * vs
        kd = k * jnp.exp(b_last - b)
        upd = _dot(v.T.astype(BF16), kd.astype(BF16))
        st_ref[...] = st * jnp.exp(b_last) + jnp.where(same_head, upd, 0.0)

        o2_hi, o2_lo = _split_bf16(o * o)
        ms = (_dot(o2_hi, head_ones) + _dot(o2_lo, head_ones)) * (1.0 / HEAD_DIM)
        y = o * lax.rsqrt(ms + RMS_EPS) * ng
        o_ref[0, pl.ds(r0, L), :] = (y * (g / (1.0 + jnp.exp(-g)))).astype(o_ref.dtype)
        return 0

    lax.fori_loop(0, n_chunks, chunk, 0)


def _hgrn(hg, lb_params, norm_g, rows):
    B, T, width = hg.shape
    return pl.pallas_call(
        _hgrn_kernel,
        grid=(B, T // rows),
        in_specs=[pl.BlockSpec((1, rows, width), lambda b, t: (b, t, 0)), _full(lb_params.shape), _full(norm_g.shape)],
        out_specs=pl.BlockSpec((1, rows, HGRN_WIDTH), lambda b, t: (b, t, 0)),
        out_shape=jax.ShapeDtypeStruct((B, T, HGRN_WIDTH), BF16),
        scratch_shapes=[pltpu.VMEM((HGRN_WIDTH, HGRN_WIDTH), F32)],
        compiler_params=_params("arbitrary", "arbitrary"),
        name="hgrn",
    )(hg, lb_params, norm_g)


def _memkv_kernel(m_ref, g_ref, w_ref, k_ref, v_ref):
    kv = _dot(_rmsnorm(m_ref[0], g_ref[...]).astype(BF16), w_ref[...])
    k_ref[0] = kv[:, :CROSS_WIDTH].astype(BF16)
    v_ref[0] = kv[:, CROSS_WIDTH:].astype(BF16)


def _memkv(mem, gain, wkv):
    B, M, D = mem.shape
    out = pl.BlockSpec((1, M, CROSS_WIDTH), lambda b: (b, 0, 0))
    return pl.pallas_call(
        _memkv_kernel,
        grid=(B,),
        in_specs=[pl.BlockSpec((1, M, D), lambda b: (b, 0, 0)), _full(gain.shape), _full(wkv.shape)],
        out_specs=[out, out],
        out_shape=[jax.ShapeDtypeStruct((B, M, CROSS_WIDTH), BF16)] * 2,
        compiler_params=_params("arbitrary"),
        name="memkv",
    )(mem, gain, wkv)


def _post_kernel(h_ref, on_ref, oc_ref, oh_ref, wo_ref, gc_ref, wq_ref, km_ref, vm_ref, wco_ref, gf_ref,
                 wr_hi_ref, wr_lo_ref, br_ref, ltri_ref, h2_ref, info_ref, cnt_ref, carry_ref):
    tm = h_ref.shape[1]
    first = (pl.program_id(0) == 0) & (pl.program_id(1) == 0)

    @pl.when(first)
    def _():
        carry_ref[...] = jnp.zeros_like(carry_ref)

    a, b = NSA_WIDTH, NSA_WIDTH + CONV_WIDTH
    mix = _dot(on_ref[0], wo_ref[0:a]) + _dot(oc_ref[0], wo_ref[a:b]) + _dot(oh_ref[0], wo_ref[b:])
    h1 = h_ref[0] + mix

    q = _dot(_rmsnorm(h1, gc_ref[...]).astype(BF16), wq_ref[...]) * QK_SCALE
    km, vm = km_ref[0], vm_ref[0]
    lane_head = lax.shift_right_logical(lax.broadcasted_iota(jnp.int32, (1, CROSS_WIDTH), 1), 6)
    o = jnp.zeros((tm, CROSS_WIDTH), F32)
    for hd in range(CROSS_HEADS):
        in_head = lane_head == hd
        s = _dot_nt(jnp.where(in_head, q, 0.0).astype(BF16), km)
        m = jnp.max(s, axis=1, keepdims=True)
        e = jnp.exp(s - m)
        p = e / jnp.sum(e, axis=1, keepdims=True)
        o = o + _dot(p.astype(BF16), jnp.where(in_head, vm, jnp.zeros_like(vm)))
    h2 = h1 + _dot(o.astype(BF16), wco_ref[...])
    h2_ref[0] = h2

    xn = _rmsnorm(h2, gf_ref[...])
    x_hi, x_lo = _split_bf16(xn)
    w_hi = wr_hi_ref[...]
    logits = _dot(x_hi, w_hi) + _dot(x_lo, w_hi) + _dot(x_hi, wr_lo_ref[...]) + br_ref[...]
    lane = lax.broadcasted_iota(jnp.int32, (tm, LANES), 1)
    lanef = lane.astype(F32)
    big = float(LANES)

    lg = jnp.where(lane < N_GROUPS, logits, -jnp.inf)
    mg = jnp.max(lg, axis=1, keepdims=True)
    p_grp_sel = 1.0 / jnp.sum(jnp.exp(lg - mg), axis=1, keepdims=True)
    grp = jnp.min(jnp.where(lg == mg, lanef, big), axis=1, keepdims=True)

    lo_lane = EXPERT_LANE0 + EXPERTS_PER_GROUP * grp
    in_grp = (lanef >= lo_lane) & (lanef < lo_lane + EXPERTS_PER_GROUP)
    le = jnp.where(in_grp, logits, -jnp.inf)
    me = jnp.max(le, axis=1, keepdims=True)
    ee = jnp.exp(le - me)
    pe = jnp.where(in_grp, ee / jnp.sum(ee, axis=1, keepdims=True), -1.0)
    p1 = jnp.max(pe, axis=1, keepdims=True)
    l1 = jnp.min(jnp.where(pe == p1, lanef, big), axis=1, keepdims=True)
    pe2 = jnp.where(lanef == l1, -1.0, pe)
    p2 = jnp.max(pe2, axis=1, keepdims=True)
    l2 = jnp.min(jnp.where(pe2 == p2, lanef, big), axis=1, keepdims=True)
    scale = p_grp_sel / (p1 + p2)
    g0, g1 = p1 * scale, p2 * scale

    oh0 = jnp.where(lanef == l1, 1.0, 0.0)
    oh1 = jnp.where(lanef == l2, 1.0, 0.0)
    ltri = ltri_ref[...]
    c0, c1 = carry_ref[0:1], carry_ref[1:2]
    r0 = jnp.sum((_dot(ltri, oh0.astype(BF16)) + c0) * oh0, axis=1, keepdims=True)
    r1 = jnp.sum((_dot(ltri, oh1.astype(BF16)) + c1) * oh1, axis=1, keepdims=True)
    c0 = c0 + jnp.sum(oh0, axis=0, keepdims=True)
    c1 = c1 + jnp.sum(oh1, axis=0, keepdims=True)
    carry_ref[0:1] = c0
    carry_ref[1:2] = c1
    cnt_ref[...] = carry_ref[...]

    info = jnp.zeros((tm, LANES), F32)
    for col, val in ((INFO_E0, l1 - EXPERT_LANE0), (INFO_E1, l2 - EXPERT_LANE0), (INFO_R0, r0), (INFO_R1, r1),
                     (INFO_G0, g0), (INFO_G1, g1)):
        info = jnp.where(lane == col, val, info)
    info_ref[0] = info


def _post(h, o_nsa, o_conv, o_hgrn, w_out, g_cross, wq, kmem, vmem, wco, g_ffn, wr_hi, wr_lo, br, ltri, tm):
    B, T, D = h.shape
    M = kmem.shape[1]
    tok = lambda w: pl.BlockSpec((1, tm, w), lambda b, t: (b, t, 0))
    memspec = pl.BlockSpec((1, M, CROSS_WIDTH), lambda b, t: (b, 0, 0))
    return pl.pallas_call(
        _post_kernel,
        grid=(B, T // tm),
        in_specs=[tok(D), tok(NSA_WIDTH), tok(CONV_WIDTH), tok(HGRN_WIDTH), _full(w_out.shape), _full(g_cross.shape),
                  _full(wq.shape), memspec, memspec, _full(wco.shape), _full(g_ffn.shape), _full(wr_hi.shape),
                  _full(wr_lo.shape), _full(br.shape), _full(ltri.shape)],
        out_specs=[tok(D), tok(LANES), _full((SUBLANES, LANES))],
        out_shape=[jax.ShapeDtypeStruct((B, T, D), F32), jax.ShapeDtypeStruct((B, T, LANES), F32),
                   jax.ShapeDtypeStruct((SUBLANES, LANES), F32)],
        scratch_shapes=[pltpu.VMEM((SUBLANES, LANES), F32)],
        compiler_params=_params("arbitrary", "arbitrary"),
        name="post",
    )(h, o_nsa, o_conv, o_hgrn, w_out, g_cross, wq, kmem, vmem, wco, g_ffn, wr_hi, wr_lo, br, ltri)


def _row_copy(src, src_row, dst, dst_row, sem):
    return pltpu.make_async_copy(src.at[pl.ds(src_row, 1)], dst.at[pl.ds(dst_row, 1)], sem)


def _dispatch_kernel(d0_ref, d1_ref, h_ref, g_ref, xs_in_ref, xs_ref, xn_ref, sem):
    del xs_in_ref
    tm = h_ref.shape[0]
    xn_ref[...] = _rmsnorm(h_ref[...], g_ref[...])

    def start(r, _):
        _row_copy(xn_ref, r, xs_ref, d0_ref[0, 0, r], sem).start()
        _row_copy(xn_ref, r, xs_ref, d1_ref[0, 0, r], sem).start()
        return 0

    def wait(r, _):
        _row_copy(xn_ref, r, xs_ref, d0_ref[0, 0, r], sem).wait()
        _row_copy(xn_ref, r, xs_ref, d1_ref[0, 0, r], sem).wait()
        return 0

    lax.fori_loop(0, tm, start, 0)
    lax.fori_loop(0, tm, wait, 0)


def _dispatch(h2, g_ffn, dest0, dest1, xs_init, tm):
    N, D = h2.shape
    nt = N // tm
    idx = pl.BlockSpec((1, 1, tm), lambda i: (i, 0, 0), memory_space=pltpu.SMEM)
    return pl.pallas_call(
        _dispatch_kernel,
        grid=(nt,),
        in_specs=[idx, idx, pl.BlockSpec((tm, D), lambda i: (i, 0)), _full(g_ffn.shape),
                  pl.BlockSpec(memory_space=pl.ANY)],
        out_specs=pl.BlockSpec(memory_space=pl.ANY),
        out_shape=jax.ShapeDtypeStruct(xs_init.shape, F32),
        scratch_shapes=[pltpu.VMEM((tm, D), F32), pltpu.SemaphoreType.DMA(())],
        input_output_aliases={4: 0},
        compiler_params=_params("arbitrary"),
        name="dispatch",
    )(dest0.reshape(nt, 1, tm), dest1.reshape(nt, 1, tm), h2, g_ffn, xs_init)


def _experts_kernel(te_ref, nu_ref, x_ref, wgu_ref, wdn_ref, y_ref):
    del te_ref
    j = pl.program_id(0)

    @pl.when(j < nu_ref[0])
    def _():
        gu = _dot(x_ref[...].astype(BF16), wgu_ref[0])
        gate, up = gu[:, :D_EXPERT], gu[:, D_EXPERT:]
        act = gate / (1.0 + jnp.exp(-gate)) * up
        y_ref[...] = _dot(act.astype(BF16), wdn_ref[0])

    @pl.when(j >= nu_ref[0])
    def _():
        y_ref[...] = jnp.zeros_like(y_ref)


def _experts(xs, wgu, wdn, tile_expert, n_used, te):
    cap, D = xs.shape
    grid_spec = pltpu.PrefetchScalarGridSpec(
        num_scalar_prefetch=2,
        grid=(cap // te,),
        in_specs=[pl.BlockSpec((te, D), lambda j, te_, nu: (j, 0)),
                  pl.BlockSpec((1, D, 2 * D_EXPERT), lambda j, te_, nu: (te_[j], 0, 0)),
                  pl.BlockSpec((1, D_EXPERT, D), lambda j, te_, nu: (te_[j], 0, 0))],
        out_specs=pl.BlockSpec((te, D), lambda j, te_, nu: (j, 0)),
    )
    return pl.pallas_call(
        _experts_kernel,
        grid_spec=grid_spec,
        out_shape=jax.ShapeDtypeStruct((cap, D), F32),
        compiler_params=_params("arbitrary"),
        name="experts",
    )(tile_expert, n_used, xs, wgu, wdn)


def _combine_kernel(d0_ref, d1_ref, h_ref, info_ref, gfin_ref, ys_ref, o_ref, y0_ref, y1_ref, sem, *, final_norm):
    tm = h_ref.shape[0]

    def start(r, _):
        _row_copy(ys_ref, d0_ref[0, 0, r], y0_ref, r, sem).start()
        _row_copy(ys_ref, d1_ref[0, 0, r], y1_ref, r, sem).start()
        return 0

    def wait(r, _):
        _row_copy(ys_ref, d0_ref[0, 0, r], y0_ref, r, sem).wait()
        _row_copy(ys_ref, d1_ref[0, 0, r], y1_ref, r, sem).wait()
        return 0

    lax.fori_loop(0, tm, start, 0)
    lax.fori_loop(0, tm, wait, 0)
    info = info_ref[...]
    out = h_ref[...] + info[:, INFO_G0:INFO_G0 + 1] * y0_ref[...] + info[:, INFO_G1:INFO_G1 + 1] * y1_ref[...]
    if final_norm:
        out = _rmsnorm(out, gfin_ref[...])
    o_ref[...] = out


def _combine(h2, info, g_final, ys, dest0, dest1, tm, final_norm):
    N, D = h2.shape
    nt = N // tm
    idx = pl.BlockSpec((1, 1, tm), lambda i: (i, 0, 0), memory_space=pltpu.SMEM)
    return pl.pallas_call(
        functools.partial(_combine_kernel, final_norm=final_norm),
        grid=(nt,),
        in_specs=[idx, idx, pl.BlockSpec((tm, D), lambda i: (i, 0)), pl.BlockSpec((tm, LANES), lambda i: (i, 0)),
                  _full(g_final.shape), pl.BlockSpec(memory_space=pl.ANY)],
        out_specs=pl.BlockSpec((tm, D), lambda i: (i, 0)),
        out_shape=jax.ShapeDtypeStruct((N, D), F32),
        scratch_shapes=[pltpu.VMEM((tm, D), F32), pltpu.VMEM((tm, D), F32), pltpu.SemaphoreType.DMA(())],
        compiler_params=_params("arbitrary"),
        name="combine",
    )(dest0.reshape(nt, 1, tm), dest1.reshape(nt, 1, tm), h2, info, g_final, ys)


IN_SIZES = (NSA_WIDTH,) + (KV_WIDTH,) * 6 + (NSA_HEADS * 3,) + (CONV_WIDTH,) * 3 + (HGRN_WIDTH,) * 4
IN_OFF = tuple(int(v) for v in np.cumsum((0,) + IN_SIZES))


def _rope_tables(T):
    half = HEAD_DIM // 2
    inv = ROPE_THETA ** (-jnp.arange(half, dtype=F32) / half)
    ang = jnp.arange(T).astype(F32)[:, None] * inv[None, :]
    cos, sin = jnp.cos(ang), jnp.sin(ang)
    zero = jnp.zeros_like(sin)
    reps = LANES // HEAD_DIM
    cos_t = jnp.tile(jnp.concatenate([cos, cos], axis=1), (1, reps))
    sin_a = jnp.tile(jnp.concatenate([-sin, zero], axis=1), (1, reps))
    sin_b = jnp.tile(jnp.concatenate([zero, sin], axis=1), (1, reps))
    return cos_t, sin_a, sin_b


def _cmp_to_sel(T):
    nc_pad, ns = T // CMP_STRIDE, T // SEL_BLOCK
    nc = (T - CMP_LEN) // CMP_STRIDE + 1
    cs = np.arange(nc_pad) * CMP_STRIDE
    ss = np.arange(ns) * SEL_BLOCK
    ov = np.clip(np.minimum(cs[:, None] + CMP_LEN, ss[None, :] + SEL_BLOCK)
                 - np.maximum(cs[:, None], ss[None, :]), 0, None) / CMP_LEN
    ov[nc:] = 0.0
    return jnp.asarray(ov, dtype=BF16)


def _gate_weight(w_in_l):
    wg = w_in_l[:, IN_OFF[7]:IN_OFF[8]].reshape(D_MODEL, NSA_KV_HEADS, NSA_GROUP * 3)
    return jnp.pad(wg, ((0, 0), (0, 0), (0, LANES - NSA_GROUP * 3))).reshape(D_MODEL, NSA_KV_HEADS * LANES)


def _heads_first(x, B, T):
    return x.reshape(B, T, NSA_KV_HEADS, HEAD_DIM).transpose(0, 2, 1, 3)


def _routing(info, counts, n_tok, te):
    e0 = info[:, INFO_E0].astype(jnp.int32)
    e1 = info[:, INFO_E1].astype(jnp.int32)
    r0 = info[:, INFO_R0].astype(jnp.int32)
    r1 = info[:, INFO_R1].astype(jnp.int32)
    c0 = counts[0, EXPERT_LANE0:EXPERT_LANE0 + N_EXPERTS].astype(jnp.int32)
    c1 = counts[1, EXPERT_LANE0:EXPERT_LANE0 + N_EXPERTS].astype(jnp.int32)
    tiles = (c0 + c1 + te - 1) // te
    tile_end = jnp.cumsum(tiles)
    off = (tile_end - tiles) * te
    dest0 = off[e0] + r0
    dest1 = off[e1] + c0[e1] + r1
    n_tiles = (2 * n_tok) // te + N_EXPERTS
    tile_expert = jnp.minimum(jnp.searchsorted(tile_end, jnp.arange(n_tiles), side="right"), N_EXPERTS - 1)
    return dest0, dest1, tile_expert.astype(jnp.int32), tile_end[-1:].astype(jnp.int32)


def kernel(x, mem, w_in, conv_w, cmp_pe, cmp_w1, cmp_w2, hgrn_lb_logits, hgrn_norm, w_out, cross_wq, cross_wkv,
           cross_wo, router_group_w, router_group_b, router_expert_w, router_expert_b, expert_w_gate_up,
           expert_w_down, norm_mix, norm_cross, norm_mem, norm_ffn, norm_final):
    B, T, D = x.shape
    depth = w_in.shape[0]
    n_tok = B * T
    tm = min(512, T)
    tm_row = min(256, T)
    te = 256

    cos_t, sin_a, sin_b = _rope_tables(T)
    c2s = _cmp_to_sel(T)
    ltri = jnp.asarray(np.tril(np.ones((tm, tm), np.float32), -1), dtype=BF16)

    p_lb = jax.nn.softmax(hgrn_lb_logits.astype(F32), axis=0)
    lower_bounds = jnp.cumsum(p_lb, axis=0) - p_lb[0:1]
    xs_rows = 2 * n_tok + N_EXPERTS * te

    h = x
    for l in range(depth):
        wl = w_in[l]
        piece = lambda a, b: wl[:, IN_OFF[a]:IN_OFF[b]].astype(BF16)
        cw = jnp.pad(conv_w[l], ((0, SUBLANES - CONV_K), (0, 0)))
        (q, qr, kc, vc, ks, vs, kw, vw, gs, o_conv, hg) = _inproj(
            h, norm_mix[l][None], cos_t, sin_a, sin_b, piece(0, 1), piece(1, 7), _gate_weight(wl).astype(BF16),
            piece(8, 11), piece(11, 15), cw, tm)

        chunks = lambda a: _heads_first(a, B, T).reshape(B, NSA_KV_HEADS, T // CMP_STRIDE, CMP_STRIDE * HEAD_DIM)
        kvc = _compress(jnp.stack([chunks(kc), chunks(vc)]), cmp_pe[l].reshape(2, 1, CMP_LEN * HEAD_DIM),
                        cmp_w1[l].astype(BF16), cmp_w2[l].astype(BF16))
        o_nsa = _nsa(q, qr, kvc, _heads_first(ks, B, T), _heads_first(vs, B, T), _heads_first(kw, B, T),
                     _heads_first(vw, B, T), gs, c2s)

        lb = lower_bounds[l][None]
        lb_params = jnp.concatenate([jnp.log(jnp.maximum(lb, LB_FLOOR)), jnp.log1p(-lb), 1.0 - lb,
                                     jnp.zeros((SUBLANES - 3, HGRN_WIDTH), F32)], axis=0)
        o_hgrn = _hgrn(hg, lb_params, hgrn_norm[l][None], tm)

        kmem, vmem = _memkv(mem, norm_mem[l][None], cross_wkv[l].astype(BF16))

        wr = jnp.zeros((D, LANES), F32).at[:, :N_GROUPS].set(router_group_w[l])
        wr = wr.at[:, EXPERT_LANE0:EXPERT_LANE0 + N_EXPERTS].set(router_expert_w[l])
        br = jnp.zeros((1, LANES), F32).at[0, :N_GROUPS].set(router_group_b[l])
        br = br.at[0, EXPERT_LANE0:EXPERT_LANE0 + N_EXPERTS].set(router_expert_b[l])
        wr_hi = wr.astype(BF16)
        wr_lo = (wr - wr_hi.astype(F32)).astype(BF16)
        h2, info, counts = _post(h, o_nsa, o_conv, o_hgrn, w_out[l].astype(BF16), norm_cross[l][None],
                                 cross_wq[l].astype(BF16), kmem, vmem, cross_wo[l].astype(BF16), norm_ffn[l][None],
                                 wr_hi, wr_lo, br, ltri, tm)

        h2 = h2.reshape(n_tok, D)
        info = info.reshape(n_tok, LANES)
        dest0, dest1, tile_expert, n_used = _routing(info, counts, n_tok, te)
        xs = _dispatch(h2, norm_ffn[l][None], dest0, dest1, jnp.zeros((xs_rows, D), F32), tm_row)
        ys = _experts(xs, expert_w_gate_up[l].astype(BF16), expert_w_down[l].astype(BF16), tile_expert, n_used, te)
        h = _combine(h2, info, norm_final[None], ys, dest0, dest1, tm_row, l == depth - 1).reshape(B, T, D)
    return h
```

```python
import functools

import numpy as np
import jax
import jax.numpy as jnp
from jax import lax
from jax.experimental import pallas as pl
from jax.experimental.pallas import tpu as pltpu

F32 = jnp.float32
BF16 = jnp.bfloat16

D_MODEL = 1024
HEAD_DIM = 64
NSA_HEADS = 8
NSA_KV_HEADS = 2
NSA_GROUP = NSA_HEADS // NSA_KV_HEADS
NSA_WIDTH = NSA_HEADS * HEAD_DIM
KV_WIDTH = NSA_KV_HEADS * HEAD_DIM
CMP_LEN = 32
CMP_STRIDE = 16
CMP_HIDDEN = 2 * HEAD_DIM
SEL_BLOCK = 64
N_SELECT = 16
WINDOW = 512
Q_BLOCK = 128
MASK_VALUE = -1e30
SEL_FORCE = 1e6
CONV_WIDTH = 256
CONV_K = 3
HGRN_HEADS = 4
HGRN_WIDTH = HGRN_HEADS * HEAD_DIM
HGRN_CHUNK = 64
LB_FLOOR = 1e-30
ROPE_THETA = 10000.0
CROSS_HEADS = 4
CROSS_WIDTH = CROSS_HEADS * HEAD_DIM
N_GROUPS = 4
EXPERTS_PER_GROUP = 8
N_EXPERTS = N_GROUPS * EXPERTS_PER_GROUP
D_EXPERT = 512
RMS_EPS = 1e-6
QK_SCALE = HEAD_DIM ** -0.5
LOG2E = 1.4426950408889634
HEAD_SHIFT = HEAD_DIM.bit_length() - 1
SEL_SHIFT = SEL_BLOCK.bit_length() - 1

LANES = 128
SUBLANES = 8
VMEM_LIMIT_BYTES = 48 * 1024 * 1024

Q_AUG_WIDTH = NSA_HEADS * LANES

EXPERT_LANE0 = 32
INFO_E0, INFO_E1, INFO_R0, INFO_R1, INFO_G0, INFO_G1 = 0, 1, 2, 3, 4, 5


def _dot(a, b):
    return jnp.dot(a, b, preferred_element_type=F32)


def _dot_nt(a, b):
    return lax.dot_general(a, b, (((1,), (1,)), ((), ())), preferred_element_type=F32)


def _split_bf16(x):
    hi = x.astype(BF16)
    lo = (x - hi.astype(F32)).astype(BF16)
    return hi, lo


def _rmsnorm(x, g):
    return x * lax.rsqrt(jnp.mean(x * x, axis=-1, keepdims=True) + RMS_EPS) * g


def _params(*sem):
    return pltpu.CompilerParams(dimension_semantics=sem, vmem_limit_bytes=VMEM_LIMIT_BYTES)


def _full(shape):
    nd = len(shape)
    return pl.BlockSpec(shape, lambda *_: (0,) * nd)


def _rope(x, cos, sin_a, sin_b):
    width = x.shape[1]
    reps = width // LANES
    if reps > 1:
        cos = jnp.concatenate([cos] * reps, axis=1)
        sin_a = jnp.concatenate([sin_a] * reps, axis=1)
        sin_b = jnp.concatenate([sin_b] * reps, axis=1)
    half = HEAD_DIM // 2
    return x * cos + pltpu.roll(x, width - half, 1) * sin_a + pltpu.roll(x, half, 1) * sin_b


def _inproj_kernel(h_ref, g_ref, cos_ref, sa_ref, sb_ref, wq_ref, wkv_ref, wg_ref, wc_ref, wh_ref, cw_ref,
                   q_ref, qr_ref, kc_ref, vc_ref, ks_ref, vs_ref, kw_ref, vw_ref, gs_ref, oc_ref, hg_ref,
                   carry_ref):
    tm = h_ref.shape[1]

    @pl.when(pl.program_id(1) == 0)
    def _():
        carry_ref[...] = jnp.zeros_like(carry_ref)

    xb = _rmsnorm(h_ref[0], g_ref[...]).astype(BF16)
    cos, sa, sb = cos_ref[...], sa_ref[...], sb_ref[...]

    q = _dot(xb, wq_ref[...]) * (QK_SCALE * LOG2E)
    q_ref[0] = q.astype(BF16)
    qr_ref[0] = _rope(q, cos, sa, sb).astype(BF16)

    kv = _dot(xb, wkv_ref[...])
    kw_ = KV_WIDTH
    kc_ref[0] = kv[:, 0 * kw_:1 * kw_].astype(BF16)
    vc_ref[0] = kv[:, 1 * kw_:2 * kw_].astype(BF16)
    pos = pl.program_id(1) * tm + lax.broadcasted_iota(jnp.int32, (tm, LANES), 0)
    blk = lax.broadcasted_iota(jnp.int32, (tm, LANES), 1) == lax.shift_right_logical(pos, SEL_SHIFT)
    ks_ref[0, :, 0:kw_] = _rope(kv[:, 2 * kw_:3 * kw_], cos, sa, sb).astype(BF16)
    ks_ref[0, :, kw_:kw_ + LANES] = jnp.where(blk, 1.0, 0.0).astype(BF16)
    vs_ref[0] = kv[:, 3 * kw_:4 * kw_].astype(BF16)
    kw_ref[0] = _rope(kv[:, 4 * kw_:5 * kw_], cos, sa, sb).astype(BF16)
    vw_ref[0] = kv[:, 5 * kw_:6 * kw_].astype(BF16)

    gl = _dot(xb, wg_ref[...])
    gs_ref[0] = 1.0 / (1.0 + jnp.exp(-gl))

    cv = _dot(xb, wc_ref[...])
    cb = cv[:, 0:CONV_WIDTH]
    u = cv[:, CONV_WIDTH:2 * CONV_WIDTH] * cv[:, 2 * CONV_WIDTH:3 * CONV_WIDTH]
    ext = jnp.concatenate([carry_ref[...], u], axis=0)
    u1 = pltpu.roll(ext, 1, 0)[SUBLANES:]
    u2 = pltpu.roll(ext, 2, 0)[SUBLANES:]
    cw = cw_ref[...]
    oc_ref[0] = (cb * (cw[0:1] * u2 + cw[1:2] * u1 + cw[2:3] * u)).astype(BF16)
    carry_ref[...] = u[tm - SUBLANES:tm]

    hg_ref[0] = _dot(xb, wh_ref[...])


def _inproj(h, gain, cos, sin_a, sin_b, wq, wkv, wg, wc, wh, cw, tm):
    B, T, D = h.shape
    nt = T // tm
    tok = lambda w: pl.BlockSpec((1, tm, w), lambda b, t: (b, t, 0))
    tab = pl.BlockSpec((tm, LANES), lambda b, t: (t, 0))
    kvw = KV_WIDTH
    out_w = [(Q_AUG_WIDTH, BF16), (Q_AUG_WIDTH, BF16), (kvw, BF16), (kvw, BF16), (kvw + LANES, BF16), (kvw, BF16),
             (kvw, BF16), (kvw, BF16), (2 * LANES, F32), (CONV_WIDTH, BF16), (4 * HGRN_WIDTH, F32)]
    return pl.pallas_call(
        _inproj_kernel,
        grid=(B, nt),
        in_specs=[tok(D), _full(gain.shape), tab, tab, tab, _full(wq.shape), _full(wkv.shape), _full(wg.shape),
                  _full(wc.shape), _full(wh.shape), _full(cw.shape)],
        out_specs=[tok(w) for w, _ in out_w],
        out_shape=[jax.ShapeDtypeStruct((B, T, w), dt) for w, dt in out_w],
        scratch_shapes=[pltpu.VMEM((SUBLANES, CONV_WIDTH), F32)],
        compiler_params=_params("arbitrary", "arbitrary"),
        name="inproj",
    )(h, gain, cos, sin_a, sin_b, wq, wkv, wg, wc, wh, cw)


def _compress_kernel(x_ref, pe_ref, w1a_ref, w1b_ref, w2_ref, o_ref):
    x = x_ref[0, 0]
    n = x.shape[0]
    w1a, w1b = w1a_ref[0], w1b_ref[0]
    pe_a = jnp.broadcast_to(pe_ref[0, 0:1], (SUBLANES, x.shape[1])).astype(BF16)
    pe_b = jnp.broadcast_to(pe_ref[0, 1:2], (SUBLANES, x.shape[1])).astype(BF16)
    pe_term = (_dot(pe_a, w1a) + _dot(pe_b, w1b))[0:1]
    hid = _dot(x, w1a) + pltpu.roll(_dot(x, w1b), n - 1, 0) + pe_term
    act = hid / (1.0 + jnp.exp(-hid))
    o_ref[0, 0] = _dot(act.astype(BF16), w2_ref[0]).astype(BF16)


def _compress(xkv, pe, w1a, w1b, w2):
    _, B, n, width = xkv.shape
    hid = NSA_KV_HEADS * CMP_HIDDEN
    per_kv = lambda *shape: pl.BlockSpec((1,) + shape, lambda s, b: (s,) + (0,) * len(shape))
    return pl.pallas_call(
        _compress_kernel,
        grid=(2, B),
        in_specs=[pl.BlockSpec((1, 1, n, width), lambda s, b: (s, b, 0, 0)), per_kv(SUBLANES, width),
                  per_kv(width, hid), per_kv(width, hid), per_kv(hid, KV_WIDTH)],
        out_specs=pl.BlockSpec((1, 1, n, KV_WIDTH), lambda s, b: (s, b, 0, 0)),
        out_shape=jax.ShapeDtypeStruct((2, B, n, KV_WIDTH), BF16),
        compiler_params=_params("arbitrary", "arbitrary"),
        name="compress",
    )(xkv, pe, w1a, w1b, w2)


def _masked_softmax2(s, mask):
    s = jnp.where(mask, s, MASK_VALUE)
    m = jnp.max(s, axis=1, keepdims=True)
    e = jnp.exp2(s - m)
    den = jnp.sum(e, axis=1, keepdims=True)
    return e, jnp.where(m > 0.5 * MASK_VALUE, 1.0 / den, 0.0)


def _nsa_kernel(q_ref, qr_ref, kc_ref, vc_ref, ka_ref, vs_ref, kw_ref, vw_ref, gs_ref, c2s_ref, o_ref, *, key_tile):
    i = pl.program_id(1)
    qb, g_, n_h = Q_BLOCK, NSA_GROUP, NSA_KV_HEADS
    rows = g_ * qb
    heads = range(n_h)

    def q_groups(ref, h):
        return [ref[0, :, (h * g_ + g) * LANES:(h * g_ + g + 1) * LANES] for g in range(g_)]

    t_row = i * qb + (lax.broadcasted_iota(jnp.int32, (rows, 1), 0) & (qb - 1))
    t_q = i * qb + lax.broadcasted_iota(jnp.int32, (qb, 1), 0)

    kc, vc = kc_ref[0, 0], vc_ref[0, 0]
    c_end = lax.broadcasted_iota(jnp.int32, (1, kc.shape[0]), 1) * CMP_STRIDE + (CMP_LEN - 1)
    o_c, imp = [], []
    for h in heads:
        e_c, inv_c = _masked_softmax2(_dot_nt(jnp.concatenate(q_groups(q_ref, h), axis=0), kc), c_end <= t_row)
        p_c = e_c * inv_c
        o_c.append(_dot(p_c.astype(BF16), vc))
        p_g = p_c[0:qb]
        for g in range(1, g_):
            p_g = p_g + p_c[g * qb:(g + 1) * qb]
        imp.append(_dot(p_g.astype(BF16), c2s_ref[...]))

    wk = WINDOW + qb
    w_start = pl.multiple_of(jnp.maximum(i * qb - WINDOW, 0), qb)
    kwin = kw_ref[0, pl.ds(w_start, wk), :]
    vwin = vw_ref[0, pl.ds(w_start, wk), :]
    dist = t_row - (w_start + lax.broadcasted_iota(jnp.int32, (1, wk), 1))
    w_mask = (dist >= 0) & (dist < WINDOW)
    o_w = []
    for h in heads:
        e_w, inv_w = _masked_softmax2(_dot_nt(jnp.concatenate(q_groups(qr_ref, h), axis=0), kwin), w_mask)
        o_w.append(_dot(e_w.astype(BF16), vwin) * inv_w)

    cur = lax.shift_right_logical(t_q, SEL_SHIFT)
    j = lax.broadcasted_iota(jnp.int32, (qb, LANES), 1)
    jf = j.astype(F32)
    forced = (j == 0) | (j == cur) | (j == cur - 1)
    q_aug = []
    for h in heads:
        work = jnp.where(forced | (j > cur), -jnp.inf, imp[h])
        sel = jnp.where(forced, 1.0, 0.0)
        for _ in range(N_SELECT - 3):
            mx = jnp.max(work, axis=1, keepdims=True)
            first = jnp.min(jnp.where(work == mx, jf, float(LANES)), axis=1, keepdims=True)
            pick = jf == first
            sel = jnp.where(pick, 1.0, sel)
            work = jnp.where(pick, -jnp.inf, work)
        sel_bias = jnp.where(sel > 0, 0.0, MASK_VALUE).astype(BF16)
        q_aug.append(jnp.concatenate([jnp.concatenate([qg, sel_bias], axis=1) for qg in q_groups(qr_ref, h)],
                                     axis=0))

    key_col = lax.broadcasted_iota(jnp.int32, (1, key_tile), 1)

    def sel_tile(kt, carry, causal):
        start = pl.multiple_of(kt * key_tile, key_tile)
        ka = ka_ref[0, pl.ds(start, key_tile), :]
        v = vs_ref[0, pl.ds(start, key_tile), :]
        out = []
        for h in heads:
            m, l, acc = carry[h]
            s = _dot_nt(q_aug[h], ka)
            if causal:
                s = jnp.where(start + key_col <= t_row, s, MASK_VALUE)
            m_new = jnp.maximum(m, jnp.max(s, axis=1, keepdims=True))
            alpha = jnp.exp2(m - m_new)
            p = jnp.exp2(s - m_new)
            l = alpha * l + jnp.sum(p, axis=1, keepdims=True)
            acc = alpha * acc + _dot(p.astype(BF16), v)
            out.append((m_new, l, acc))
        return tuple(out)

    n_full = (i * qb) // key_tile
    init = tuple((jnp.full((rows, 1), MASK_VALUE, F32), jnp.zeros((rows, 1), F32), jnp.zeros((rows, LANES), F32))
                 for _ in heads)
    carry = lax.fori_loop(0, n_full, lambda kt, c: sel_tile(kt, c, False), init)
    carry = sel_tile(n_full, carry, True)
    o_s = [acc * (1.0 / l) for _, l, acc in carry]

    gs = gs_ref[0]
    low_half = lax.broadcasted_iota(jnp.int32, (qb, LANES), 1) < HEAD_DIM
    out_groups = []
    for h in heads:
        mixed = []
        for g in range(g_):
            r = slice(g * qb, (g + 1) * qb)
            c = h * LANES + 3 * g
            mixed.append(gs[:, c:c + 1] * o_c[h][r] + gs[:, c + 1:c + 2] * o_s[h][r] + gs[:, c + 2:c + 3] * o_w[h][r])
        for pair in range(g_ // 2):
            a, b = mixed[2 * pair], mixed[2 * pair + 1]
            if h == 0:
                out_groups.append(jnp.where(low_half, a, pltpu.roll(b, HEAD_DIM, 1)))
            else:
                out_groups.append(jnp.where(low_half, pltpu.roll(a, HEAD_DIM, 1), b))
    o_ref[0] = jnp.concatenate(out_groups, axis=1).astype(o_ref.dtype)


def _nsa(q, qr, kvc, ka, vs, kw, vw, gs, c2s):
    B, T, _ = q.shape
    assert T // SEL_BLOCK <= LANES
    nc = kvc.shape[2]
    key_tile = min(512, T)
    blk = lambda w: pl.BlockSpec((1, Q_BLOCK, w), lambda b, i: (b, i, 0))
    seq = lambda w: pl.BlockSpec((1, T, w), lambda b, i: (b, 0, 0))
    cmp_spec = lambda s: pl.BlockSpec((1, 1, nc, KV_WIDTH), lambda b, i: (s, b, 0, 0))
    return pl.pallas_call(
        functools.partial(_nsa_kernel, key_tile=key_tile),
        grid=(B, T // Q_BLOCK),
        in_specs=[blk(Q_AUG_WIDTH), blk(Q_AUG_WIDTH), cmp_spec(0), cmp_spec(1), seq(KV_WIDTH + LANES), seq(KV_WIDTH),
                  seq(KV_WIDTH), seq(KV_WIDTH), blk(2 * LANES), _full(c2s.shape)],
        out_specs=blk(NSA_WIDTH),
        out_shape=jax.ShapeDtypeStruct((B, T, NSA_WIDTH), BF16),
        compiler_params=_params("arbitrary", "arbitrary"),
        name="nsa",
    )(q, qr, kvc, kvc, ka, vs, kw, vw, gs, c2s)


def _hgrn_kernel(x_ref, lbp_ref, ng_ref, o_ref, st_ref):
    w = HGRN_WIDTH
    L = HGRN_CHUNK
    n_chunks = x_ref.shape[1] // L

    @pl.when(pl.program_id(1) == 0)
    def _():
        st_ref[...] = jnp.zeros_like(st_ref)

    log_lb, log_1m_lb, one_m_lb = lbp_ref[0:1], lbp_ref[1:2], lbp_ref[2:3]
    ng = ng_ref[...]
    ri = lax.broadcasted_iota(jnp.int32, (L, L), 0)
    ci = lax.broadcasted_iota(jnp.int32, (L, L), 1)
    tril = jnp.where(ci <= ri, 1.0, 0.0).astype(BF16)
    hr = lax.shift_right_logical(lax.broadcasted_iota(jnp.int32, (w, w), 0), HEAD_SHIFT)
    hc = lax.shift_right_logical(lax.broadcasted_iota(jnp.int32, (w, w), 1), HEAD_SHIFT)
    same_head = hr == hc
    head_ones = jnp.where(same_head, 1.0, 0.0).astype(BF16)
    row = lax.broadcasted_iota(jnp.int32, (L, 1), 0)

    def chunk(c, _):
        r0 = pl.multiple_of(c * L, L)
        q = x_ref[0, pl.ds(r0, L), 0 * w:1 * w] * QK_SCALE
        z = x_ref[0, pl.ds(r0, L), 1 * w:2 * w]
        v = x_ref[0, pl.ds(r0, L), 2 * w:3 * w]
        g = x_ref[0, pl.ds(r0, L), 3 * w:4 * w]

        log_sig = jnp.minimum(z, 0.0) - jnp.log(1.0 + jnp.exp(-jnp.abs(z)))
        bb = log_1m_lb + log_sig
        log_f = jnp.maximum(log_lb, bb) + jnp.log(1.0 + jnp.exp(-jnp.abs(log_lb - bb)))
        k = one_m_lb / (1.0 + jnp.exp(z))

        lf_hi, lf_lo = _split_bf16(log_f)
        b = _dot(tril, lf_hi) + _dot(tril, lf_lo)
        b_last = b[L - 1:L]
        st = st_ref[...]
        o = _dot_nt((q * jnp.exp(b)).astype(BF16), st.astype(BF16))
        for d in range(L):
            if d == 0:
                wgt = q * k
                vs = v
            else:
                ok = row >= d
                diff = jnp.where(ok, b - pltpu.roll(b, d, 0), 0.0)
                wgt = jnp.where(ok, q * pltpu.roll(k, d, 0) * jnp.exp(diff), 0.0)
                vs = pltpu.roll(v, d, 0)
            o = o + _dot(wgt.astype(BF16), head_ones) * vs
        kd = k * jnp.exp(b_last - b)
        upd = _dot(v.T.astype(BF16), kd.astype(BF16))
        st_ref[...] = st * jnp.exp(b_last) + jnp.where(same_head, upd, 0.0)

        o2_hi, o2_lo = _split_bf16(o * o)
        ms = (_dot(o2_hi, head_ones) + _dot(o2_lo, head_ones)) * (1.0 / HEAD_DIM)
        y = o * lax.rsqrt(ms + RMS_EPS) * ng
        o_ref[0, pl.ds(r0, L), :] = (y * (g / (1.0 + jnp.exp(-g)))).astype(o_ref.dtype)
        return 0

    lax.fori_loop(0, n_chunks, chunk, 0)


def _hgrn(hg, lb_params, norm_g, rows):
    B, T, width = hg.shape
    return pl.pallas_call(
        _hgrn_kernel,
        grid=(B, T // rows),
        in_specs=[pl.BlockSpec((1, rows, width), lambda b, t: (b, t, 0)), _full(lb_params.shape), _full(norm_g.shape)],
        out_specs=pl.BlockSpec((1, rows, HGRN_WIDTH), lambda b, t: (b, t, 0)),
        out_shape=jax.ShapeDtypeStruct((B, T, HGRN_WIDTH), BF16),
        scratch_shapes=[pltpu.VMEM((HGRN_WIDTH, HGRN_WIDTH), F32)],
        compiler_params=_params("arbitrary", "arbitrary"),
        name="hgrn",
    )(hg, lb_params, norm_g)


def _memkv_kernel(m_ref, g_ref, w_ref, k_ref, v_ref):
    kv = _dot(_rmsnorm(m_ref[0], g_ref[...]).astype(BF16), w_ref[...])
    k_ref[0] = kv[:, :CROSS_WIDTH].astype(BF16)
    v_ref[0] = kv[:, CROSS_WIDTH:].astype(BF16)


def _memkv(mem, gain, wkv):
    B, M, D = mem.shape
    out = pl.BlockSpec((1, M, CROSS_WIDTH), lambda b: (b, 0, 0))
    return pl.pallas_call(
        _memkv_kernel,
        grid=(B,),
        in_specs=[pl.BlockSpec((1, M, D), lambda b: (b, 0, 0)), _full(gain.shape), _full(wkv.shape)],
        out_specs=[out, out],
        out_shape=[jax.ShapeDtypeStruct((B, M, CROSS_WIDTH), BF16)] * 2,
        compiler_params=_params("arbitrary"),
        name="memkv",
    )(mem, gain, wkv)


def _post_kernel(h_ref, on_ref, oc_ref, oh_ref, wo_ref, gc_ref, wq_ref, km_ref, vm_ref, wco_ref, gf_ref,
                 wr_hi_ref, wr_lo_ref, br_ref, ltri_ref, h2_ref, info_ref, cnt_ref, carry_ref):
    tm = h_ref.shape[1]
    first = (pl.program_id(0) == 0) & (pl.program_id(1) == 0)

    @pl.when(first)
    def _():
        carry_ref[...] = jnp.zeros_like(carry_ref)

    a, b = NSA_WIDTH, NSA_WIDTH + CONV_WIDTH
    mix = _dot(on_ref[0], wo_ref[0:a]) + _dot(oc_ref[0], wo_ref[a:b]) + _dot(oh_ref[0], wo_ref[b:])
    h1 = h_ref[0] + mix

    q = _dot(_rmsnorm(h1, gc_ref[...]).astype(BF16), wq_ref[...]) * QK_SCALE
    km, vm = km_ref[0], vm_ref[0]
    lane_head = lax.shift_right_logical(lax.broadcasted_iota(jnp.int32, (1, CROSS_WIDTH), 1), HEAD_SHIFT)
    o = jnp.zeros((tm, CROSS_WIDTH), F32)
    for hd in range(CROSS_HEADS):
        in_head = lane_head == hd
        s = _dot_nt(jnp.where(in_head, q, 0.0).astype(BF16), km)
        m = jnp.max(s, axis=1, keepdims=True)
        e = jnp.exp(s - m)
        p = e / jnp.sum(e, axis=1, keepdims=True)
        o = o + _dot(p.astype(BF16), jnp.where(in_head, vm, jnp.zeros_like(vm)))
    h2 = h1 + _dot(o.astype(BF16), wco_ref[...])
    h2_ref[0] = h2

    xn = _rmsnorm(h2, gf_ref[...])
    x_hi, x_lo = _split_bf16(xn)
    w_hi = wr_hi_ref[...]
    logits = _dot(x_hi, w_hi) + _dot(x_lo, w_hi) + _dot(x_hi, wr_lo_ref[...]) + br_ref[...]
    lane = lax.broadcasted_iota(jnp.int32, (tm, LANES), 1)
    lanef = lane.astype(F32)
    big = float(LANES)

    lg = jnp.where(lane < N_GROUPS, logits, -jnp.inf)
    mg = jnp.max(lg, axis=1, keepdims=True)
    p_grp_sel = 1.0 / jnp.sum(jnp.exp(lg - mg), axis=1, keepdims=True)
    grp = jnp.min(jnp.where(lg == mg, lanef, big), axis=1, keepdims=True)

    lo_lane = EXPERT_LANE0 + EXPERTS_PER_GROUP * grp
    in_grp = (lanef >= lo_lane) & (lanef < lo_lane + EXPERTS_PER_GROUP)
    le = jnp.where(in_grp, logits, -jnp.inf)
    me = jnp.max(le, axis=1, keepdims=True)
    ee = jnp.exp(le - me)
    pe = jnp.where(in_grp, ee / jnp.sum(ee, axis=1, keepdims=True), -1.0)
    p1 = jnp.max(pe, axis=1, keepdims=True)
    l1 = jnp.min(jnp.where(pe == p1, lanef, big), axis=1, keepdims=True)
    pe2 = jnp.where(lanef == l1, -1.0, pe)
    p2 = jnp.max(pe2, axis=1, keepdims=True)
    l2 = jnp.min(jnp.where(pe2 == p2, lanef, big), axis=1, keepdims=True)
    scale = p_grp_sel / (p1 + p2)
    g0, g1 = p1 * scale, p2 * scale

    oh0 = jnp.where(lanef == l1, 1.0, 0.0)
    oh1 = jnp.where(lanef == l2, 1.0, 0.0)
    ltri = ltri_ref[...]
    c0, c1 = carry_ref[0:1], carry_ref[1:2]
    r0 = jnp.sum((_dot(ltri, oh0.astype(BF16)) + c0) * oh0, axis=1, keepdims=True)
    r1 = jnp.sum((_dot(ltri, oh1.astype(BF16)) + c1) * oh1, axis=1, keepdims=True)
    c0 = c0 + jnp.sum(oh0, axis=0, keepdims=True)
    c1 = c1 + jnp.sum(oh1, axis=0, keepdims=True)
    carry_ref[0:1] = c0
    carry_ref[1:2] = c1
    cnt_ref[...] = carry_ref[...]

    info = jnp.zeros((tm, LANES), F32)
    for col, val in ((INFO_E0, l1 - EXPERT_LANE0), (INFO_E1, l2 - EXPERT_LANE0), (INFO_R0, r0), (INFO_R1, r1),
                     (INFO_G0, g0), (INFO_G1, g1)):
        info = jnp.where(lane == col, val, info)
    info_ref[0] = info


def _post(h, o_nsa, o_conv, o_hgrn, w_out, g_cross, wq, kmem, vmem, wco, g_ffn, wr_hi, wr_lo, br, ltri, tm):
    B, T, D = h.shape
    M = kmem.shape[1]
    tok = lambda w: pl.BlockSpec((1, tm, w), lambda b, t: (b, t, 0))
    memspec = pl.BlockSpec((1, M, CROSS_WIDTH), lambda b, t: (b, 0, 0))
    return pl.pallas_call(
        _post_kernel,
        grid=(B, T // tm),
        in_specs=[tok(D), tok(NSA_WIDTH), tok(CONV_WIDTH), tok(HGRN_WIDTH), _full(w_out.shape), _full(g_cross.shape),
                  _full(wq.shape), memspec, memspec, _full(wco.shape), _full(g_ffn.shape), _full(wr_hi.shape),
                  _full(wr_lo.shape), _full(br.shape), _full(ltri.shape)],
        out_specs=[tok(D), tok(LANES), _full((SUBLANES, LANES))],
        out_shape=[jax.ShapeDtypeStruct((B, T, D), F32), jax.ShapeDtypeStruct((B, T, LANES), F32),
                   jax.ShapeDtypeStruct((SUBLANES, LANES), F32)],
        scratch_shapes=[pltpu.VMEM((SUBLANES, LANES), F32)],
        compiler_params=_params("arbitrary", "arbitrary"),
        name="post",
    )(h, o_nsa, o_conv, o_hgrn, w_out, g_cross, wq, kmem, vmem, wco, g_ffn, wr_hi, wr_lo, br, ltri)


ROW_PIECES = D_MODEL // LANES
assert ROW_PIECES == SUBLANES


def _to_token_tiles(x, ref):
    tm = x.shape[0]
    for s in range(ROW_PIECES):
        ref[pl.ds(s, tm, stride=ROW_PIECES), :] = x[:, s * LANES:(s + 1) * LANES]


def _from_token_tiles(ref, tm):
    return jnp.concatenate([ref[pl.ds(s, tm, stride=ROW_PIECES), :] for s in range(ROW_PIECES)], axis=1)


def _token_copy(src, src_tok, dst, dst_tok, sem):
    rows = lambda t: pl.ds(pl.multiple_of(t * ROW_PIECES, ROW_PIECES), ROW_PIECES)
    return pltpu.make_async_copy(src.at[rows(src_tok)], dst.at[rows(dst_tok)], sem)


DMA_UNROLL = 8


def _dispatch_kernel(d0_ref, d1_ref, h_ref, g_ref, xs_in_ref, xs_ref, xn_ref, sem):
    del xs_in_ref
    tm = h_ref.shape[0]
    _to_token_tiles(_rmsnorm(h_ref[...], g_ref[...]), xn_ref)

    def start(r, _):
        _token_copy(xn_ref, r, xs_ref, d0_ref[0, 0, r], sem).start(priority=0)
        _token_copy(xn_ref, r, xs_ref, d1_ref[0, 0, r], sem).start(priority=1)
        return 0

    def wait(r, _):
        _token_copy(xn_ref, r, xs_ref, d0_ref[0, 0, r], sem).wait()
        _token_copy(xn_ref, r, xs_ref, d1_ref[0, 0, r], sem).wait()
        return 0

    lax.fori_loop(0, tm, start, 0, unroll=DMA_UNROLL)
    lax.fori_loop(0, tm, wait, 0, unroll=DMA_UNROLL)


def _dispatch(h2, g_ffn, dest0, dest1, xs_init, tm):
    N, D = h2.shape
    nt = N // tm
    idx = pl.BlockSpec((1, 1, tm), lambda i: (i, 0, 0), memory_space=pltpu.SMEM)
    return pl.pallas_call(
        _dispatch_kernel,
        grid=(nt,),
        in_specs=[idx, idx, pl.BlockSpec((tm, D), lambda i: (i, 0)), _full(g_ffn.shape),
                  pl.BlockSpec(memory_space=pl.ANY)],
        out_specs=pl.BlockSpec(memory_space=pl.ANY),
        out_shape=jax.ShapeDtypeStruct(xs_init.shape, F32),
        scratch_shapes=[pltpu.VMEM((tm * ROW_PIECES, LANES), F32), pltpu.SemaphoreType.DMA(())],
        input_output_aliases={4: 0},
        compiler_params=_params("arbitrary"),
        name="dispatch",
    )(dest0.reshape(nt, 1, tm), dest1.reshape(nt, 1, tm), h2, g_ffn, xs_init)


def _experts_kernel(te_ref, nu_ref, x_ref, wgu_ref, wdn_ref, y_ref):
    del te_ref
    j = pl.program_id(0)

    te = x_ref.shape[0] // ROW_PIECES

    @pl.when(j < nu_ref[0])
    def _():
        gu = _dot(_from_token_tiles(x_ref, te).astype(BF16), wgu_ref[0])
        gate, up = gu[:, :D_EXPERT], gu[:, D_EXPERT:]
        act = gate / (1.0 + jnp.exp(-gate)) * up
        _to_token_tiles(_dot(act.astype(BF16), wdn_ref[0]), y_ref)

    @pl.when(j >= nu_ref[0])
    def _():
        y_ref[...] = jnp.zeros_like(y_ref)


def _experts(xs, wgu, wdn, tile_expert, n_used, te):
    D = D_MODEL
    n_tiles = xs.shape[0] // (te * ROW_PIECES)
    tiles = pl.BlockSpec((te * ROW_PIECES, LANES), lambda j, te_, nu: (j, 0))
    grid_spec = pltpu.PrefetchScalarGridSpec(
        num_scalar_prefetch=2,
        grid=(n_tiles,),
        in_specs=[tiles,
                  pl.BlockSpec((1, D, 2 * D_EXPERT), lambda j, te_, nu: (te_[j], 0, 0)),
                  pl.BlockSpec((1, D_EXPERT, D), lambda j, te_, nu: (te_[j], 0, 0))],
        out_specs=tiles,
    )
    return pl.pallas_call(
        _experts_kernel,
        grid_spec=grid_spec,
        out_shape=jax.ShapeDtypeStruct(xs.shape, F32),
        compiler_params=_params("arbitrary"),
        name="experts",
    )(tile_expert, n_used, xs, wgu, wdn)


def _combine_kernel(d0_ref, d1_ref, h_ref, info_ref, gfin_ref, ys_ref, o_ref, y0_ref, y1_ref, sem, *, final_norm):
    tm = h_ref.shape[0]

    def start(r, _):
        _token_copy(ys_ref, d0_ref[0, 0, r], y0_ref, r, sem).start(priority=0)
        _token_copy(ys_ref, d1_ref[0, 0, r], y1_ref, r, sem).start(priority=1)
        return 0

    def wait(r, _):
        _token_copy(ys_ref, d0_ref[0, 0, r], y0_ref, r, sem).wait()
        _token_copy(ys_ref, d1_ref[0, 0, r], y1_ref, r, sem).wait()
        return 0

    lax.fori_loop(0, tm, start, 0, unroll=DMA_UNROLL)
    lax.fori_loop(0, tm, wait, 0, unroll=DMA_UNROLL)
    info = info_ref[...]
    out = (h_ref[...] + info[:, INFO_G0:INFO_G0 + 1] * _from_token_tiles(y0_ref, tm)
           + info[:, INFO_G1:INFO_G1 + 1] * _from_token_tiles(y1_ref, tm))
    if final_norm:
        out = _rmsnorm(out, gfin_ref[...])
    o_ref[...] = out


def _combine(h2, info, g_final, ys, dest0, dest1, tm, final_norm):
    N, D = h2.shape
    nt = N // tm
    idx = pl.BlockSpec((1, 1, tm), lambda i: (i, 0, 0), memory_space=pltpu.SMEM)
    return pl.pallas_call(
        functools.partial(_combine_kernel, final_norm=final_norm),
        grid=(nt,),
        in_specs=[idx, idx, pl.BlockSpec((tm, D), lambda i: (i, 0)), pl.BlockSpec((tm, LANES), lambda i: (i, 0)),
                  _full(g_final.shape), pl.BlockSpec(memory_space=pl.ANY)],
        out_specs=pl.BlockSpec((tm, D), lambda i: (i, 0)),
        out_shape=jax.ShapeDtypeStruct((N, D), F32),
        scratch_shapes=[pltpu.VMEM((tm * ROW_PIECES, LANES), F32), pltpu.VMEM((tm * ROW_PIECES, LANES), F32),
                        pltpu.SemaphoreType.DMA(())],
        compiler_params=_params("arbitrary"),
        name="combine",
    )(dest0.reshape(nt, 1, tm), dest1.reshape(nt, 1, tm), h2, info, g_final, ys)


IN_SIZES = (NSA_WIDTH,) + (KV_WIDTH,) * 6 + (NSA_HEADS * 3,) + (CONV_WIDTH,) * 3 + (HGRN_WIDTH,) * 4
IN_OFF = tuple(int(v) for v in np.cumsum((0,) + IN_SIZES))


def _rope_tables(T):
    half = HEAD_DIM // 2
    inv = ROPE_THETA ** (-jnp.arange(half, dtype=F32) / half)
    ang = jnp.arange(T).astype(F32)[:, None] * inv[None, :]
    cos, sin = jnp.cos(ang), jnp.sin(ang)
    zero = jnp.zeros_like(sin)
    reps = LANES // HEAD_DIM
    cos_t = jnp.tile(jnp.concatenate([cos, cos], axis=1), (1, reps))
    sin_a = jnp.tile(jnp.concatenate([-sin, zero], axis=1), (1, reps))
    sin_b = jnp.tile(jnp.concatenate([zero, sin], axis=1), (1, reps))
    return cos_t, sin_a, sin_b


def _cmp_to_sel(T):
    nc_pad, ns = T // CMP_STRIDE, T // SEL_BLOCK
    nc = (T - CMP_LEN) // CMP_STRIDE + 1
    cs = np.arange(nc_pad) * CMP_STRIDE
    ss = np.arange(ns) * SEL_BLOCK
    ov = np.clip(np.minimum(cs[:, None] + CMP_LEN, ss[None, :] + SEL_BLOCK)
                 - np.maximum(cs[:, None], ss[None, :]), 0, None) / CMP_LEN
    ov[nc:] = 0.0
    return jnp.asarray(np.pad(ov, ((0, 0), (0, LANES - ns))), dtype=BF16)


def _q_weight(w_in_l):
    wq = w_in_l[:, IN_OFF[0]:IN_OFF[1]].reshape(D_MODEL, NSA_KV_HEADS, NSA_GROUP, 1, HEAD_DIM)
    half = jnp.eye(NSA_KV_HEADS, dtype=wq.dtype)[None, :, None, :, None]
    return (wq * half).reshape(D_MODEL, Q_AUG_WIDTH)


def _compress_weights(pe, w1, w2):
    eye = jnp.eye(NSA_KV_HEADS, dtype=F32)
    half_len = CMP_LEN // 2
    w1r = w1.reshape(2, 2, half_len, HEAD_DIM, CMP_HIDDEN)
    w1_aug = jnp.einsum("sxldm,hg->sxlhdgm", w1r, eye).reshape(2, 2, half_len * KV_WIDTH, NSA_KV_HEADS * CMP_HIDDEN)
    w2_aug = jnp.einsum("smd,hg->shmgd", w2, eye).reshape(2, NSA_KV_HEADS * CMP_HIDDEN, KV_WIDTH)
    pe_r = jnp.broadcast_to(pe.reshape(2, 2, half_len, 1, HEAD_DIM), (2, 2, half_len, NSA_KV_HEADS, HEAD_DIM))
    pe_aug = jnp.pad(pe_r.reshape(2, 2, half_len * KV_WIDTH), ((0, 0), (0, SUBLANES - 2), (0, 0)))
    return pe_aug, w1_aug[:, 0].astype(BF16), w1_aug[:, 1].astype(BF16), w2_aug.astype(BF16)


def _gate_weight(w_in_l):
    wg = w_in_l[:, IN_OFF[7]:IN_OFF[8]].reshape(D_MODEL, NSA_KV_HEADS, NSA_GROUP * 3)
    return jnp.pad(wg, ((0, 0), (0, 0), (0, LANES - NSA_GROUP * 3))).reshape(D_MODEL, NSA_KV_HEADS * LANES)


def _routing(info, counts, n_tok, te):
    e0 = info[:, INFO_E0].astype(jnp.int32)
    e1 = info[:, INFO_E1].astype(jnp.int32)
    r0 = info[:, INFO_R0].astype(jnp.int32)
    r1 = info[:, INFO_R1].astype(jnp.int32)
    c0 = counts[0, EXPERT_LANE0:EXPERT_LANE0 + N_EXPERTS].astype(jnp.int32)
    c1 = counts[1, EXPERT_LANE0:EXPERT_LANE0 + N_EXPERTS].astype(jnp.int32)
    tiles = (c0 + c1 + te - 1) // te
    tile_end = jnp.cumsum(tiles)
    off = (tile_end - tiles) * te
    dest0 = off[e0] + r0
    dest1 = off[e1] + c0[e1] + r1
    n_tiles = (2 * n_tok) // te + N_EXPERTS
    tile_expert = jnp.minimum(jnp.searchsorted(tile_end, jnp.arange(n_tiles), side="right"), N_EXPERTS - 1)
    return dest0, dest1, tile_expert.astype(jnp.int32), tile_end[-1:].astype(jnp.int32)


def kernel(x, mem, w_in, conv_w, cmp_pe, cmp_w1, cmp_w2, hgrn_lb_logits, hgrn_norm, w_out, cross_wq, cross_wkv,
           cross_wo, router_group_w, router_group_b, router_expert_w, router_expert_b, expert_w_gate_up,
           expert_w_down, norm_mix, norm_cross, norm_mem, norm_ffn, norm_final):
    B, T, D = x.shape
    depth = w_in.shape[0]
    n_tok = B * T
    tm = min(512, T)
    tm_row = min(256, T)
    te = 256

    cos_t, sin_a, sin_b = _rope_tables(T)
    c2s = _cmp_to_sel(T)
    ltri = jnp.asarray(np.tril(np.ones((tm, tm), np.float32), -1), dtype=BF16)

    p_lb = jax.nn.softmax(hgrn_lb_logits.astype(F32), axis=0)
    lower_bounds = jnp.cumsum(p_lb, axis=0) - p_lb[0:1]
    xs_rows = 2 * n_tok + N_EXPERTS * te

    h = x
    for l in range(depth):
        wl = w_in[l]
        piece = lambda a, b: wl[:, IN_OFF[a]:IN_OFF[b]].astype(BF16)
        cw = jnp.pad(conv_w[l], ((0, SUBLANES - CONV_K), (0, 0)))
        (q, qr, kc, vc, ka, vs, kw, vw, gs, o_conv, hg) = _inproj(
            h, norm_mix[l][None], cos_t, sin_a, sin_b, _q_weight(wl).astype(BF16), piece(1, 7),
            _gate_weight(wl).astype(BF16), piece(8, 11), piece(11, 15), cw, tm)

        chunks = lambda a: a.reshape(B, T // CMP_STRIDE, CMP_STRIDE * KV_WIDTH)
        kvc = _compress(jnp.stack([chunks(kc), chunks(vc)]), *_compress_weights(cmp_pe[l], cmp_w1[l], cmp_w2[l]))
        o_nsa = _nsa(q, qr, kvc, ka, vs, kw, vw, gs, c2s)

        lb = lower_bounds[l][None]
        lb_params = jnp.concatenate([jnp.log(jnp.maximum(lb, LB_FLOOR)), jnp.log1p(-lb), 1.0 - lb,
                                     jnp.zeros((SUBLANES - 3, HGRN_WIDTH), F32)], axis=0)
        o_hgrn = _hgrn(hg, lb_params, hgrn_norm[l][None], tm)

        kmem, vmem = _memkv(mem, norm_mem[l][None], cross_wkv[l].astype(BF16))

        wr = jnp.zeros((D, LANES), F32).at[:, :N_GROUPS].set(router_group_w[l])
        wr = wr.at[:, EXPERT_LANE0:EXPERT_LANE0 + N_EXPERTS].set(router_expert_w[l])
        br = jnp.zeros((1, LANES), F32).at[0, :N_GROUPS].set(router_group_b[l])
        br = br.at[0, EXPERT_LANE0:EXPERT_LANE0 + N_EXPERTS].set(router_expert_b[l])
        wr_hi = wr.astype(BF16)
        wr_lo = (wr - wr_hi.astype(F32)).astype(BF16)
        h2, info, counts = _post(h, o_nsa, o_conv, o_hgrn, w_out[l].astype(BF16), norm_cross[l][None],
                                 cross_wq[l].astype(BF16), kmem, vmem, cross_wo[l].astype(BF16), norm_ffn[l][None],
                                 wr_hi, wr_lo, br, ltri, tm)

        h2 = h2.reshape(n_tok, D)
        info = info.reshape(n_tok, LANES)
        dest0, dest1, tile_expert, n_used = _routing(info, counts, n_tok, te)
        xs = _dispatch(h2, norm_ffn[l][None], dest0, dest1, jnp.zeros((xs_rows * ROW_PIECES, LANES), F32), tm_row)
        ys = _experts(xs, expert_w_gate_up[l].astype(BF16), expert_w_down[l].astype(BF16), tile_expert, n_used, te)
        h = _combine(h2, info, norm_final[None], ys, dest0, dest1, tm_row, l == depth - 1).reshape(B, T, D)
    return h
```

```python
import functools

import numpy as np
import jax
import jax.numpy as jnp
from jax import lax
from jax.experimental import pallas as pl
from jax.experimental.pallas import tpu as pltpu

F32 = jnp.float32
BF16 = jnp.bfloat16

D_MODEL = 1024
HEAD_DIM = 64
NSA_HEADS = 8
NSA_KV_HEADS = 2
NSA_GROUP = NSA_HEADS // NSA_KV_HEADS
NSA_WIDTH = NSA_HEADS * HEAD_DIM
KV_WIDTH = NSA_KV_HEADS * HEAD_DIM
CMP_LEN = 32
CMP_STRIDE = 16
CMP_HIDDEN = 2 * HEAD_DIM
SEL_BLOCK = 64
N_SELECT = 16
WINDOW = 512
Q_BLOCK = 128
MASK_VALUE = -1e30
SEL_FORCE = 1e6
CONV_WIDTH = 256
CONV_K = 3
HGRN_HEADS = 4
HGRN_WIDTH = HGRN_HEADS * HEAD_DIM
HGRN_CHUNK = 64
HGRN_SUB = 16
LB_FLOOR = 1e-30
ROPE_THETA = 10000.0
CROSS_HEADS = 4
CROSS_WIDTH = CROSS_HEADS * HEAD_DIM
N_GROUPS = 4
EXPERTS_PER_GROUP = 8
N_EXPERTS = N_GROUPS * EXPERTS_PER_GROUP
D_EXPERT = 512
RMS_EPS = 1e-6
QK_SCALE = HEAD_DIM ** -0.5
LOG2E = 1.4426950408889634
HEAD_SHIFT = HEAD_DIM.bit_length() - 1
SEL_SHIFT = SEL_BLOCK.bit_length() - 1

LANES = 128
SUBLANES = 8
VMEM_LIMIT_BYTES = 48 * 1024 * 1024

Q_AUG_WIDTH = NSA_HEADS * LANES

EXPERT_LANE0 = 32
INFO_E0, INFO_E1, INFO_R0, INFO_R1, INFO_G0, INFO_G1 = 0, 1, 2, 3, 4, 5


def _dot(a, b):
    return jnp.dot(a, b, preferred_element_type=F32)


def _dot_nt(a, b):
    return lax.dot_general(a, b, (((1,), (1,)), ((), ())), preferred_element_type=F32)


def _split_bf16(x):
    hi = x.astype(BF16)
    lo = (x - hi.astype(F32)).astype(BF16)
    return hi, lo


def _rmsnorm(x, g):
    return x * lax.rsqrt(jnp.mean(x * x, axis=-1, keepdims=True) + RMS_EPS) * g


def _params(*sem):
    return pltpu.CompilerParams(dimension_semantics=sem, vmem_limit_bytes=VMEM_LIMIT_BYTES)


def _full(shape):
    nd = len(shape)
    return pl.BlockSpec(shape, lambda *_: (0,) * nd)


def _rope(x, cos, sin_a, sin_b):
    width = x.shape[1]
    reps = width // LANES
    if reps > 1:
        cos = jnp.concatenate([cos] * reps, axis=1)
        sin_a = jnp.concatenate([sin_a] * reps, axis=1)
        sin_b = jnp.concatenate([sin_b] * reps, axis=1)
    half = HEAD_DIM // 2
    return x * cos + pltpu.roll(x, width - half, 1) * sin_a + pltpu.roll(x, half, 1) * sin_b


def _inproj_kernel(h_ref, g_ref, cos_ref, sa_ref, sb_ref, wq_ref, wkv_ref, wg_ref, wc_ref, wh_ref, cw_ref,
                   q_ref, qr_ref, kc_ref, vc_ref, ks_ref, vs_ref, kw_ref, vw_ref, gs_ref, oc_ref, hg_ref,
                   carry_ref):
    tm = h_ref.shape[1]

    @pl.when(pl.program_id(1) == 0)
    def _():
        carry_ref[...] = jnp.zeros_like(carry_ref)

    xb = _rmsnorm(h_ref[0], g_ref[...]).astype(BF16)
    cos, sa, sb = cos_ref[...], sa_ref[...], sb_ref[...]

    q = _dot(xb, wq_ref[...]) * (QK_SCALE * LOG2E)
    q_ref[0] = q.astype(BF16)
    qr_ref[0] = _rope(q, cos, sa, sb).astype(BF16)

    kv = _dot(xb, wkv_ref[...])
    kw_ = KV_WIDTH
    kc_ref[0] = kv[:, 0 * kw_:1 * kw_].astype(BF16)
    vc_ref[0] = kv[:, 1 * kw_:2 * kw_].astype(BF16)
    pos = pl.program_id(1) * tm + lax.broadcasted_iota(jnp.int32, (tm, LANES), 0)
    blk = lax.broadcasted_iota(jnp.int32, (tm, LANES), 1) == lax.shift_right_logical(pos, SEL_SHIFT)
    ks_ref[0, :, 0:kw_] = _rope(kv[:, 2 * kw_:3 * kw_], cos, sa, sb).astype(BF16)
    ks_ref[0, :, kw_:kw_ + LANES] = jnp.where(blk, 1.0, 0.0).astype(BF16)
    vs_ref[0] = kv[:, 3 * kw_:4 * kw_].astype(BF16)
    kw_ref[0] = _rope(kv[:, 4 * kw_:5 * kw_], cos, sa, sb).astype(BF16)
    vw_ref[0] = kv[:, 5 * kw_:6 * kw_].astype(BF16)

    gl = _dot(xb, wg_ref[...])
    gs_ref[0] = 1.0 / (1.0 + jnp.exp(-gl))

    cv = _dot(xb, wc_ref[...])
    cb = cv[:, 0:CONV_WIDTH]
    u = cv[:, CONV_WIDTH:2 * CONV_WIDTH] * cv[:, 2 * CONV_WIDTH:3 * CONV_WIDTH]
    ext = jnp.concatenate([carry_ref[...], u], axis=0)
    u1 = pltpu.roll(ext, 1, 0)[SUBLANES:]
    u2 = pltpu.roll(ext, 2, 0)[SUBLANES:]
    cw = cw_ref[...]
    oc_ref[0] = (cb * (cw[0:1] * u2 + cw[1:2] * u1 + cw[2:3] * u)).astype(BF16)
    carry_ref[...] = u[tm - SUBLANES:tm]

    hg_ref[0] = _dot(xb, wh_ref[...])


def _inproj(h, gain, cos, sin_a, sin_b, wq, wkv, wg, wc, wh, cw, tm):
    B, T, D = h.shape
    nt = T // tm
    tok = lambda w: pl.BlockSpec((1, tm, w), lambda b, t: (b, t, 0))
    tab = pl.BlockSpec((tm, LANES), lambda b, t: (t, 0))
    kvw = KV_WIDTH
    out_w = [(Q_AUG_WIDTH, BF16), (Q_AUG_WIDTH, BF16), (kvw, BF16), (kvw, BF16), (kvw + LANES, BF16), (kvw, BF16),
             (kvw, BF16), (kvw, BF16), (2 * LANES, F32), (CONV_WIDTH, BF16), (4 * HGRN_WIDTH, F32)]
    return pl.pallas_call(
        _inproj_kernel,
        grid=(B, nt),
        in_specs=[tok(D), _full(gain.shape), tab, tab, tab, _full(wq.shape), _full(wkv.shape), _full(wg.shape),
                  _full(wc.shape), _full(wh.shape), _full(cw.shape)],
        out_specs=[tok(w) for w, _ in out_w],
        out_shape=[jax.ShapeDtypeStruct((B, T, w), dt) for w, dt in out_w],
        scratch_shapes=[pltpu.VMEM((SUBLANES, CONV_WIDTH), F32)],
        compiler_params=_params("arbitrary", "arbitrary"),
        name="inproj",
    )(h, gain, cos, sin_a, sin_b, wq, wkv, wg, wc, wh, cw)


def _compress_kernel(x_ref, pe_ref, w1a_ref, w1b_ref, w2_ref, o_ref):
    x = x_ref[0, 0]
    n = x.shape[0]
    w1a, w1b = w1a_ref[0], w1b_ref[0]
    pe_a = jnp.broadcast_to(pe_ref[0, 0:1], (SUBLANES, x.shape[1])).astype(BF16)
    pe_b = jnp.broadcast_to(pe_ref[0, 1:2], (SUBLANES, x.shape[1])).astype(BF16)
    pe_term = (_dot(pe_a, w1a) + _dot(pe_b, w1b))[0:1]
    hid = _dot(x, w1a) + pltpu.roll(_dot(x, w1b), n - 1, 0) + pe_term
    act = hid / (1.0 + jnp.exp(-hid))
    o_ref[0, 0] = _dot(act.astype(BF16), w2_ref[0]).astype(BF16)


def _compress(xkv, pe, w1a, w1b, w2):
    _, B, n, width = xkv.shape
    hid = NSA_KV_HEADS * CMP_HIDDEN
    per_kv = lambda *shape: pl.BlockSpec((1,) + shape, lambda s, b: (s,) + (0,) * len(shape))
    return pl.pallas_call(
        _compress_kernel,
        grid=(2, B),
        in_specs=[pl.BlockSpec((1, 1, n, width), lambda s, b: (s, b, 0, 0)), per_kv(SUBLANES, width),
                  per_kv(width, hid), per_kv(width, hid), per_kv(hid, KV_WIDTH)],
        out_specs=pl.BlockSpec((1, 1, n, KV_WIDTH), lambda s, b: (s, b, 0, 0)),
        out_shape=jax.ShapeDtypeStruct((2, B, n, KV_WIDTH), BF16),
        compiler_params=_params("arbitrary", "arbitrary"),
        name="compress",
    )(xkv, pe, w1a, w1b, w2)


def _masked_softmax2(s, mask):
    s = jnp.where(mask, s, MASK_VALUE)
    m = jnp.max(s, axis=1, keepdims=True)
    e = jnp.exp2(s - m)
    den = jnp.sum(e, axis=1, keepdims=True)
    return e, jnp.where(m > 0.5 * MASK_VALUE, 1.0 / den, 0.0)


def _nsa_kernel(q_ref, qr_ref, kc_ref, vc_ref, ka_ref, vs_ref, kw_ref, vw_ref, gs_ref, c2s_ref, o_ref, *, key_tile):
    i = pl.program_id(1)
    qb, g_, n_h = Q_BLOCK, NSA_GROUP, NSA_KV_HEADS
    rows = g_ * qb
    heads = range(n_h)

    def q_groups(ref, h):
        return [ref[0, :, (h * g_ + g) * LANES:(h * g_ + g + 1) * LANES] for g in range(g_)]

    t_row = i * qb + (lax.broadcasted_iota(jnp.int32, (rows, 1), 0) & (qb - 1))
    t_q = i * qb + lax.broadcasted_iota(jnp.int32, (qb, 1), 0)

    kc, vc = kc_ref[0, 0], vc_ref[0, 0]
    c_end = lax.broadcasted_iota(jnp.int32, (1, kc.shape[0]), 1) * CMP_STRIDE + (CMP_LEN - 1)
    o_c, imp = [], []
    for h in heads:
        e_c, inv_c = _masked_softmax2(_dot_nt(jnp.concatenate(q_groups(q_ref, h), axis=0), kc), c_end <= t_row)
        p_c = e_c * inv_c
        o_c.append(_dot(p_c.astype(BF16), vc))
        p_g = p_c[0:qb]
        for g in range(1, g_):
            p_g = p_g + p_c[g * qb:(g + 1) * qb]
        imp.append(_dot(p_g.astype(BF16), c2s_ref[...]))

    wk = WINDOW + qb
    w_start = pl.multiple_of(jnp.maximum(i * qb - WINDOW, 0), qb)
    kwin = kw_ref[0, pl.ds(w_start, wk), :]
    vwin = vw_ref[0, pl.ds(w_start, wk), :]
    dist = t_row - (w_start + lax.broadcasted_iota(jnp.int32, (1, wk), 1))
    w_mask = (dist >= 0) & (dist < WINDOW)
    o_w = []
    for h in heads:
        e_w, inv_w = _masked_softmax2(_dot_nt(jnp.concatenate(q_groups(qr_ref, h), axis=0), kwin), w_mask)
        o_w.append(_dot(e_w.astype(BF16), vwin) * inv_w)

    cur = lax.shift_right_logical(t_q, SEL_SHIFT)
    j = lax.broadcasted_iota(jnp.int32, (qb, LANES), 1)
    jf = j.astype(F32)
    forced = (j == 0) | (j == cur) | (j == cur - 1)
    q_aug = []
    for h in heads:
        work = jnp.where(forced | (j > cur), -jnp.inf, imp[h])
        sel = jnp.where(forced, 1.0, 0.0)
        for _ in range(N_SELECT - 3):
            mx = jnp.max(work, axis=1, keepdims=True)
            first = jnp.min(jnp.where(work == mx, jf, float(LANES)), axis=1, keepdims=True)
            pick = jf == first
            sel = jnp.where(pick, 1.0, sel)
            work = jnp.where(pick, -jnp.inf, work)
        sel_bias = jnp.where(sel > 0, 0.0, MASK_VALUE).astype(BF16)
        q_aug.append(jnp.concatenate([jnp.concatenate([qg, sel_bias], axis=1) for qg in q_groups(qr_ref, h)],
                                     axis=0))

    key_col = lax.broadcasted_iota(jnp.int32, (1, key_tile), 1)

    def sel_tile(kt, carry, causal):
        start = pl.multiple_of(kt * key_tile, key_tile)
        ka = ka_ref[0, pl.ds(start, key_tile), :]
        v = vs_ref[0, pl.ds(start, key_tile), :]
        out = []
        for h in heads:
            m, l, acc = carry[h]
            s = _dot_nt(q_aug[h], ka)
            if causal:
                s = jnp.where(start + key_col <= t_row, s, MASK_VALUE)
            m_new = jnp.maximum(m, jnp.max(s, axis=1, keepdims=True))
            alpha = jnp.exp2(m - m_new)
            p = jnp.exp2(s - m_new)
            l = alpha * l + jnp.sum(p, axis=1, keepdims=True)
            acc = alpha * acc + _dot(p.astype(BF16), v)
            out.append((m_new, l, acc))
        return tuple(out)

    n_full = (i * qb) // key_tile
    init = tuple((jnp.full((rows, 1), MASK_VALUE, F32), jnp.zeros((rows, 1), F32), jnp.zeros((rows, LANES), F32))
                 for _ in heads)
    carry = lax.fori_loop(0, n_full, lambda kt, c: sel_tile(kt, c, False), init)
    carry = sel_tile(n_full, carry, True)
    o_s = [acc * (1.0 / l) for _, l, acc in carry]

    gs = gs_ref[0]
    low_half = lax.broadcasted_iota(jnp.int32, (qb, LANES), 1) < HEAD_DIM
    out_groups = []
    for h in heads:
        mixed = []
        for g in range(g_):
            r = slice(g * qb, (g + 1) * qb)
            c = h * LANES + 3 * g
            mixed.append(gs[:, c:c + 1] * o_c[h][r] + gs[:, c + 1:c + 2] * o_s[h][r] + gs[:, c + 2:c + 3] * o_w[h][r])
        for pair in range(g_ // 2):
            a, b = mixed[2 * pair], mixed[2 * pair + 1]
            if h == 0:
                out_groups.append(jnp.where(low_half, a, pltpu.roll(b, HEAD_DIM, 1)))
            else:
                out_groups.append(jnp.where(low_half, pltpu.roll(a, HEAD_DIM, 1), b))
    o_ref[0] = jnp.concatenate(out_groups, axis=1).astype(o_ref.dtype)


def _nsa(q, qr, kvc, ka, vs, kw, vw, gs, c2s):
    B, T, _ = q.shape
    assert T // SEL_BLOCK <= LANES
    nc = kvc.shape[2]
    key_tile = min(1024, T)
    blk = lambda w: pl.BlockSpec((1, Q_BLOCK, w), lambda b, i: (b, i, 0))
    seq = lambda w: pl.BlockSpec((1, T, w), lambda b, i: (b, 0, 0))
    cmp_spec = lambda s: pl.BlockSpec((1, 1, nc, KV_WIDTH), lambda b, i: (s, b, 0, 0))
    return pl.pallas_call(
        functools.partial(_nsa_kernel, key_tile=key_tile),
        grid=(B, T // Q_BLOCK),
        in_specs=[blk(Q_AUG_WIDTH), blk(Q_AUG_WIDTH), cmp_spec(0), cmp_spec(1), seq(KV_WIDTH + LANES), seq(KV_WIDTH),
                  seq(KV_WIDTH), seq(KV_WIDTH), blk(2 * LANES), _full(c2s.shape)],
        out_specs=blk(NSA_WIDTH),
        out_shape=jax.ShapeDtypeStruct((B, T, NSA_WIDTH), BF16),
        compiler_params=_params("arbitrary", "arbitrary"),
        name="nsa",
    )(q, qr, kvc, kvc, ka, vs, kw, vw, gs, c2s)


def _hgrn_kernel(x_ref, lbp_ref, ng_ref, o_ref, st_ref):
    w = HGRN_WIDTH
    L = HGRN_CHUNK
    n_chunks = x_ref.shape[1] // L

    @pl.when(pl.program_id(1) == 0)
    def _():
        st_ref[...] = jnp.zeros_like(st_ref)

    log_lb, log_1m_lb, one_m_lb = lbp_ref[0:1], lbp_ref[1:2], lbp_ref[2:3]
    ng = ng_ref[...]
    ri = lax.broadcasted_iota(jnp.int32, (L, L), 0)
    ci = lax.broadcasted_iota(jnp.int32, (L, L), 1)
    tril = jnp.where(ci <= ri, 1.0, 0.0).astype(BF16)
    hr = lax.shift_right_logical(lax.broadcasted_iota(jnp.int32, (w, w), 0), HEAD_SHIFT)
    hc = lax.shift_right_logical(lax.broadcasted_iota(jnp.int32, (w, w), 1), HEAD_SHIFT)
    same_head = hr == hc
    head_ones = jnp.where(same_head, 1.0, 0.0).astype(BF16)
    sub_row = lax.broadcasted_iota(jnp.int32, (L, 1), 0) & (HGRN_SUB - 1)
    lane_head = lax.shift_right_logical(lax.broadcasted_iota(jnp.int32, (1, w), 1), HEAD_SHIFT)

    def chunk(c, _):
        r0 = pl.multiple_of(c * L, L)
        q = x_ref[0, pl.ds(r0, L), 0 * w:1 * w] * QK_SCALE
        z = x_ref[0, pl.ds(r0, L), 1 * w:2 * w]
        v = x_ref[0, pl.ds(r0, L), 2 * w:3 * w]
        g = x_ref[0, pl.ds(r0, L), 3 * w:4 * w]

        log_sig = jnp.minimum(z, 0.0) - jnp.log(1.0 + jnp.exp(-jnp.abs(z)))
        bb = log_1m_lb + log_sig
        log_f = jnp.maximum(log_lb, bb) + jnp.log(1.0 + jnp.exp(-jnp.abs(log_lb - bb)))
        k = one_m_lb / (1.0 + jnp.exp(z))

        lf_hi, lf_lo = _split_bf16(log_f)
        b = _dot(tril, lf_hi) + _dot(tril, lf_lo)
        b_last = b[L - 1:L]
        st = st_ref[...]
        o = _dot_nt((q * jnp.exp(b)).astype(BF16), st.astype(BF16))
        for d in range(HGRN_SUB):
            if d == 0:
                wgt = q * k
                vs = v
            else:
                ok = sub_row >= d
                diff = jnp.where(ok, b - pltpu.roll(b, d, 0), 0.0)
                wgt = jnp.where(ok, q * pltpu.roll(k, d, 0) * jnp.exp(diff), 0.0)
                vs = pltpu.roll(v, d, 0)
            o = o + _dot(wgt.astype(BF16), head_ones) * vs
        parts = [jnp.zeros((HGRN_SUB, w), F32)]
        for lo in range(HGRN_SUB, L, HGRN_SUB):
            r = b[lo - 1:lo]
            qs = q[lo:lo + HGRN_SUB] * jnp.exp(b[lo:lo + HGRN_SUB] - r)
            ks = (k[0:lo] * jnp.exp(r - b[0:lo])).astype(BF16)
            acc = jnp.zeros((HGRN_SUB, w), F32)
            for hd in range(HGRN_HEADS):
                in_head = lane_head == hd
                a = _dot_nt(jnp.where(in_head, qs, 0.0).astype(BF16), ks)
                acc = acc + _dot(a.astype(BF16), jnp.where(in_head, v[0:lo], 0.0).astype(BF16))
            parts.append(acc)
        o = o + jnp.concatenate(parts, axis=0)
        kd = k * jnp.exp(b_last - b)
        upd = _dot(v.T.astype(BF16), kd.astype(BF16))
        st_ref[...] = st * jnp.exp(b_last) + jnp.where(same_head, upd, 0.0)

        o2_hi, o2_lo = _split_bf16(o * o)
        ms = (_dot(o2_hi, head_ones) + _dot(o2_lo, head_ones)) * (1.0 / HEAD_DIM)
        y = o * lax.rsqrt(ms + RMS_EPS) * ng
        o_ref[0, pl.ds(r0, L), :] = (y * (g / (1.0 + jnp.exp(-g)))).astype(o_ref.dtype)
        return 0

    lax.fori_loop(0, n_chunks, chunk, 0, unroll=2)


def _hgrn(hg, lb_params, norm_g, rows):
    B, T, width = hg.shape
    return pl.pallas_call(
        _hgrn_kernel,
        grid=(B, T // rows),
        in_specs=[pl.BlockSpec((1, rows, width), lambda b, t: (b, t, 0)), _full(lb_params.shape), _full(norm_g.shape)],
        out_specs=pl.BlockSpec((1, rows, HGRN_WIDTH), lambda b, t: (b, t, 0)),
        out_shape=jax.ShapeDtypeStruct((B, T, HGRN_WIDTH), BF16),
        scratch_shapes=[pltpu.VMEM((HGRN_WIDTH, HGRN_WIDTH), F32)],
        compiler_params=_params("arbitrary", "arbitrary"),
        name="hgrn",
    )(hg, lb_params, norm_g)


def _memkv_kernel(m_ref, g_ref, w_ref, k_ref, v_ref):
    kv = _dot(_rmsnorm(m_ref[0], g_ref[...]).astype(BF16), w_ref[...])
    k_ref[0] = kv[:, :CROSS_WIDTH].astype(BF16)
    v_ref[0] = kv[:, CROSS_WIDTH:].astype(BF16)


def _memkv(mem, gain, wkv):
    B, M, D = mem.shape
    out = pl.BlockSpec((1, M, CROSS_WIDTH), lambda b: (b, 0, 0))
    return pl.pallas_call(
        _memkv_kernel,
        grid=(B,),
        in_specs=[pl.BlockSpec((1, M, D), lambda b: (b, 0, 0)), _full(gain.shape), _full(wkv.shape)],
        out_specs=[out, out],
        out_shape=[jax.ShapeDtypeStruct((B, M, CROSS_WIDTH), BF16)] * 2,
        compiler_params=_params("arbitrary"),
        name="memkv",
    )(mem, gain, wkv)


def _post_kernel(h_ref, on_ref, oc_ref, oh_ref, wo_ref, gc_ref, wq_ref, km_ref, vm_ref, wco_ref, gf_ref,
                 wr_hi_ref, wr_lo_ref, br_ref, ltri_ref, h2_ref, info_ref, cnt_ref, carry_ref):
    tm = h_ref.shape[1]
    first = (pl.program_id(0) == 0) & (pl.program_id(1) == 0)

    @pl.when(first)
    def _():
        carry_ref[...] = jnp.zeros_like(carry_ref)

    a, b = NSA_WIDTH, NSA_WIDTH + CONV_WIDTH
    mix = _dot(on_ref[0], wo_ref[0:a]) + _dot(oc_ref[0], wo_ref[a:b]) + _dot(oh_ref[0], wo_ref[b:])
    h1 = h_ref[0] + mix

    q = _dot(_rmsnorm(h1, gc_ref[...]).astype(BF16), wq_ref[...]) * QK_SCALE
    km, vm = km_ref[0], vm_ref[0]
    lane_head = lax.shift_right_logical(lax.broadcasted_iota(jnp.int32, (1, CROSS_WIDTH), 1), HEAD_SHIFT)
    o = jnp.zeros((tm, CROSS_WIDTH), F32)
    for hd in range(CROSS_HEADS):
        in_head = lane_head == hd
        s = _dot_nt(jnp.where(in_head, q, 0.0).astype(BF16), km)
        m = jnp.max(s, axis=1, keepdims=True)
        e = jnp.exp(s - m)
        p = e / jnp.sum(e, axis=1, keepdims=True)
        o = o + _dot(p.astype(BF16), jnp.where(in_head, vm, jnp.zeros_like(vm)))
    h2 = h1 + _dot(o.astype(BF16), wco_ref[...])
    h2_ref[0] = h2

    xn = _rmsnorm(h2, gf_ref[...])
    x_hi, x_lo = _split_bf16(xn)
    w_hi = wr_hi_ref[...]
    logits = _dot(x_hi, w_hi) + _dot(x_lo, w_hi) + _dot(x_hi, wr_lo_ref[...]) + br_ref[...]
    lane = lax.broadcasted_iota(jnp.int32, (tm, LANES), 1)
    lanef = lane.astype(F32)
    big = float(LANES)

    lg = jnp.where(lane < N_GROUPS, logits, -jnp.inf)
    mg = jnp.max(lg, axis=1, keepdims=True)
    p_grp_sel = 1.0 / jnp.sum(jnp.exp(lg - mg), axis=1, keepdims=True)
    grp = jnp.min(jnp.where(lg == mg, lanef, big), axis=1, keepdims=True)

    lo_lane = EXPERT_LANE0 + EXPERTS_PER_GROUP * grp
    in_grp = (lanef >= lo_lane) & (lanef < lo_lane + EXPERTS_PER_GROUP)
    le = jnp.where(in_grp, logits, -jnp.inf)
    me = jnp.max(le, axis=1, keepdims=True)
    ee = jnp.exp(le - me)
    pe = jnp.where(in_grp, ee / jnp.sum(ee, axis=1, keepdims=True), -1.0)
    p1 = jnp.max(pe, axis=1, keepdims=True)
    l1 = jnp.min(jnp.where(pe == p1, lanef, big), axis=1, keepdims=True)
    pe2 = jnp.where(lanef == l1, -1.0, pe)
    p2 = jnp.max(pe2, axis=1, keepdims=True)
    l2 = jnp.min(jnp.where(pe2 == p2, lanef, big), axis=1, keepdims=True)
    scale = p_grp_sel / (p1 + p2)
    g0, g1 = p1 * scale, p2 * scale

    oh0 = jnp.where(lanef == l1, 1.0, 0.0)
    oh1 = jnp.where(lanef == l2, 1.0, 0.0)
    ltri = ltri_ref[...]
    c0, c1 = carry_ref[0:1], carry_ref[1:2]
    r0 = jnp.sum((_dot(ltri, oh0.astype(BF16)) + c0) * oh0, axis=1, keepdims=True)
    r1 = jnp.sum((_dot(ltri, oh1.astype(BF16)) + c1) * oh1, axis=1, keepdims=True)
    c0 = c0 + jnp.sum(oh0, axis=0, keepdims=True)
    c1 = c1 + jnp.sum(oh1, axis=0, keepdims=True)
    carry_ref[0:1] = c0
    carry_ref[1:2] = c1
    cnt_ref[...] = carry_ref[...]

    info = jnp.zeros((tm, LANES), F32)
    for col, val in ((INFO_E0, l1 - EXPERT_LANE0), (INFO_E1, l2 - EXPERT_LANE0), (INFO_R0, r0), (INFO_R1, r1),
                     (INFO_G0, g0), (INFO_G1, g1)):
        info = jnp.where(lane == col, val, info)
    info_ref[0] = info


def _post(h, o_nsa, o_conv, o_hgrn, w_out, g_cross, wq, kmem, vmem, wco, g_ffn, wr_hi, wr_lo, br, ltri, tm):
    B, T, D = h.shape
    M = kmem.shape[1]
    tok = lambda w: pl.BlockSpec((1, tm, w), lambda b, t: (b, t, 0))
    memspec = pl.BlockSpec((1, M, CROSS_WIDTH), lambda b, t: (b, 0, 0))
    return pl.pallas_call(
        _post_kernel,
        grid=(B, T // tm),
        in_specs=[tok(D), tok(NSA_WIDTH), tok(CONV_WIDTH), tok(HGRN_WIDTH), _full(w_out.shape), _full(g_cross.shape),
                  _full(wq.shape), memspec, memspec, _full(wco.shape), _full(g_ffn.shape), _full(wr_hi.shape),
                  _full(wr_lo.shape), _full(br.shape), _full(ltri.shape)],
        out_specs=[tok(D), tok(LANES), _full((SUBLANES, LANES))],
        out_shape=[jax.ShapeDtypeStruct((B, T, D), F32), jax.ShapeDtypeStruct((B, T, LANES), F32),
                   jax.ShapeDtypeStruct((SUBLANES, LANES), F32)],
        scratch_shapes=[pltpu.VMEM((SUBLANES, LANES), F32)],
        compiler_params=_params("arbitrary", "arbitrary"),
        name="post",
    )(h, o_nsa, o_conv, o_hgrn, w_out, g_cross, wq, kmem, vmem, wco, g_ffn, wr_hi, wr_lo, br, ltri)


ROW_PIECES = D_MODEL // LANES
assert ROW_PIECES == SUBLANES


def _to_token_tiles(x, ref):
    tm = x.shape[0]
    for s in range(ROW_PIECES):
        ref[pl.ds(s, tm, stride=ROW_PIECES), :] = x[:, s * LANES:(s + 1) * LANES]


def _from_token_tiles(ref, tm):
    return jnp.concatenate([ref[pl.ds(s, tm, stride=ROW_PIECES), :] for s in range(ROW_PIECES)], axis=1)


def _token_copy(src, src_tok, dst, dst_tok, sem):
    rows = lambda t: pl.ds(pl.multiple_of(t * ROW_PIECES, ROW_PIECES), ROW_PIECES)
    return pltpu.make_async_copy(src.at[rows(src_tok)], dst.at[rows(dst_tok)], sem)


DMA_UNROLL = 8


def _route_kernel(info_ref, tab_ref, dest_ref):
    info = info_ref[...]
    lanef = lax.broadcasted_iota(jnp.int32, info.shape, 1).astype(F32)
    pick = lambda col, row: jnp.sum(jnp.where(lanef == info[:, col:col + 1] + EXPERT_LANE0, tab_ref[row:row + 1], 0.0),
                                    axis=1, keepdims=True)
    d0 = pick(INFO_E0, 0) + info[:, INFO_R0:INFO_R0 + 1]
    d1 = pick(INFO_E1, 1) + info[:, INFO_R1:INFO_R1 + 1]
    dest_ref[...] = jnp.where(lanef == 0.0, d0, jnp.where(lanef == 1.0, d1, 0.0)).astype(jnp.int32)


def _route(info, tab, tm):
    N = info.shape[0]
    blk = pl.BlockSpec((tm, LANES), lambda i: (i, 0))
    return pl.pallas_call(
        _route_kernel,
        grid=(N // tm,),
        in_specs=[blk, _full(tab.shape)],
        out_specs=blk,
        out_shape=jax.ShapeDtypeStruct((N, LANES), jnp.int32),
        compiler_params=_params("arbitrary"),
        name="route",
    )(info, tab)


def _dispatch_kernel(pad_ref, nu_ref, d0_ref, d1_ref, h_ref, g_ref, xs_ref, xn_ref, zero_ref, sem):
    tm = h_ref.shape[0]
    fill_rows = zero_ref.shape[0]

    @pl.when(pl.program_id(0) == 0)
    def _():
        zero_ref[...] = jnp.zeros_like(zero_ref)
        fill = lambda row: pltpu.make_async_copy(
            zero_ref, xs_ref.at[pl.ds(pl.multiple_of(row * ROW_PIECES, ROW_PIECES), fill_rows)], sem)
        for e in range(N_EXPERTS):
            fill(pad_ref[e]).start()
        for e in range(N_EXPERTS):
            fill(pad_ref[e]).wait()
        n_total = xs_ref.shape[0] // fill_rows
        te = fill_rows // ROW_PIECES

        def tail_start(j, _):
            fill(j * te).start()
            return 0

        def tail_wait(j, _):
            fill(j * te).wait()
            return 0

        lax.fori_loop(nu_ref[0], n_total, tail_start, 0)
        lax.fori_loop(nu_ref[0], n_total, tail_wait, 0)

    _to_token_tiles(_rmsnorm(h_ref[...], g_ref[...]), xn_ref)

    def start(r, _):
        _token_copy(xn_ref, r, xs_ref, d0_ref[0, 0, r], sem).start(priority=0)
        _token_copy(xn_ref, r, xs_ref, d1_ref[0, 0, r], sem).start(priority=1)
        return 0

    def wait(r, _):
        _token_copy(xn_ref, r, xs_ref, d0_ref[0, 0, r], sem).wait()
        _token_copy(xn_ref, r, xs_ref, d1_ref[0, 0, r], sem).wait()
        return 0

    lax.fori_loop(0, tm, start, 0, unroll=DMA_UNROLL)
    lax.fori_loop(0, tm, wait, 0, unroll=DMA_UNROLL)


def _dispatch(h2, g_ffn, dest0, dest1, pad_start, n_used, xs_rows, tm, te):
    N, D = h2.shape
    nt = N // tm
    idx = pl.BlockSpec((1, 1, tm), lambda i, pad, nu: (i, 0, 0), memory_space=pltpu.SMEM)
    grid_spec = pltpu.PrefetchScalarGridSpec(
        num_scalar_prefetch=2,
        grid=(nt,),
        in_specs=[idx, idx, pl.BlockSpec((tm, D), lambda i, pad, nu: (i, 0)),
                  pl.BlockSpec(g_ffn.shape, lambda i, pad, nu: (0, 0))],
        out_specs=pl.BlockSpec(memory_space=pl.ANY),
        scratch_shapes=[pltpu.VMEM((tm * ROW_PIECES, LANES), F32), pltpu.VMEM((te * ROW_PIECES, LANES), F32),
                        pltpu.SemaphoreType.DMA(())],
    )
    return pl.pallas_call(
        _dispatch_kernel,
        grid_spec=grid_spec,
        out_shape=jax.ShapeDtypeStruct(((xs_rows + te) * ROW_PIECES, LANES), F32),
        compiler_params=_params("arbitrary"),
        name="dispatch",
    )(pad_start, n_used, dest0.reshape(nt, 1, tm), dest1.reshape(nt, 1, tm), h2, g_ffn)


def _experts_kernel(te_ref, nu_ref, x_ref, wgu_ref, wdn_ref, y_ref, wgu_bf_ref, wdn_bf_ref):
    j = pl.program_id(0)
    te = x_ref.shape[0] // ROW_PIECES

    @pl.when((j == 0) | (te_ref[j] != te_ref[jnp.maximum(j - 1, 0)]))
    def _():
        wgu_bf_ref[...] = wgu_ref[0].astype(BF16)
        wdn_bf_ref[...] = wdn_ref[0].astype(BF16)

    @pl.when(j < nu_ref[0])
    def _():
        gu = _dot(_from_token_tiles(x_ref, te).astype(BF16), wgu_bf_ref[...])
        gate, up = gu[:, :D_EXPERT], gu[:, D_EXPERT:]
        act = gate / (1.0 + jnp.exp(-gate)) * up
        _to_token_tiles(_dot(act.astype(BF16), wdn_bf_ref[...]), y_ref)

    @pl.when(j >= nu_ref[0])
    def _():
        y_ref[...] = jnp.zeros_like(y_ref)


def _experts(xs, wgu, wdn, tile_expert, n_used, xs_rows, te):
    D = D_MODEL
    tiles = pl.BlockSpec((te * ROW_PIECES, LANES), lambda j, te_, nu: (j, 0))
    grid_spec = pltpu.PrefetchScalarGridSpec(
        num_scalar_prefetch=2,
        grid=(xs_rows // te,),
        in_specs=[tiles,
                  pl.BlockSpec((1, D, 2 * D_EXPERT), lambda j, te_, nu: (te_[j], 0, 0)),
                  pl.BlockSpec((1, D_EXPERT, D), lambda j, te_, nu: (te_[j], 0, 0))],
        out_specs=tiles,
        scratch_shapes=[pltpu.VMEM((D, 2 * D_EXPERT), BF16), pltpu.VMEM((D_EXPERT, D), BF16)],
    )
    return pl.pallas_call(
        _experts_kernel,
        grid_spec=grid_spec,
        out_shape=jax.ShapeDtypeStruct((xs_rows * ROW_PIECES, LANES), F32),
        compiler_params=_params("arbitrary"),
        name="experts",
    )(tile_expert, n_used, xs, wgu, wdn)


def _combine_kernel(d0_ref, d1_ref, h_ref, info_ref, gfin_ref, ys_ref, o_ref, y0_ref, y1_ref, sem, *, final_norm):
    tm = h_ref.shape[0]

    def start(r, _):
        _token_copy(ys_ref, d0_ref[0, 0, r], y0_ref, r, sem).start(priority=0)
        _token_copy(ys_ref, d1_ref[0, 0, r], y1_ref, r, sem).start(priority=1)
        return 0

    def wait(r, _):
        _token_copy(ys_ref, d0_ref[0, 0, r], y0_ref, r, sem).wait()
        _token_copy(ys_ref, d1_ref[0, 0, r], y1_ref, r, sem).wait()
        return 0

    lax.fori_loop(0, tm, start, 0, unroll=DMA_UNROLL)
    lax.fori_loop(0, tm, wait, 0, unroll=DMA_UNROLL)
    info = info_ref[...]
    out = (h_ref[...] + info[:, INFO_G0:INFO_G0 + 1] * _from_token_tiles(y0_ref, tm)
           + info[:, INFO_G1:INFO_G1 + 1] * _from_token_tiles(y1_ref, tm))
    if final_norm:
        out = _rmsnorm(out, gfin_ref[...])
    o_ref[...] = out


def _combine(h2, info, g_final, ys, dest0, dest1, tm, final_norm):
    N, D = h2.shape
    nt = N // tm
    idx = pl.BlockSpec((1, 1, tm), lambda i: (i, 0, 0), memory_space=pltpu.SMEM)
    return pl.pallas_call(
        functools.partial(_combine_kernel, final_norm=final_norm),
        grid=(nt,),
        in_specs=[idx, idx, pl.BlockSpec((tm, D), lambda i: (i, 0)), pl.BlockSpec((tm, LANES), lambda i: (i, 0)),
                  _full(g_final.shape), pl.BlockSpec(memory_space=pl.ANY)],
        out_specs=pl.BlockSpec((tm, D), lambda i: (i, 0)),
        out_shape=jax.ShapeDtypeStruct((N, D), F32),
        scratch_shapes=[pltpu.VMEM((tm * ROW_PIECES, LANES), F32), pltpu.VMEM((tm * ROW_PIECES, LANES), F32),
                        pltpu.SemaphoreType.DMA(())],
        compiler_params=_params("arbitrary"),
        name="combine",
    )(dest0.reshape(nt, 1, tm), dest1.reshape(nt, 1, tm), h2, info, g_final, ys)


IN_SIZES = (NSA_WIDTH,) + (KV_WIDTH,) * 6 + (NSA_HEADS * 3,) + (CONV_WIDTH,) * 3 + (HGRN_WIDTH,) * 4
IN_OFF = tuple(int(v) for v in np.cumsum((0,) + IN_SIZES))


def _rope_tables(T):
    half = HEAD_DIM // 2
    inv = ROPE_THETA ** (-jnp.arange(half, dtype=F32) / half)
    ang = jnp.arange(T).astype(F32)[:, None] * inv[None, :]
    cos, sin = jnp.cos(ang), jnp.sin(ang)
    zero = jnp.zeros_like(sin)
    reps = LANES // HEAD_DIM
    cos_t = jnp.tile(jnp.concatenate([cos, cos], axis=1), (1, reps))
    sin_a = jnp.tile(jnp.concatenate([-sin, zero], axis=1), (1, reps))
    sin_b = jnp.tile(jnp.concatenate([zero, sin], axis=1), (1, reps))
    return cos_t, sin_a, sin_b


def _cmp_to_sel(T):
    nc_pad, ns = T // CMP_STRIDE, T // SEL_BLOCK
    nc = (T - CMP_LEN) // CMP_STRIDE + 1
    cs = np.arange(nc_pad) * CMP_STRIDE
    ss = np.arange(ns) * SEL_BLOCK
    ov = np.clip(np.minimum(cs[:, None] + CMP_LEN, ss[None, :] + SEL_BLOCK)
                 - np.maximum(cs[:, None], ss[None, :]), 0, None) / CMP_LEN
    ov[nc:] = 0.0
    return jnp.asarray(np.pad(ov, ((0, 0), (0, LANES - ns))), dtype=BF16)


def _q_weight(w_in_l):
    wq = w_in_l[:, IN_OFF[0]:IN_OFF[1]].reshape(D_MODEL, NSA_KV_HEADS, NSA_GROUP, 1, HEAD_DIM)
    half = jnp.eye(NSA_KV_HEADS, dtype=wq.dtype)[None, :, None, :, None]
    return (wq * half).reshape(D_MODEL, Q_AUG_WIDTH)


def _compress_weights(pe, w1, w2):
    eye = jnp.eye(NSA_KV_HEADS, dtype=F32)
    half_len = CMP_LEN // 2
    w1r = w1.reshape(2, 2, half_len, HEAD_DIM, CMP_HIDDEN)
    w1_aug = jnp.einsum("sxldm,hg->sxlhdgm", w1r, eye).reshape(2, 2, half_len * KV_WIDTH, NSA_KV_HEADS * CMP_HIDDEN)
    w2_aug = jnp.einsum("smd,hg->shmgd", w2, eye).reshape(2, NSA_KV_HEADS * CMP_HIDDEN, KV_WIDTH)
    pe_r = jnp.broadcast_to(pe.reshape(2, 2, half_len, 1, HEAD_DIM), (2, 2, half_len, NSA_KV_HEADS, HEAD_DIM))
    pe_aug = jnp.pad(pe_r.reshape(2, 2, half_len * KV_WIDTH), ((0, 0), (0, SUBLANES - 2), (0, 0)))
    return pe_aug, w1_aug[:, 0].astype(BF16), w1_aug[:, 1].astype(BF16), w2_aug.astype(BF16)


def _gate_weight(w_in_l):
    wg = w_in_l[:, IN_OFF[7]:IN_OFF[8]].reshape(D_MODEL, NSA_KV_HEADS, NSA_GROUP * 3)
    return jnp.pad(wg, ((0, 0), (0, 0), (0, LANES - NSA_GROUP * 3))).reshape(D_MODEL, NSA_KV_HEADS * LANES)


def _segments(counts, n_tok, te):
    c0 = counts[0, EXPERT_LANE0:EXPERT_LANE0 + N_EXPERTS].astype(jnp.int32)
    c1 = counts[1, EXPERT_LANE0:EXPERT_LANE0 + N_EXPERTS].astype(jnp.int32)
    tiles = (c0 + c1 + te - 1) // te
    tile_end = jnp.cumsum(tiles)
    off = (tile_end - tiles) * te
    tab = jnp.zeros((SUBLANES, LANES), F32)
    tab = tab.at[0, EXPERT_LANE0:EXPERT_LANE0 + N_EXPERTS].set(off.astype(F32))
    tab = tab.at[1, EXPERT_LANE0:EXPERT_LANE0 + N_EXPERTS].set((off + c0).astype(F32))
    n_tiles = (2 * n_tok) // te + N_EXPERTS
    tile_expert = jnp.sum(tile_end[None, :] <= jnp.arange(n_tiles)[:, None], axis=1)
    tile_expert = jnp.minimum(tile_expert, N_EXPERTS - 1).astype(jnp.int32)
    return tab, off + c0 + c1, tile_expert, tile_end[-1:].astype(jnp.int32)


def kernel(x, mem, w_in, conv_w, cmp_pe, cmp_w1, cmp_w2, hgrn_lb_logits, hgrn_norm, w_out, cross_wq, cross_wkv,
           cross_wo, router_group_w, router_group_b, router_expert_w, router_expert_b, expert_w_gate_up,
           expert_w_down, norm_mix, norm_cross, norm_mem, norm_ffn, norm_final):
    B, T, D = x.shape
    depth = w_in.shape[0]
    n_tok = B * T
    tm = min(512, T)
    tm_row = min(256, T)
    te = 256

    cos_t, sin_a, sin_b = _rope_tables(T)
    c2s = _cmp_to_sel(T)
    ltri = jnp.asarray(np.tril(np.ones((tm, tm), np.float32), -1), dtype=BF16)

    p_lb = jax.nn.softmax(hgrn_lb_logits.astype(F32), axis=0)
    lower_bounds = jnp.cumsum(p_lb, axis=0) - p_lb[0:1]
    xs_rows = 2 * n_tok + N_EXPERTS * te

    h = x
    for l in range(depth):
        wl = w_in[l]
        piece = lambda a, b: wl[:, IN_OFF[a]:IN_OFF[b]].astype(BF16)
        cw = jnp.pad(conv_w[l], ((0, SUBLANES - CONV_K), (0, 0)))
        (q, qr, kc, vc, ka, vs, kw, vw, gs, o_conv, hg) = _inproj(
            h, norm_mix[l][None], cos_t, sin_a, sin_b, _q_weight(wl).astype(BF16), piece(1, 7),
            _gate_weight(wl).astype(BF16), piece(8, 11), piece(11, 15), cw, tm)

        chunks = lambda a: a.reshape(B, T // CMP_STRIDE, CMP_STRIDE * KV_WIDTH)
        kvc = _compress(jnp.stack([chunks(kc), chunks(vc)]), *_compress_weights(cmp_pe[l], cmp_w1[l], cmp_w2[l]))
        o_nsa = _nsa(q, qr, kvc, ka, vs, kw, vw, gs, c2s)

        lb = lower_bounds[l][None]
        lb_params = jnp.concatenate([jnp.log(jnp.maximum(lb, LB_FLOOR)), jnp.log1p(-lb), 1.0 - lb,
                                     jnp.zeros((SUBLANES - 3, HGRN_WIDTH), F32)], axis=0)
        o_hgrn = _hgrn(hg, lb_params, hgrn_norm[l][None], tm)

        kmem, vmem = _memkv(mem, norm_mem[l][None], cross_wkv[l].astype(BF16))

        wr = jnp.zeros((D, LANES), F32).at[:, :N_GROUPS].set(router_group_w[l])
        wr = wr.at[:, EXPERT_LANE0:EXPERT_LANE0 + N_EXPERTS].set(router_expert_w[l])
        br = jnp.zeros((1, LANES), F32).at[0, :N_GROUPS].set(router_group_b[l])
        br = br.at[0, EXPERT_LANE0:EXPERT_LANE0 + N_EXPERTS].set(router_expert_b[l])
        wr_hi = wr.astype(BF16)
        wr_lo = (wr - wr_hi.astype(F32)).astype(BF16)
        h2, info, counts = _post(h, o_nsa, o_conv, o_hgrn, w_out[l].astype(BF16), norm_cross[l][None],
                                 cross_wq[l].astype(BF16), kmem, vmem, cross_wo[l].astype(BF16), norm_ffn[l][None],
                                 wr_hi, wr_lo, br, ltri, tm)

        h2 = h2.reshape(n_tok, D)
        info = info.reshape(n_tok, LANES)
        tab, pad_start, tile_expert, n_used = _segments(counts, n_tok, te)
        dest = _route(info, tab, tm)
        dest0, dest1 = dest[:, 0], dest[:, 1]
        xs = _dispatch(h2, norm_ffn[l][None], dest0, dest1, pad_start, n_used, xs_rows, tm_row, te)
        ys = _experts(xs, expert_w_gate_up[l], expert_w_down[l], tile_expert, n_used, xs_rows, te)
        h = _combine(h2, info, norm_final[None], ys, dest0, dest1, tm_row, l == depth - 1).reshape(B, T, D)
    return h
```

```python
import functools

import numpy as np
import jax
import jax.numpy as jnp
from jax import lax
from jax.experimental import pallas as pl
from jax.experimental.pallas import tpu as pltpu

F32 = jnp.float32
BF16 = jnp.bfloat16

D_MODEL = 1024
HEAD_DIM = 64
NSA_HEADS = 8
NSA_KV_HEADS = 2
NSA_GROUP = NSA_HEADS // NSA_KV_HEADS
NSA_WIDTH = NSA_HEADS * HEAD_DIM
KV_WIDTH = NSA_KV_HEADS * HEAD_DIM
CMP_LEN = 32
CMP_STRIDE = 16
CMP_HIDDEN = 2 * HEAD_DIM
SEL_BLOCK = 64
N_SELECT = 16
WINDOW = 512
Q_BLOCK = 128
MASK_VALUE = -1e30
SEL_FORCE = 1e6
CONV_WIDTH = 256
CONV_K = 3
HGRN_HEADS = 4
HGRN_WIDTH = HGRN_HEADS * HEAD_DIM
HGRN_CHUNK = 64
HGRN_SUB = 16
LB_FLOOR = 1e-30
ROPE_THETA = 10000.0
CROSS_HEADS = 4
CROSS_WIDTH = CROSS_HEADS * HEAD_DIM
N_GROUPS = 4
EXPERTS_PER_GROUP = 8
N_EXPERTS = N_GROUPS * EXPERTS_PER_GROUP
D_EXPERT = 512
RMS_EPS = 1e-6
QK_SCALE = HEAD_DIM ** -0.5
LOG2E = 1.4426950408889634
HEAD_SHIFT = HEAD_DIM.bit_length() - 1
SEL_SHIFT = SEL_BLOCK.bit_length() - 1

LANES = 128
SUBLANES = 8
VMEM_LIMIT_BYTES = 48 * 1024 * 1024

Q_AUG_WIDTH = NSA_HEADS * LANES

EXPERT_LANE0 = 32
INFO_E0, INFO_E1, INFO_R0, INFO_R1, INFO_G0, INFO_G1 = 0, 1, 2, 3, 4, 5


def _dot(a, b):
    return jnp.dot(a, b, preferred_element_type=F32)


def _dot_nt(a, b):
    return lax.dot_general(a, b, (((1,), (1,)), ((), ())), preferred_element_type=F32)


def _split_bf16(x):
    hi = x.astype(BF16)
    lo = (x - hi.astype(F32)).astype(BF16)
    return hi, lo


def _rmsnorm(x, g):
    return x * lax.rsqrt(jnp.mean(x * x, axis=-1, keepdims=True) + RMS_EPS) * g


def _params(*sem):
    return pltpu.CompilerParams(dimension_semantics=sem, vmem_limit_bytes=VMEM_LIMIT_BYTES)


def _full(shape):
    nd = len(shape)
    return pl.BlockSpec(shape, lambda *_: (0,) * nd)


def _rope(x, cos, sin_a, sin_b):
    width = x.shape[1]
    reps = width // LANES
    if reps > 1:
        cos = jnp.concatenate([cos] * reps, axis=1)
        sin_a = jnp.concatenate([sin_a] * reps, axis=1)
        sin_b = jnp.concatenate([sin_b] * reps, axis=1)
    half = HEAD_DIM // 2
    return x * cos + pltpu.roll(x, width - half, 1) * sin_a + pltpu.roll(x, half, 1) * sin_b


def _inproj_kernel(h_ref, g_ref, cos_ref, sa_ref, sb_ref, wq_ref, wkv_ref, wg_ref, wc_ref, wh_ref, cw_ref,
                   q_ref, qr_ref, kc_ref, vc_ref, ks_ref, vs_ref, kw_ref, vw_ref, gs_ref, oc_ref, hg_ref,
                   carry_ref):
    tm = h_ref.shape[1]

    @pl.when(pl.program_id(1) == 0)
    def _():
        carry_ref[...] = jnp.zeros_like(carry_ref)

    xb = _rmsnorm(h_ref[0], g_ref[...]).astype(BF16)
    cos, sa, sb = cos_ref[...], sa_ref[...], sb_ref[...]

    q = _dot(xb, wq_ref[...]) * (QK_SCALE * LOG2E)
    q_ref[0] = q.astype(BF16)
    qr_ref[0] = _rope(q, cos, sa, sb).astype(BF16)

    kv = _dot(xb, wkv_ref[...])
    kw_ = KV_WIDTH
    kc_ref[0] = kv[:, 0 * kw_:1 * kw_].astype(BF16)
    vc_ref[0] = kv[:, 1 * kw_:2 * kw_].astype(BF16)
    pos = pl.program_id(1) * tm + lax.broadcasted_iota(jnp.int32, (tm, LANES), 0)
    blk = lax.broadcasted_iota(jnp.int32, (tm, LANES), 1) == lax.shift_right_logical(pos, SEL_SHIFT)
    ks_ref[0, :, 0:kw_] = _rope(kv[:, 2 * kw_:3 * kw_], cos, sa, sb).astype(BF16)
    ks_ref[0, :, kw_:kw_ + LANES] = jnp.where(blk, 1.0, 0.0).astype(BF16)
    vs_ref[0] = kv[:, 3 * kw_:4 * kw_].astype(BF16)
    kw_ref[0] = _rope(kv[:, 4 * kw_:5 * kw_], cos, sa, sb).astype(BF16)
    vw_ref[0] = kv[:, 5 * kw_:6 * kw_].astype(BF16)

    gl = _dot(xb, wg_ref[...])
    gs_ref[0] = 1.0 / (1.0 + jnp.exp(-gl))

    cv = _dot(xb, wc_ref[...])
    cb = cv[:, 0:CONV_WIDTH]
    u = cv[:, CONV_WIDTH:2 * CONV_WIDTH] * cv[:, 2 * CONV_WIDTH:3 * CONV_WIDTH]
    ext = jnp.concatenate([carry_ref[...], u], axis=0)
    u1 = pltpu.roll(ext, 1, 0)[SUBLANES:]
    u2 = pltpu.roll(ext, 2, 0)[SUBLANES:]
    cw = cw_ref[...]
    oc_ref[0] = (cb * (cw[0:1] * u2 + cw[1:2] * u1 + cw[2:3] * u)).astype(BF16)
    carry_ref[...] = u[tm - SUBLANES:tm]

    hg_ref[0] = _dot(xb, wh_ref[...])


def _inproj(h, gain, cos, sin_a, sin_b, wq, wkv, wg, wc, wh, cw, tm):
    B, T, D = h.shape
    nt = T // tm
    tok = lambda w: pl.BlockSpec((1, tm, w), lambda b, t: (b, t, 0))
    tab = pl.BlockSpec((tm, LANES), lambda b, t: (t, 0))
    kvw = KV_WIDTH
    out_w = [(Q_AUG_WIDTH, BF16), (Q_AUG_WIDTH, BF16), (kvw, BF16), (kvw, BF16), (kvw + LANES, BF16), (kvw, BF16),
             (kvw, BF16), (kvw, BF16), (2 * LANES, F32), (CONV_WIDTH, BF16), (4 * HGRN_WIDTH, F32)]
    return pl.pallas_call(
        _inproj_kernel,
        grid=(B, nt),
        in_specs=[tok(D), _full(gain.shape), tab, tab, tab, _full(wq.shape), _full(wkv.shape), _full(wg.shape),
                  _full(wc.shape), _full(wh.shape), _full(cw.shape)],
        out_specs=[tok(w) for w, _ in out_w],
        out_shape=[jax.ShapeDtypeStruct((B, T, w), dt) for w, dt in out_w],
        scratch_shapes=[pltpu.VMEM((SUBLANES, CONV_WIDTH), F32)],
        compiler_params=_params("arbitrary", "arbitrary"),
        name="inproj",
    )(h, gain, cos, sin_a, sin_b, wq, wkv, wg, wc, wh, cw)


def _compress_kernel(x_ref, pe_ref, w1a_ref, w1b_ref, w2_ref, o_ref):
    x = x_ref[0, 0]
    n = x.shape[0]
    w1a, w1b = w1a_ref[0], w1b_ref[0]
    pe_a = jnp.broadcast_to(pe_ref[0, 0:1], (SUBLANES, x.shape[1])).astype(BF16)
    pe_b = jnp.broadcast_to(pe_ref[0, 1:2], (SUBLANES, x.shape[1])).astype(BF16)
    pe_term = (_dot(pe_a, w1a) + _dot(pe_b, w1b))[0:1]
    hid = _dot(x, w1a) + pltpu.roll(_dot(x, w1b), n - 1, 0) + pe_term
    act = hid / (1.0 + jnp.exp(-hid))
    o_ref[0, 0] = _dot(act.astype(BF16), w2_ref[0]).astype(BF16)


def _compress(xkv, pe, w1a, w1b, w2):
    _, B, n, width = xkv.shape
    hid = NSA_KV_HEADS * CMP_HIDDEN
    per_kv = lambda *shape: pl.BlockSpec((1,) + shape, lambda s, b: (s,) + (0,) * len(shape))
    return pl.pallas_call(
        _compress_kernel,
        grid=(2, B),
        in_specs=[pl.BlockSpec((1, 1, n, width), lambda s, b: (s, b, 0, 0)), per_kv(SUBLANES, width),
                  per_kv(width, hid), per_kv(width, hid), per_kv(hid, KV_WIDTH)],
        out_specs=pl.BlockSpec((1, 1, n, KV_WIDTH), lambda s, b: (s, b, 0, 0)),
        out_shape=jax.ShapeDtypeStruct((2, B, n, KV_WIDTH), BF16),
        compiler_params=_params("arbitrary", "arbitrary"),
        name="compress",
    )(xkv, pe, w1a, w1b, w2)


def _masked_softmax2(s, mask):
    s = jnp.where(mask, s, MASK_VALUE)
    m = jnp.max(s, axis=1, keepdims=True)
    e = jnp.exp2(s - m)
    den = jnp.sum(e, axis=1, keepdims=True)
    return e, jnp.where(m > 0.5 * MASK_VALUE, 1.0 / den, 0.0)


def _nsa_kernel(q_ref, qr_ref, kc_ref, vc_ref, ka_ref, vs_ref, kw_ref, vw_ref, gs_ref, c2s_ref, o_ref, *, key_tile):
    i = pl.program_id(1)
    qb, g_, n_h = Q_BLOCK, NSA_GROUP, NSA_KV_HEADS
    rows = g_ * qb
    heads = range(n_h)

    def q_groups(ref, h):
        return [ref[0, :, (h * g_ + g) * LANES:(h * g_ + g + 1) * LANES] for g in range(g_)]

    t_row = i * qb + (lax.broadcasted_iota(jnp.int32, (rows, 1), 0) & (qb - 1))
    t_q = i * qb + lax.broadcasted_iota(jnp.int32, (qb, 1), 0)

    kc, vc = kc_ref[0, 0], vc_ref[0, 0]
    c_end = lax.broadcasted_iota(jnp.int32, (1, kc.shape[0]), 1) * CMP_STRIDE + (CMP_LEN - 1)
    o_c, imp = [], []
    for h in heads:
        e_c, inv_c = _masked_softmax2(_dot_nt(jnp.concatenate(q_groups(q_ref, h), axis=0), kc), c_end <= t_row)
        p_c = e_c * inv_c
        o_c.append(_dot(p_c.astype(BF16), vc))
        p_g = p_c[0:qb]
        for g in range(1, g_):
            p_g = p_g + p_c[g * qb:(g + 1) * qb]
        imp.append(_dot(p_g.astype(BF16), c2s_ref[...]))

    wk = WINDOW + qb
    w_start = pl.multiple_of(jnp.maximum(i * qb - WINDOW, 0), qb)
    kwin = kw_ref[0, pl.ds(w_start, wk), :]
    vwin = vw_ref[0, pl.ds(w_start, wk), :]
    dist = t_row - (w_start + lax.broadcasted_iota(jnp.int32, (1, wk), 1))
    w_mask = (dist >= 0) & (dist < WINDOW)
    o_w = []
    for h in heads:
        e_w, inv_w = _masked_softmax2(_dot_nt(jnp.concatenate(q_groups(qr_ref, h), axis=0), kwin), w_mask)
        o_w.append(_dot(e_w.astype(BF16), vwin) * inv_w)

    cur = lax.shift_right_logical(t_q, SEL_SHIFT)
    j = lax.broadcasted_iota(jnp.int32, (qb, LANES), 1)
    jf = j.astype(F32)
    forced = (j == 0) | (j == cur) | (j == cur - 1)
    q_aug = []
    for h in heads:
        work = jnp.where(forced | (j > cur), -jnp.inf, imp[h])
        sel = jnp.where(forced, 1.0, 0.0)
        for _ in range(N_SELECT - 3):
            mx = jnp.max(work, axis=1, keepdims=True)
            first = jnp.min(jnp.where(work == mx, jf, float(LANES)), axis=1, keepdims=True)
            pick = jf == first
            sel = jnp.where(pick, 1.0, sel)
            work = jnp.where(pick, -jnp.inf, work)
        sel_bias = jnp.where(sel > 0, 0.0, MASK_VALUE).astype(BF16)
        q_aug.append(jnp.concatenate([jnp.concatenate([qg, sel_bias], axis=1) for qg in q_groups(qr_ref, h)],
                                     axis=0))

    def sel_tile(kt, width, carry, causal):
        start = pl.multiple_of(kt * key_tile, key_tile)
        ka = ka_ref[0, pl.ds(start, width), :]
        v = vs_ref[0, pl.ds(start, width), :]
        out = []
        for h in heads:
            m, l, acc = carry[h]
            s = _dot_nt(q_aug[h], ka)
            if causal:
                s = jnp.where(start + lax.broadcasted_iota(jnp.int32, (1, width), 1) <= t_row, s, MASK_VALUE)
            m_new = jnp.maximum(m, jnp.max(s, axis=1, keepdims=True))
            alpha = jnp.exp2(m - m_new)
            p = jnp.exp2(s - m_new)
            l = alpha * l + jnp.sum(p, axis=1, keepdims=True)
            acc = alpha * acc + _dot(p.astype(BF16), v)
            out.append((m_new, l, acc))
        return tuple(out)

    n_full = (i * qb) // key_tile
    init = tuple((jnp.full((rows, 1), MASK_VALUE, F32), jnp.zeros((rows, 1), F32), jnp.zeros((rows, LANES), F32))
                 for _ in heads)
    carry = lax.fori_loop(0, n_full, lambda kt, c: sel_tile(kt, key_tile, c, False), init)
    carry = sel_tile(n_full, key_tile, carry, True)
    o_s = [acc * (1.0 / l) for _, l, acc in carry]

    gs = gs_ref[0]
    low_half = lax.broadcasted_iota(jnp.int32, (qb, LANES), 1) < HEAD_DIM
    out_groups = []
    for h in heads:
        mixed = []
        for g in range(g_):
            r = slice(g * qb, (g + 1) * qb)
            c = h * LANES + 3 * g
            mixed.append(gs[:, c:c + 1] * o_c[h][r] + gs[:, c + 1:c + 2] * o_s[h][r] + gs[:, c + 2:c + 3] * o_w[h][r])
        for pair in range(g_ // 2):
            a, b = mixed[2 * pair], mixed[2 * pair + 1]
            if h == 0:
                out_groups.append(jnp.where(low_half, a, pltpu.roll(b, HEAD_DIM, 1)))
            else:
                out_groups.append(jnp.where(low_half, pltpu.roll(a, HEAD_DIM, 1), b))
    o_ref[0] = jnp.concatenate(out_groups, axis=1).astype(o_ref.dtype)


def _nsa(q, qr, kvc, ka, vs, kw, vw, gs, c2s):
    B, T, _ = q.shape
    assert T // SEL_BLOCK <= LANES
    nc = kvc.shape[2]
    key_tile = min(1024, T)
    blk = lambda w: pl.BlockSpec((1, Q_BLOCK, w), lambda b, i: (b, i, 0))
    seq = lambda w: pl.BlockSpec((1, T, w), lambda b, i: (b, 0, 0))
    cmp_spec = lambda s: pl.BlockSpec((1, 1, nc, KV_WIDTH), lambda b, i: (s, b, 0, 0))
    return pl.pallas_call(
        functools.partial(_nsa_kernel, key_tile=key_tile),
        grid=(B, T // Q_BLOCK),
        in_specs=[blk(Q_AUG_WIDTH), blk(Q_AUG_WIDTH), cmp_spec(0), cmp_spec(1), seq(KV_WIDTH + LANES), seq(KV_WIDTH),
                  seq(KV_WIDTH), seq(KV_WIDTH), blk(2 * LANES), _full(c2s.shape)],
        out_specs=blk(NSA_WIDTH),
        out_shape=jax.ShapeDtypeStruct((B, T, NSA_WIDTH), BF16),
        compiler_params=_params("arbitrary", "arbitrary"),
        name="nsa",
    )(q, qr, kvc, kvc, ka, vs, kw, vw, gs, c2s)


def _hgrn_kernel(x_ref, lbp_ref, ng_ref, o_ref, st_ref):
    w = HGRN_WIDTH
    L = HGRN_CHUNK
    n_chunks = x_ref.shape[1] // L

    @pl.when(pl.program_id(1) == 0)
    def _():
        st_ref[...] = jnp.zeros_like(st_ref)

    log_lb, log_1m_lb, one_m_lb = lbp_ref[0:1], lbp_ref[1:2], lbp_ref[2:3]
    ng = ng_ref[...]
    ri = lax.broadcasted_iota(jnp.int32, (L, L), 0)
    ci = lax.broadcasted_iota(jnp.int32, (L, L), 1)
    tril = jnp.where(ci <= ri, 1.0, 0.0).astype(BF16)
    hr = lax.shift_right_logical(lax.broadcasted_iota(jnp.int32, (w, w), 0), HEAD_SHIFT)
    hc = lax.shift_right_logical(lax.broadcasted_iota(jnp.int32, (w, w), 1), HEAD_SHIFT)
    same_head = hr == hc
    head_ones = jnp.where(same_head, 1.0, 0.0).astype(BF16)
    sub_row = lax.broadcasted_iota(jnp.int32, (L, 1), 0) & (HGRN_SUB - 1)
    lane_head = lax.shift_right_logical(lax.broadcasted_iota(jnp.int32, (1, w), 1), HEAD_SHIFT)

    def chunk(c, _):
        r0 = pl.multiple_of(c * L, L)
        q = x_ref[0, pl.ds(r0, L), 0 * w:1 * w] * QK_SCALE
        z = x_ref[0, pl.ds(r0, L), 1 * w:2 * w]
        v = x_ref[0, pl.ds(r0, L), 2 * w:3 * w]
        g = x_ref[0, pl.ds(r0, L), 3 * w:4 * w]

        log_sig = jnp.minimum(z, 0.0) - jnp.log(1.0 + jnp.exp(-jnp.abs(z)))
        bb = log_1m_lb + log_sig
        log_f = jnp.maximum(log_lb, bb) + jnp.log(1.0 + jnp.exp(-jnp.abs(log_lb - bb)))
        k = one_m_lb / (1.0 + jnp.exp(z))

        lf_hi, lf_lo = _split_bf16(log_f)
        b = _dot(tril, lf_hi) + _dot(tril, lf_lo)
        b_last = b[L - 1:L]
        st = st_ref[...]
        o = _dot_nt((q * jnp.exp(b)).astype(BF16), st.astype(BF16))
        for d in range(HGRN_SUB):
            if d == 0:
                wgt = q * k
                vs = v
            else:
                ok = sub_row >= d
                diff = jnp.where(ok, b - pltpu.roll(b, d, 0), 0.0)
                wgt = jnp.where(ok, q * pltpu.roll(k, d, 0) * jnp.exp(diff), 0.0)
                vs = pltpu.roll(v, d, 0)
            o = o + _dot(wgt.astype(BF16), head_ones) * vs
        parts = [jnp.zeros((HGRN_SUB, w), F32)]
        for lo in range(HGRN_SUB, L, HGRN_SUB):
            r = b[lo - 1:lo]
            qs = q[lo:lo + HGRN_SUB] * jnp.exp(b[lo:lo + HGRN_SUB] - r)
            ks = (k[0:lo] * jnp.exp(r - b[0:lo])).astype(BF16)
            acc = jnp.zeros((HGRN_SUB, w), F32)
            for hd in range(HGRN_HEADS):
                in_head = lane_head == hd
                a = _dot_nt(jnp.where(in_head, qs, 0.0).astype(BF16), ks)
                acc = acc + _dot(a.astype(BF16), jnp.where(in_head, v[0:lo], 0.0).astype(BF16))
            parts.append(acc)
        o = o + jnp.concatenate(parts, axis=0)
        kd = k * jnp.exp(b_last - b)
        upd = _dot(v.T.astype(BF16), kd.astype(BF16))
        st_ref[...] = st * jnp.exp(b_last) + jnp.where(same_head, upd, 0.0)

        o2_hi, o2_lo = _split_bf16(o * o)
        ms = (_dot(o2_hi, head_ones) + _dot(o2_lo, head_ones)) * (1.0 / HEAD_DIM)
        y = o * lax.rsqrt(ms + RMS_EPS) * ng
        o_ref[0, pl.ds(r0, L), :] = (y * (g / (1.0 + jnp.exp(-g)))).astype(o_ref.dtype)
        return 0

    lax.fori_loop(0, n_chunks, chunk, 0, unroll=2)


def _hgrn(hg, lb_params, norm_g, rows):
    B, T, width = hg.shape
    return pl.pallas_call(
        _hgrn_kernel,
        grid=(B, T // rows),
        in_specs=[pl.BlockSpec((1, rows, width), lambda b, t: (b, t, 0)), _full(lb_params.shape), _full(norm_g.shape)],
        out_specs=pl.BlockSpec((1, rows, HGRN_WIDTH), lambda b, t: (b, t, 0)),
        out_shape=jax.ShapeDtypeStruct((B, T, HGRN_WIDTH), BF16),
        scratch_shapes=[pltpu.VMEM((HGRN_WIDTH, HGRN_WIDTH), F32)],
        compiler_params=_params("arbitrary", "arbitrary"),
        name="hgrn",
    )(hg, lb_params, norm_g)


def _memkv_kernel(m_ref, g_ref, w_ref, k_ref, v_ref):
    kv = _dot(_rmsnorm(m_ref[0], g_ref[...]).astype(BF16), w_ref[...])
    k_ref[0] = kv[:, :CROSS_WIDTH].astype(BF16)
    v_ref[0] = kv[:, CROSS_WIDTH:].astype(BF16)


def _memkv(mem, gain, wkv):
    B, M, D = mem.shape
    out = pl.BlockSpec((1, M, CROSS_WIDTH), lambda b: (b, 0, 0))
    return pl.pallas_call(
        _memkv_kernel,
        grid=(B,),
        in_specs=[pl.BlockSpec((1, M, D), lambda b: (b, 0, 0)), _full(gain.shape), _full(wkv.shape)],
        out_specs=[out, out],
        out_shape=[jax.ShapeDtypeStruct((B, M, CROSS_WIDTH), BF16)] * 2,
        compiler_params=_params("arbitrary"),
        name="memkv",
    )(mem, gain, wkv)


def _post_kernel(h_ref, on_ref, oc_ref, oh_ref, wo_ref, gc_ref, wq_ref, km_ref, vm_ref, wco_ref, gf_ref,
                 wr_hi_ref, wr_lo_ref, br_ref, ltri_ref, h2_ref, info_ref, cnt_ref, carry_ref):
    tm = h_ref.shape[1]
    first = (pl.program_id(0) == 0) & (pl.program_id(1) == 0)

    @pl.when(first)
    def _():
        carry_ref[...] = jnp.zeros_like(carry_ref)

    a, b = NSA_WIDTH, NSA_WIDTH + CONV_WIDTH
    mix = _dot(on_ref[0], wo_ref[0:a]) + _dot(oc_ref[0], wo_ref[a:b]) + _dot(oh_ref[0], wo_ref[b:])
    h1 = h_ref[0] + mix

    q = _dot(_rmsnorm(h1, gc_ref[...]).astype(BF16), wq_ref[...]) * QK_SCALE
    km, vm = km_ref[0], vm_ref[0]
    lane_head = lax.shift_right_logical(lax.broadcasted_iota(jnp.int32, (1, CROSS_WIDTH), 1), HEAD_SHIFT)
    o = jnp.zeros((tm, CROSS_WIDTH), F32)
    for hd in range(CROSS_HEADS):
        in_head = lane_head == hd
        s = _dot_nt(jnp.where(in_head, q, 0.0).astype(BF16), km)
        m = jnp.max(s, axis=1, keepdims=True)
        e = jnp.exp(s - m)
        p = e / jnp.sum(e, axis=1, keepdims=True)
        o = o + _dot(p.astype(BF16), jnp.where(in_head, vm, jnp.zeros_like(vm)))
    h2 = h1 + _dot(o.astype(BF16), wco_ref[...])
    h2_ref[0] = h2

    xn = _rmsnorm(h2, gf_ref[...])
    x_hi, x_lo = _split_bf16(xn)
    w_hi = wr_hi_ref[...]
    logits = _dot(x_hi, w_hi) + _dot(x_lo, w_hi) + _dot(x_hi, wr_lo_ref[...]) + br_ref[...]
    lane = lax.broadcasted_iota(jnp.int32, (tm, LANES), 1)
    lanef = lane.astype(F32)
    big = float(LANES)

    lg = jnp.where(lane < N_GROUPS, logits, -jnp.inf)
    mg = jnp.max(lg, axis=1, keepdims=True)
    p_grp_sel = 1.0 / jnp.sum(jnp.exp(lg - mg), axis=1, keepdims=True)
    grp = jnp.min(jnp.where(lg == mg, lanef, big), axis=1, keepdims=True)

    lo_lane = EXPERT_LANE0 + EXPERTS_PER_GROUP * grp
    in_grp = (lanef >= lo_lane) & (lanef < lo_lane + EXPERTS_PER_GROUP)
    le = jnp.where(in_grp, logits, -jnp.inf)
    me = jnp.max(le, axis=1, keepdims=True)
    ee = jnp.exp(le - me)
    pe = jnp.where(in_grp, ee / jnp.sum(ee, axis=1, keepdims=True), -1.0)
    p1 = jnp.max(pe, axis=1, keepdims=True)
    l1 = jnp.min(jnp.where(pe == p1, lanef, big), axis=1, keepdims=True)
    pe2 = jnp.where(lanef == l1, -1.0, pe)
    p2 = jnp.max(pe2, axis=1, keepdims=True)
    l2 = jnp.min(jnp.where(pe2 == p2, lanef, big), axis=1, keepdims=True)
    scale = p_grp_sel / (p1 + p2)
    g0, g1 = p1 * scale, p2 * scale

    oh0 = jnp.where(lanef == l1, 1.0, 0.0)
    oh1 = jnp.where(lanef == l2, 1.0, 0.0)
    ltri = ltri_ref[...]
    c0, c1 = carry_ref[0:1], carry_ref[1:2]
    r0 = jnp.sum((_dot(ltri, oh0.astype(BF16)) + c0) * oh0, axis=1, keepdims=True)
    r1 = jnp.sum((_dot(ltri, oh1.astype(BF16)) + c1) * oh1, axis=1, keepdims=True)
    c0 = c0 + jnp.sum(oh0, axis=0, keepdims=True)
    c1 = c1 + jnp.sum(oh1, axis=0, keepdims=True)
    carry_ref[0:1] = c0
    carry_ref[1:2] = c1
    cnt_ref[...] = carry_ref[...]

    info = jnp.zeros((tm, LANES), F32)
    for col, val in ((INFO_E0, l1 - EXPERT_LANE0), (INFO_E1, l2 - EXPERT_LANE0), (INFO_R0, r0), (INFO_R1, r1),
                     (INFO_G0, g0), (INFO_G1, g1)):
        info = jnp.where(lane == col, val, info)
    info_ref[0] = info


def _post(h, o_nsa, o_conv, o_hgrn, w_out, g_cross, wq, kmem, vmem, wco, g_ffn, wr_hi, wr_lo, br, ltri, tm):
    B, T, D = h.shape
    M = kmem.shape[1]
    tok = lambda w: pl.BlockSpec((1, tm, w), lambda b, t: (b, t, 0))
    memspec = pl.BlockSpec((1, M, CROSS_WIDTH), lambda b, t: (b, 0, 0))
    return pl.pallas_call(
        _post_kernel,
        grid=(B, T // tm),
        in_specs=[tok(D), tok(NSA_WIDTH), tok(CONV_WIDTH), tok(HGRN_WIDTH), _full(w_out.shape), _full(g_cross.shape),
                  _full(wq.shape), memspec, memspec, _full(wco.shape), _full(g_ffn.shape), _full(wr_hi.shape),
                  _full(wr_lo.shape), _full(br.shape), _full(ltri.shape)],
        out_specs=[tok(D), tok(LANES), _full((SUBLANES, LANES))],
        out_shape=[jax.ShapeDtypeStruct((B, T, D), F32), jax.ShapeDtypeStruct((B, T, LANES), F32),
                   jax.ShapeDtypeStruct((SUBLANES, LANES), F32)],
        scratch_shapes=[pltpu.VMEM((SUBLANES, LANES), F32)],
        compiler_params=_params("arbitrary", "arbitrary"),
        name="post",
    )(h, o_nsa, o_conv, o_hgrn, w_out, g_cross, wq, kmem, vmem, wco, g_ffn, wr_hi, wr_lo, br, ltri)


ROW_PIECES = D_MODEL // LANES
assert ROW_PIECES == SUBLANES


def _to_token_tiles(x, ref):
    tm = x.shape[0]
    for s in range(ROW_PIECES):
        ref[pl.ds(s, tm, stride=ROW_PIECES), :] = x[:, s * LANES:(s + 1) * LANES]


def _from_token_tiles(ref, tm):
    return jnp.concatenate([ref[pl.ds(s, tm, stride=ROW_PIECES), :] for s in range(ROW_PIECES)], axis=1)


def _token_copy(src, src_tok, dst, dst_tok, sem):
    rows = lambda t: pl.ds(pl.multiple_of(t * ROW_PIECES, ROW_PIECES), ROW_PIECES)
    return pltpu.make_async_copy(src.at[rows(src_tok)], dst.at[rows(dst_tok)], sem)


DMA_UNROLL = 8


def _route_kernel(info_ref, tab_ref, dest_ref):
    info = info_ref[...]
    lanef = lax.broadcasted_iota(jnp.int32, info.shape, 1).astype(F32)
    pick = lambda col, row: jnp.sum(jnp.where(lanef == info[:, col:col + 1] + EXPERT_LANE0, tab_ref[row:row + 1], 0.0),
                                    axis=1, keepdims=True)
    d0 = pick(INFO_E0, 0) + info[:, INFO_R0:INFO_R0 + 1]
    d1 = pick(INFO_E1, 1) + info[:, INFO_R1:INFO_R1 + 1]
    dest_ref[...] = jnp.where(lanef == 0.0, d0, jnp.where(lanef == 1.0, d1, 0.0)).astype(jnp.int32)


def _route(info, tab, tm):
    N = info.shape[0]
    blk = pl.BlockSpec((tm, LANES), lambda i: (i, 0))
    return pl.pallas_call(
        _route_kernel,
        grid=(N // tm,),
        in_specs=[blk, _full(tab.shape)],
        out_specs=blk,
        out_shape=jax.ShapeDtypeStruct((N, LANES), jnp.int32),
        compiler_params=_params("arbitrary"),
        name="route",
    )(info, tab)


def _dispatch_kernel(pad_ref, nu_ref, d0_ref, d1_ref, d0p_ref, d1p_ref, h_ref, g_ref, xs_ref, xn_ref, zero_ref, sem):
    tm = h_ref.shape[0]
    fill_rows = zero_ref.shape[0]

    @pl.when(pl.program_id(0) == 0)
    def _():
        zero_ref[...] = jnp.zeros_like(zero_ref)
        fill = lambda row: pltpu.make_async_copy(
            zero_ref, xs_ref.at[pl.ds(pl.multiple_of(row * ROW_PIECES, ROW_PIECES), fill_rows)], sem.at[0])
        for e in range(N_EXPERTS):
            fill(pad_ref[e]).start()
        for e in range(N_EXPERTS):
            fill(pad_ref[e]).wait()
        n_total = xs_ref.shape[0] // fill_rows
        te = fill_rows // ROW_PIECES

        def tail_start(j, _):
            fill(j * te).start()
            return 0

        def tail_wait(j, _):
            fill(j * te).wait()
            return 0

        lax.fori_loop(nu_ref[0], n_total, tail_start, 0)
        lax.fori_loop(nu_ref[0], n_total, tail_wait, 0)

    i = pl.program_id(0)
    slot = i % 2
    _to_token_tiles(_rmsnorm(h_ref[...], g_ref[...]), xn_ref.at[slot])

    def scatters(da_ref, db_ref, s, go):
        def body(r, _):
            go(_token_copy(xn_ref.at[s], r, xs_ref, da_ref[0, 0, r], sem.at[s]), 0)
            go(_token_copy(xn_ref.at[s], r, xs_ref, db_ref[0, 0, r], sem.at[s]), 1)
            return 0
        lax.fori_loop(0, tm, body, 0, unroll=DMA_UNROLL)

    start = lambda cp, prio: cp.start(priority=prio)
    wait = lambda cp, prio: cp.wait()
    scatters(d0_ref, d1_ref, slot, start)

    @pl.when(i > 0)
    def _():
        scatters(d0p_ref, d1p_ref, 1 - slot, wait)

    @pl.when(i == pl.num_programs(0) - 1)
    def _():
        scatters(d0_ref, d1_ref, slot, wait)


def _dispatch(h2, g_ffn, dest0, dest1, pad_start, n_used, xs_rows, tm, te):
    N, D = h2.shape
    nt = N // tm
    idx = pl.BlockSpec((1, 1, tm), lambda i, pad, nu: (i, 0, 0), memory_space=pltpu.SMEM)
    idx_prev = pl.BlockSpec((1, 1, tm), lambda i, pad, nu: (jnp.maximum(i - 1, 0), 0, 0), memory_space=pltpu.SMEM)
    d0, d1 = dest0.reshape(nt, 1, tm), dest1.reshape(nt, 1, tm)
    grid_spec = pltpu.PrefetchScalarGridSpec(
        num_scalar_prefetch=2,
        grid=(nt,),
        in_specs=[idx, idx, idx_prev, idx_prev, pl.BlockSpec((tm, D), lambda i, pad, nu: (i, 0)),
                  pl.BlockSpec(g_ffn.shape, lambda i, pad, nu: (0, 0))],
        out_specs=pl.BlockSpec(memory_space=pl.ANY),
        scratch_shapes=[pltpu.VMEM((2, tm * ROW_PIECES, LANES), F32), pltpu.VMEM((te * ROW_PIECES, LANES), F32),
                        pltpu.SemaphoreType.DMA((2,))],
    )
    return pl.pallas_call(
        _dispatch_kernel,
        grid_spec=grid_spec,
        out_shape=jax.ShapeDtypeStruct(((xs_rows + te) * ROW_PIECES, LANES), F32),
        compiler_params=_params("arbitrary"),
        name="dispatch",
    )(pad_start, n_used, d0, d1, d0, d1, h2, g_ffn)


def _experts_kernel(te_ref, nu_ref, x_ref, wgu_ref, wdn_ref, y_ref, wgu_bf_ref, wdn_bf_ref):
    j = pl.program_id(0)
    te = x_ref.shape[0] // ROW_PIECES

    @pl.when((j == 0) | (te_ref[j] != te_ref[jnp.maximum(j - 1, 0)]))
    def _():
        wgu_bf_ref[...] = wgu_ref[0].astype(BF16)
        wdn_bf_ref[...] = wdn_ref[0].astype(BF16)

    @pl.when(j < nu_ref[0])
    def _():
        gu = _dot(_from_token_tiles(x_ref, te).astype(BF16), wgu_bf_ref[...])
        gate, up = gu[:, :D_EXPERT], gu[:, D_EXPERT:]
        act = gate / (1.0 + jnp.exp(-gate)) * up
        _to_token_tiles(_dot(act.astype(BF16), wdn_bf_ref[...]), y_ref)

    @pl.when(j >= nu_ref[0])
    def _():
        y_ref[...] = jnp.zeros_like(y_ref)


def _experts(xs, wgu, wdn, layer, tile_expert, n_used, xs_rows, te):
    D = D_MODEL
    tiles = pl.BlockSpec((te * ROW_PIECES, LANES), lambda j, te_, nu: (j, 0))
    grid_spec = pltpu.PrefetchScalarGridSpec(
        num_scalar_prefetch=2,
        grid=(xs_rows // te,),
        in_specs=[tiles,
                  pl.BlockSpec((None, 1, D, 2 * D_EXPERT), lambda j, te_, nu: (layer, te_[j], 0, 0)),
                  pl.BlockSpec((None, 1, D_EXPERT, D), lambda j, te_, nu: (layer, te_[j], 0, 0))],
        out_specs=tiles,
        scratch_shapes=[pltpu.VMEM((D, 2 * D_EXPERT), BF16), pltpu.VMEM((D_EXPERT, D), BF16)],
    )
    return pl.pallas_call(
        _experts_kernel,
        grid_spec=grid_spec,
        out_shape=jax.ShapeDtypeStruct((xs_rows * ROW_PIECES, LANES), F32),
        compiler_params=_params("arbitrary"),
        name="experts",
    )(tile_expert, n_used, xs, wgu, wdn)


def _combine_kernel(d0_ref, d1_ref, d0n_ref, d1n_ref, h_ref, info_ref, gfin_ref, ys_ref, o_ref, y0_ref, y1_ref, sem,
                    *, final_norm):
    tm = h_ref.shape[0]
    i = pl.program_id(0)
    slot = i % 2

    def gathers(da_ref, db_ref, s, go):
        def body(r, _):
            go(_token_copy(ys_ref, da_ref[0, 0, r], y0_ref.at[s], r, sem.at[s]), 0)
            go(_token_copy(ys_ref, db_ref[0, 0, r], y1_ref.at[s], r, sem.at[s]), 1)
            return 0
        lax.fori_loop(0, tm, body, 0, unroll=DMA_UNROLL)

    start = lambda cp, prio: cp.start(priority=prio)
    wait = lambda cp, prio: cp.wait()

    @pl.when(i == 0)
    def _():
        gathers(d0_ref, d1_ref, 0, start)

    @pl.when(i + 1 < pl.num_programs(0))
    def _():
        gathers(d0n_ref, d1n_ref, 1 - slot, start)

    gathers(d0_ref, d1_ref, slot, wait)
    info = info_ref[...]
    out = (h_ref[...] + info[:, INFO_G0:INFO_G0 + 1] * _from_token_tiles(y0_ref.at[slot], tm)
           + info[:, INFO_G1:INFO_G1 + 1] * _from_token_tiles(y1_ref.at[slot], tm))
    if final_norm:
        out = _rmsnorm(out, gfin_ref[...])
    o_ref[...] = out


def _combine(h2, info, g_final, ys, dest0, dest1, tm, final_norm):
    N, D = h2.shape
    nt = N // tm
    idx = pl.BlockSpec((1, 1, tm), lambda i: (i, 0, 0), memory_space=pltpu.SMEM)
    idx_next = pl.BlockSpec((1, 1, tm), lambda i: (jnp.minimum(i + 1, nt - 1), 0, 0), memory_space=pltpu.SMEM)
    d0, d1 = dest0.reshape(nt, 1, tm), dest1.reshape(nt, 1, tm)
    buf = pltpu.VMEM((2, tm * ROW_PIECES, LANES), F32)
    return pl.pallas_call(
        functools.partial(_combine_kernel, final_norm=final_norm),
        grid=(nt,),
        in_specs=[idx, idx, idx_next, idx_next, pl.BlockSpec((tm, D), lambda i: (i, 0)),
                  pl.BlockSpec((tm, LANES), lambda i: (i, 0)), _full(g_final.shape), pl.BlockSpec(memory_space=pl.ANY)],
        out_specs=pl.BlockSpec((tm, D), lambda i: (i, 0)),
        out_shape=jax.ShapeDtypeStruct((N, D), F32),
        scratch_shapes=[buf, buf, pltpu.SemaphoreType.DMA((2,))],
        compiler_params=_params("arbitrary"),
        name="combine",
    )(d0, d1, d0, d1, h2, info, g_final, ys)


IN_SIZES = (NSA_WIDTH,) + (KV_WIDTH,) * 6 + (NSA_HEADS * 3,) + (CONV_WIDTH,) * 3 + (HGRN_WIDTH,) * 4
IN_OFF = tuple(int(v) for v in np.cumsum((0,) + IN_SIZES))


def _rope_tables(T):
    half = HEAD_DIM // 2
    inv = ROPE_THETA ** (-jnp.arange(half, dtype=F32) / half)
    ang = jnp.arange(T).astype(F32)[:, None] * inv[None, :]
    cos, sin = jnp.cos(ang), jnp.sin(ang)
    zero = jnp.zeros_like(sin)
    reps = LANES // HEAD_DIM
    cos_t = jnp.tile(jnp.concatenate([cos, cos], axis=1), (1, reps))
    sin_a = jnp.tile(jnp.concatenate([-sin, zero], axis=1), (1, reps))
    sin_b = jnp.tile(jnp.concatenate([zero, sin], axis=1), (1, reps))
    return cos_t, sin_a, sin_b


def _cmp_to_sel(T):
    nc_pad, ns = T // CMP_STRIDE, T // SEL_BLOCK
    nc = (T - CMP_LEN) // CMP_STRIDE + 1
    cs = np.arange(nc_pad) * CMP_STRIDE
    ss = np.arange(ns) * SEL_BLOCK
    ov = np.clip(np.minimum(cs[:, None] + CMP_LEN, ss[None, :] + SEL_BLOCK)
                 - np.maximum(cs[:, None], ss[None, :]), 0, None) / CMP_LEN
    ov[nc:] = 0.0
    return jnp.asarray(np.pad(ov, ((0, 0), (0, LANES - ns))), dtype=BF16)


def _q_weight(w_in_l):
    wq = w_in_l[:, IN_OFF[0]:IN_OFF[1]].reshape(D_MODEL, NSA_KV_HEADS, NSA_GROUP, 1, HEAD_DIM)
    half = jnp.eye(NSA_KV_HEADS, dtype=wq.dtype)[None, :, None, :, None]
    return (wq * half).reshape(D_MODEL, Q_AUG_WIDTH)


def _compress_weights(pe, w1, w2):
    eye = jnp.eye(NSA_KV_HEADS, dtype=F32)
    half_len = CMP_LEN // 2
    w1r = w1.reshape(2, 2, half_len, HEAD_DIM, CMP_HIDDEN)
    w1_aug = jnp.einsum("sxldm,hg->sxlhdgm", w1r, eye).reshape(2, 2, half_len * KV_WIDTH, NSA_KV_HEADS * CMP_HIDDEN)
    w2_aug = jnp.einsum("smd,hg->shmgd", w2, eye).reshape(2, NSA_KV_HEADS * CMP_HIDDEN, KV_WIDTH)
    pe_r = jnp.broadcast_to(pe.reshape(2, 2, half_len, 1, HEAD_DIM), (2, 2, half_len, NSA_KV_HEADS, HEAD_DIM))
    pe_aug = jnp.pad(pe_r.reshape(2, 2, half_len * KV_WIDTH), ((0, 0), (0, SUBLANES - 2), (0, 0)))
    return pe_aug, w1_aug[:, 0].astype(BF16), w1_aug[:, 1].astype(BF16), w2_aug.astype(BF16)


def _gate_weight(w_in_l):
    wg = w_in_l[:, IN_OFF[7]:IN_OFF[8]].reshape(D_MODEL, NSA_KV_HEADS, NSA_GROUP * 3)
    return jnp.pad(wg, ((0, 0), (0, 0), (0, LANES - NSA_GROUP * 3))).reshape(D_MODEL, NSA_KV_HEADS * LANES)


def _segments(counts, n_tok, te):
    c0 = counts[0, EXPERT_LANE0:EXPERT_LANE0 + N_EXPERTS].astype(jnp.int32)
    c1 = counts[1, EXPERT_LANE0:EXPERT_LANE0 + N_EXPERTS].astype(jnp.int32)
    tiles = (c0 + c1 + te - 1) // te
    tile_end = jnp.cumsum(tiles)
    off = (tile_end - tiles) * te
    tab = jnp.zeros((SUBLANES, LANES), F32)
    tab = tab.at[0, EXPERT_LANE0:EXPERT_LANE0 + N_EXPERTS].set(off.astype(F32))
    tab = tab.at[1, EXPERT_LANE0:EXPERT_LANE0 + N_EXPERTS].set((off + c0).astype(F32))
    n_tiles = (2 * n_tok) // te + N_EXPERTS
    tile_expert = jnp.sum(tile_end[None, :] <= jnp.arange(n_tiles)[:, None], axis=1)
    tile_expert = jnp.minimum(tile_expert, N_EXPERTS - 1).astype(jnp.int32)
    return tab, off + c0 + c1, tile_expert, tile_end[-1:].astype(jnp.int32)


def kernel(x, mem, w_in, conv_w, cmp_pe, cmp_w1, cmp_w2, hgrn_lb_logits, hgrn_norm, w_out, cross_wq, cross_wkv,
           cross_wo, router_group_w, router_group_b, router_expert_w, router_expert_b, expert_w_gate_up,
           expert_w_down, norm_mix, norm_cross, norm_mem, norm_ffn, norm_final):
    B, T, D = x.shape
    depth = w_in.shape[0]
    n_tok = B * T
    tm = min(512, T)
    tm_row = min(256, T)
    te = 512

    cos_t, sin_a, sin_b = _rope_tables(T)
    c2s = _cmp_to_sel(T)
    ltri = jnp.asarray(np.tril(np.ones((tm, tm), np.float32), -1), dtype=BF16)

    p_lb = jax.nn.softmax(hgrn_lb_logits.astype(F32), axis=0)
    lower_bounds = jnp.cumsum(p_lb, axis=0) - p_lb[0:1]
    xs_rows = 2 * n_tok + N_EXPERTS * te

    h = x
    for l in range(depth):
        wl = w_in[l]
        piece = lambda a, b: wl[:, IN_OFF[a]:IN_OFF[b]].astype(BF16)
        cw = jnp.pad(conv_w[l], ((0, SUBLANES - CONV_K), (0, 0)))
        (q, qr, kc, vc, ka, vs, kw, vw, gs, o_conv, hg) = _inproj(
            h, norm_mix[l][None], cos_t, sin_a, sin_b, _q_weight(wl).astype(BF16), piece(1, 7),
            _gate_weight(wl).astype(BF16), piece(8, 11), piece(11, 15), cw, tm)

        chunks = lambda a: a.reshape(B, T // CMP_STRIDE, CMP_STRIDE * KV_WIDTH)
        kvc = _compress(jnp.stack([chunks(kc), chunks(vc)]), *_compress_weights(cmp_pe[l], cmp_w1[l], cmp_w2[l]))
        o_nsa = _nsa(q, qr, kvc, ka, vs, kw, vw, gs, c2s)

        lb = lower_bounds[l][None]
        lb_params = jnp.concatenate([jnp.log(jnp.maximum(lb, LB_FLOOR)), jnp.log1p(-lb), 1.0 - lb,
                                     jnp.zeros((SUBLANES - 3, HGRN_WIDTH), F32)], axis=0)
        o_hgrn = _hgrn(hg, lb_params, hgrn_norm[l][None], tm)

        kmem, vmem = _memkv(mem, norm_mem[l][None], cross_wkv[l].astype(BF16))

        wr = jnp.zeros((D, LANES), F32).at[:, :N_GROUPS].set(router_group_w[l])
        wr = wr.at[:, EXPERT_LANE0:EXPERT_LANE0 + N_EXPERTS].set(router_expert_w[l])
        br = jnp.zeros((1, LANES), F32).at[0, :N_GROUPS].set(router_group_b[l])
        br = br.at[0, EXPERT_LANE0:EXPERT_LANE0 + N_EXPERTS].set(router_expert_b[l])
        wr_hi = wr.astype(BF16)
        wr_lo = (wr - wr_hi.astype(F32)).astype(BF16)
        h2, info, counts = _post(h, o_nsa, o_conv, o_hgrn, w_out[l].astype(BF16), norm_cross[l][None],
                                 cross_wq[l].astype(BF16), kmem, vmem, cross_wo[l].astype(BF16), norm_ffn[l][None],
                                 wr_hi, wr_lo, br, ltri, tm)

        h2 = h2.reshape(n_tok, D)
        info = info.reshape(n_tok, LANES)
        tab, pad_start, tile_expert, n_used = _segments(counts, n_tok, te)
        dest = _route(info, tab, tm)
        dest0, dest1 = dest[:, 0], dest[:, 1]
        xs = _dispatch(h2, norm_ffn[l][None], dest0, dest1, pad_start, n_used, xs_rows, tm_row, te)
        ys = _experts(xs, expert_w_gate_up, expert_w_down, l, tile_expert, n_used, xs_rows, te)
        h = _combine(h2, info, norm_final[None], ys, dest0, dest1, tm_row, l == depth - 1).reshape(B, T, D)
    return h
```

```python
import functools

import numpy as np
import jax
import jax.numpy as jnp
from jax import lax
from jax.experimental import pallas as pl
from jax.experimental.pallas import tpu as pltpu

F32 = jnp.float32
BF16 = jnp.bfloat16

D_MODEL = 1024
HEAD_DIM = 64
NSA_HEADS = 8
NSA_KV_HEADS = 2
NSA_GROUP = NSA_HEADS // NSA_KV_HEADS
NSA_WIDTH = NSA_HEADS * HEAD_DIM
KV_WIDTH = NSA_KV_HEADS * HEAD_DIM
CMP_LEN = 32
CMP_STRIDE = 16
CMP_HIDDEN = 2 * HEAD_DIM
SEL_BLOCK = 64
N_SELECT = 16
WINDOW = 512
Q_BLOCK = 128
NSA_Q_TILE = 256
MASK_VALUE = -1e30
SEL_FORCE = 1e6
CONV_WIDTH = 256
CONV_K = 3
HGRN_HEADS = 4
HGRN_WIDTH = HGRN_HEADS * HEAD_DIM
HGRN_CHUNK = 64
HGRN_SUB = 16
LB_FLOOR = 1e-30
ROPE_THETA = 10000.0
CROSS_HEADS = 4
CROSS_WIDTH = CROSS_HEADS * HEAD_DIM
N_GROUPS = 4
EXPERTS_PER_GROUP = 8
N_EXPERTS = N_GROUPS * EXPERTS_PER_GROUP
D_EXPERT = 512
RMS_EPS = 1e-6
QK_SCALE = HEAD_DIM ** -0.5
LOG2E = 1.4426950408889634
HEAD_SHIFT = HEAD_DIM.bit_length() - 1
SEL_SHIFT = SEL_BLOCK.bit_length() - 1

LANES = 128
SUBLANES = 8
VMEM_LIMIT_BYTES = 48 * 1024 * 1024

Q_AUG_WIDTH = NSA_HEADS * LANES

EXPERT_LANE0 = 32
INFO_E0, INFO_E1, INFO_R0, INFO_R1, INFO_G0, INFO_G1 = 0, 1, 2, 3, 4, 5


def _dot(a, b):
    return jnp.dot(a, b, preferred_element_type=F32)


def _dot_nt(a, b):
    return lax.dot_general(a, b, (((1,), (1,)), ((), ())), preferred_element_type=F32)


def _split_bf16(x):
    hi = x.astype(BF16)
    lo = (x - hi.astype(F32)).astype(BF16)
    return hi, lo


def _rmsnorm(x, g):
    return x * lax.rsqrt(jnp.mean(x * x, axis=-1, keepdims=True) + RMS_EPS) * g


def _params(*sem):
    return pltpu.CompilerParams(dimension_semantics=sem, vmem_limit_bytes=VMEM_LIMIT_BYTES)


def _full(shape):
    nd = len(shape)
    return pl.BlockSpec(shape, lambda *_: (0,) * nd)


def _rope(x, cos, sin_a, sin_b):
    width = x.shape[1]
    reps = width // LANES
    if reps > 1:
        cos = jnp.concatenate([cos] * reps, axis=1)
        sin_a = jnp.concatenate([sin_a] * reps, axis=1)
        sin_b = jnp.concatenate([sin_b] * reps, axis=1)
    half = HEAD_DIM // 2
    return x * cos + pltpu.roll(x, width - half, 1) * sin_a + pltpu.roll(x, half, 1) * sin_b


def _inproj_kernel(h_ref, g_ref, cos_ref, sa_ref, sb_ref, wq_ref, wkv_ref, wg_ref, wc_ref, wh_ref, cw_ref,
                   q_ref, qr_ref, kc_ref, vc_ref, ks_ref, vs_ref, kw_ref, vw_ref, gs_ref, oc_ref, hg_ref,
                   carry_ref):
    tm = h_ref.shape[1]

    @pl.when(pl.program_id(1) == 0)
    def _():
        carry_ref[...] = jnp.zeros_like(carry_ref)

    xb = _rmsnorm(h_ref[0], g_ref[...]).astype(BF16)
    cos, sa, sb = cos_ref[...], sa_ref[...], sb_ref[...]

    q = _dot(xb, wq_ref[...]) * (QK_SCALE * LOG2E)
    q_ref[0] = q.astype(BF16)
    qr_ref[0] = _rope(q, cos, sa, sb).astype(BF16)

    kv = _dot(xb, wkv_ref[...])
    kw_ = KV_WIDTH
    kc_ref[0] = kv[:, 0 * kw_:1 * kw_].astype(BF16)
    vc_ref[0] = kv[:, 1 * kw_:2 * kw_].astype(BF16)
    pos = pl.program_id(1) * tm + lax.broadcasted_iota(jnp.int32, (tm, LANES), 0)
    blk = lax.broadcasted_iota(jnp.int32, (tm, LANES), 1) == lax.shift_right_logical(pos, SEL_SHIFT)
    ks_ref[0, :, 0:kw_] = _rope(kv[:, 2 * kw_:3 * kw_], cos, sa, sb).astype(BF16)
    ks_ref[0, :, kw_:kw_ + LANES] = jnp.where(blk, 1.0, 0.0).astype(BF16)
    vs_ref[0] = kv[:, 3 * kw_:4 * kw_].astype(BF16)
    kw_ref[0] = _rope(kv[:, 4 * kw_:5 * kw_], cos, sa, sb).astype(BF16)
    vw_ref[0] = kv[:, 5 * kw_:6 * kw_].astype(BF16)

    gl = _dot(xb, wg_ref[...])
    gs_ref[0] = 1.0 / (1.0 + jnp.exp(-gl))

    cv = _dot(xb, wc_ref[...])
    cb = cv[:, 0:CONV_WIDTH]
    u = cv[:, CONV_WIDTH:2 * CONV_WIDTH] * cv[:, 2 * CONV_WIDTH:3 * CONV_WIDTH]
    ext = jnp.concatenate([carry_ref[...], u], axis=0)
    u1 = pltpu.roll(ext, 1, 0)[SUBLANES:]
    u2 = pltpu.roll(ext, 2, 0)[SUBLANES:]
    cw = cw_ref[...]
    oc_ref[0] = (cb * (cw[0:1] * u2 + cw[1:2] * u1 + cw[2:3] * u)).astype(BF16)
    carry_ref[...] = u[tm - SUBLANES:tm]

    hg_ref[0] = _dot(xb, wh_ref[...])


def _inproj(h, gain, cos, sin_a, sin_b, wq, wkv, wg, wc, wh, cw, tm):
    B, T, D = h.shape
    nt = T // tm
    tok = lambda w: pl.BlockSpec((1, tm, w), lambda b, t: (b, t, 0))
    tab = pl.BlockSpec((tm, LANES), lambda b, t: (t, 0))
    kvw = KV_WIDTH
    out_w = [(Q_AUG_WIDTH, BF16), (Q_AUG_WIDTH, BF16), (kvw, BF16), (kvw, BF16), (kvw + LANES, BF16), (kvw, BF16),
             (kvw, BF16), (kvw, BF16), (2 * LANES, F32), (CONV_WIDTH, BF16), (4 * HGRN_WIDTH, F32)]
    return pl.pallas_call(
        _inproj_kernel,
        grid=(B, nt),
        in_specs=[tok(D), _full(gain.shape), tab, tab, tab, _full(wq.shape), _full(wkv.shape), _full(wg.shape),
                  _full(wc.shape), _full(wh.shape), _full(cw.shape)],
        out_specs=[tok(w) for w, _ in out_w],
        out_shape=[jax.ShapeDtypeStruct((B, T, w), dt) for w, dt in out_w],
        scratch_shapes=[pltpu.VMEM((SUBLANES, CONV_WIDTH), F32)],
        compiler_params=_params("arbitrary", "arbitrary"),
        name="inproj",
    )(h, gain, cos, sin_a, sin_b, wq, wkv, wg, wc, wh, cw)


def _compress_kernel(x_ref, pe_ref, w1a_ref, w1b_ref, w2_ref, o_ref):
    x = x_ref[0, 0]
    n = x.shape[0]
    w1a, w1b = w1a_ref[0], w1b_ref[0]
    pe_a = jnp.broadcast_to(pe_ref[0, 0:1], (SUBLANES, x.shape[1])).astype(BF16)
    pe_b = jnp.broadcast_to(pe_ref[0, 1:2], (SUBLANES, x.shape[1])).astype(BF16)
    pe_term = (_dot(pe_a, w1a) + _dot(pe_b, w1b))[0:1]
    hid = _dot(x, w1a) + pltpu.roll(_dot(x, w1b), n - 1, 0) + pe_term
    act = hid / (1.0 + jnp.exp(-hid))
    o_ref[0, 0] = _dot(act.astype(BF16), w2_ref[0]).astype(BF16)


def _compress(xkv, pe, w1a, w1b, w2):
    _, B, n, width = xkv.shape
    hid = NSA_KV_HEADS * CMP_HIDDEN
    per_kv = lambda *shape: pl.BlockSpec((1,) + shape, lambda s, b: (s,) + (0,) * len(shape))
    return pl.pallas_call(
        _compress_kernel,
        grid=(2, B),
        in_specs=[pl.BlockSpec((1, 1, n, width), lambda s, b: (s, b, 0, 0)), per_kv(SUBLANES, width),
                  per_kv(width, hid), per_kv(width, hid), per_kv(hid, KV_WIDTH)],
        out_specs=pl.BlockSpec((1, 1, n, KV_WIDTH), lambda s, b: (s, b, 0, 0)),
        out_shape=jax.ShapeDtypeStruct((2, B, n, KV_WIDTH), BF16),
        compiler_params=_params("arbitrary", "arbitrary"),
        name="compress",
    )(xkv, pe, w1a, w1b, w2)


def _masked_softmax2(s, mask):
    s = jnp.where(mask, s, MASK_VALUE)
    m = jnp.max(s, axis=1, keepdims=True)
    e = jnp.exp2(s - m)
    den = jnp.sum(e, axis=1, keepdims=True)
    return e, jnp.where(m > 0.5 * MASK_VALUE, 1.0 / den, 0.0)


def _nsa_kernel(q_ref, qr_ref, kc_ref, vc_ref, ka_ref, vs_ref, kw_ref, vw_ref, gs_ref, c2s_ref, o_ref, *, key_tile):
    i = pl.program_id(1)
    qb, g_, n_h = NSA_Q_TILE, NSA_GROUP, NSA_KV_HEADS
    rows = g_ * qb
    heads = range(n_h)

    def q_groups(ref, h):
        return [ref[0, :, (h * g_ + g) * LANES:(h * g_ + g + 1) * LANES] for g in range(g_)]

    t_row = i * qb + (lax.broadcasted_iota(jnp.int32, (rows, 1), 0) & (qb - 1))
    t_q = i * qb + lax.broadcasted_iota(jnp.int32, (qb, 1), 0)

    kc, vc = kc_ref[0, 0], vc_ref[0, 0]
    c_end = lax.broadcasted_iota(jnp.int32, (1, kc.shape[0]), 1) * CMP_STRIDE + (CMP_LEN - 1)
    o_c, imp = [], []
    for h in heads:
        e_c, inv_c = _masked_softmax2(_dot_nt(jnp.concatenate(q_groups(q_ref, h), axis=0), kc), c_end <= t_row)
        p_c = e_c * inv_c
        o_c.append(_dot(p_c.astype(BF16), vc))
        p_g = p_c[0:qb]
        for g in range(1, g_):
            p_g = p_g + p_c[g * qb:(g + 1) * qb]
        imp.append(_dot(p_g.astype(BF16), c2s_ref[...]))

    wk = WINDOW + qb
    w_start = pl.multiple_of(jnp.maximum(i * qb - WINDOW, 0), qb)
    kwin = kw_ref[0, pl.ds(w_start, wk), :]
    vwin = vw_ref[0, pl.ds(w_start, wk), :]
    dist = t_row - (w_start + lax.broadcasted_iota(jnp.int32, (1, wk), 1))
    w_mask = (dist >= 0) & (dist < WINDOW)
    o_w = []
    for h in heads:
        e_w, inv_w = _masked_softmax2(_dot_nt(jnp.concatenate(q_groups(qr_ref, h), axis=0), kwin), w_mask)
        o_w.append(_dot(e_w.astype(BF16), vwin) * inv_w)

    cur = lax.shift_right_logical(t_q, SEL_SHIFT)
    j = lax.broadcasted_iota(jnp.int32, (qb, LANES), 1)
    jf = j.astype(F32)
    forced = (j == 0) | (j == cur) | (j == cur - 1)
    q_aug = []
    for h in heads:
        work = jnp.where(forced | (j > cur), -jnp.inf, imp[h])
        sel = jnp.where(forced, 1.0, 0.0)
        for _ in range(N_SELECT - 3):
            mx = jnp.max(work, axis=1, keepdims=True)
            first = jnp.min(jnp.where(work == mx, jf, float(LANES)), axis=1, keepdims=True)
            pick = jf == first
            sel = jnp.where(pick, 1.0, sel)
            work = jnp.where(pick, -jnp.inf, work)
        sel_bias = jnp.where(sel > 0, 0.0, MASK_VALUE).astype(BF16)
        q_aug.append(jnp.concatenate([jnp.concatenate([qg, sel_bias], axis=1) for qg in q_groups(qr_ref, h)],
                                     axis=0))

    def sel_tile(tile, width, carry, causal):
        start = pl.multiple_of(tile * width, width)
        ka = ka_ref[0, pl.ds(start, width), :]
        v = vs_ref[0, pl.ds(start, width), :]
        out = []
        for h in heads:
            m, l, acc = carry[h]
            s = _dot_nt(q_aug[h], ka)
            if causal:
                s = jnp.where(start + lax.broadcasted_iota(jnp.int32, (1, width), 1) <= t_row, s, MASK_VALUE)
            m_new = jnp.maximum(m, jnp.max(s, axis=1, keepdims=True))
            alpha = jnp.exp2(m - m_new)
            p = jnp.exp2(s - m_new)
            l = alpha * l + jnp.sum(p, axis=1, keepdims=True)
            acc = alpha * acc + _dot(p.astype(BF16), v)
            out.append((m_new, l, acc))
        return tuple(out)

    n_full = (i * qb) // key_tile
    init = tuple((jnp.full((rows, 1), MASK_VALUE, F32), jnp.zeros((rows, 1), F32), jnp.zeros((rows, LANES), F32))
                 for _ in heads)
    carry = lax.fori_loop(0, n_full, lambda kt, c: sel_tile(kt, key_tile, c, False), init)
    carry = sel_tile(n_full, key_tile, carry, True)
    o_s = [acc * (1.0 / l) for _, l, acc in carry]

    gs = gs_ref[0]
    low_half = lax.broadcasted_iota(jnp.int32, (qb, LANES), 1) < HEAD_DIM
    out_groups = []
    for h in heads:
        mixed = []
        for g in range(g_):
            r = slice(g * qb, (g + 1) * qb)
            c = h * LANES + 3 * g
            mixed.append(gs[:, c:c + 1] * o_c[h][r] + gs[:, c + 1:c + 2] * o_s[h][r] + gs[:, c + 2:c + 3] * o_w[h][r])
        for pair in range(g_ // 2):
            a, b = mixed[2 * pair], mixed[2 * pair + 1]
            if h == 0:
                out_groups.append(jnp.where(low_half, a, pltpu.roll(b, HEAD_DIM, 1)))
            else:
                out_groups.append(jnp.where(low_half, pltpu.roll(a, HEAD_DIM, 1), b))
    o_ref[0] = jnp.concatenate(out_groups, axis=1).astype(o_ref.dtype)


def _nsa(q, qr, kvc, ka, vs, kw, vw, gs, c2s):
    B, T, _ = q.shape
    assert T // SEL_BLOCK <= LANES
    nc = kvc.shape[2]
    key_tile = min(1024, T)
    blk = lambda w: pl.BlockSpec((1, NSA_Q_TILE, w), lambda b, i: (b, i, 0))
    seq = lambda w: pl.BlockSpec((1, T, w), lambda b, i: (b, 0, 0))
    cmp_spec = lambda s: pl.BlockSpec((1, 1, nc, KV_WIDTH), lambda b, i: (s, b, 0, 0))
    return pl.pallas_call(
        functools.partial(_nsa_kernel, key_tile=key_tile),
        grid=(B, T // NSA_Q_TILE),
        in_specs=[blk(Q_AUG_WIDTH), blk(Q_AUG_WIDTH), cmp_spec(0), cmp_spec(1), seq(KV_WIDTH + LANES), seq(KV_WIDTH),
                  seq(KV_WIDTH), seq(KV_WIDTH), blk(2 * LANES), _full(c2s.shape)],
        out_specs=blk(NSA_WIDTH),
        out_shape=jax.ShapeDtypeStruct((B, T, NSA_WIDTH), BF16),
        compiler_params=_params("arbitrary", "arbitrary"),
        name="nsa",
    )(q, qr, kvc, kvc, ka, vs, kw, vw, gs, c2s)


def _hgrn_kernel(x_ref, lbp_ref, ng_ref, o_ref, st_ref):
    w = HGRN_WIDTH
    L = HGRN_CHUNK
    n_chunks = x_ref.shape[1] // L

    @pl.when(pl.program_id(1) == 0)
    def _():
        st_ref[...] = jnp.zeros_like(st_ref)

    log_lb, log_1m_lb, one_m_lb = lbp_ref[0:1], lbp_ref[1:2], lbp_ref[2:3]
    ng = ng_ref[...]
    ri = lax.broadcasted_iota(jnp.int32, (L, L), 0)
    ci = lax.broadcasted_iota(jnp.int32, (L, L), 1)
    tril = jnp.where(ci <= ri, 1.0, 0.0).astype(BF16)
    hr = lax.shift_right_logical(lax.broadcasted_iota(jnp.int32, (w, w), 0), HEAD_SHIFT)
    hc = lax.shift_right_logical(lax.broadcasted_iota(jnp.int32, (w, w), 1), HEAD_SHIFT)
    same_head = hr == hc
    head_ones = jnp.where(same_head, 1.0, 0.0).astype(BF16)
    sub_row = lax.broadcasted_iota(jnp.int32, (L, 1), 0) & (HGRN_SUB - 1)
    lane_head = lax.shift_right_logical(lax.broadcasted_iota(jnp.int32, (1, w), 1), HEAD_SHIFT)

    def chunk(c, _):
        r0 = pl.multiple_of(c * L, L)
        q = x_ref[0, pl.ds(r0, L), 0 * w:1 * w] * QK_SCALE
        z = x_ref[0, pl.ds(r0, L), 1 * w:2 * w]
        v = x_ref[0, pl.ds(r0, L), 2 * w:3 * w]
        g = x_ref[0, pl.ds(r0, L), 3 * w:4 * w]

        log_sig = jnp.minimum(z, 0.0) - jnp.log(1.0 + jnp.exp(-jnp.abs(z)))
        bb = log_1m_lb + log_sig
        log_f = jnp.maximum(log_lb, bb) + jnp.log(1.0 + jnp.exp(-jnp.abs(log_lb - bb)))
        k = one_m_lb / (1.0 + jnp.exp(z))

        lf_hi, lf_lo = _split_bf16(log_f)
        b = _dot(tril, lf_hi) + _dot(tril, lf_lo)
        b_last = b[L - 1:L]
        st = st_ref[...]
        o = _dot_nt((q * jnp.exp(b)).astype(BF16), st.astype(BF16))
        for d in range(HGRN_SUB):
            if d == 0:
                wgt = q * k
                vs = v
            else:
                ok = sub_row >= d
                diff = jnp.where(ok, b - pltpu.roll(b, d, 0), 0.0)
                wgt = jnp.where(ok, q * pltpu.roll(k, d, 0) * jnp.exp(diff), 0.0)
                vs = pltpu.roll(v, d, 0)
            o = o + _dot(wgt.astype(BF16), head_ones) * vs
        parts = [jnp.zeros((HGRN_SUB, w), F32)]
        for lo in range(HGRN_SUB, L, HGRN_SUB):
            r = b[lo - 1:lo]
            qs = q[lo:lo + HGRN_SUB] * jnp.exp(b[lo:lo + HGRN_SUB] - r)
            ks = (k[0:lo] * jnp.exp(r - b[0:lo])).astype(BF16)
            acc = jnp.zeros((HGRN_SUB, w), F32)
            for hd in range(HGRN_HEADS):
                in_head = lane_head == hd
                a = _dot_nt(jnp.where(in_head, qs, 0.0).astype(BF16), ks)
                acc = acc + _dot(a.astype(BF16), jnp.where(in_head, v[0:lo], 0.0).astype(BF16))
            parts.append(acc)
        o = o + jnp.concatenate(parts, axis=0)
        kd = k * jnp.exp(b_last - b)
        upd = _dot(v.T.astype(BF16), kd.astype(BF16))
        st_ref[...] = st * jnp.exp(b_last) + jnp.where(same_head, upd, 0.0)

        o2_hi, o2_lo = _split_bf16(o * o)
        ms = (_dot(o2_hi, head_ones) + _dot(o2_lo, head_ones)) * (1.0 / HEAD_DIM)
        y = o * lax.rsqrt(ms + RMS_EPS) * ng
        o_ref[0, pl.ds(r0, L), :] = (y * (g / (1.0 + jnp.exp(-g)))).astype(o_ref.dtype)
        return 0

    lax.fori_loop(0, n_chunks, chunk, 0, unroll=4)


def _hgrn(hg, lb_params, norm_g, rows):
    B, T, width = hg.shape
    return pl.pallas_call(
        _hgrn_kernel,
        grid=(B, T // rows),
        in_specs=[pl.BlockSpec((1, rows, width), lambda b, t: (b, t, 0)), _full(lb_params.shape), _full(norm_g.shape)],
        out_specs=pl.BlockSpec((1, rows, HGRN_WIDTH), lambda b, t: (b, t, 0)),
        out_shape=jax.ShapeDtypeStruct((B, T, HGRN_WIDTH), BF16),
        scratch_shapes=[pltpu.VMEM((HGRN_WIDTH, HGRN_WIDTH), F32)],
        compiler_params=_params("arbitrary", "arbitrary"),
        name="hgrn",
    )(hg, lb_params, norm_g)


def _memkv_kernel(m_ref, g_ref, w_ref, k_ref, v_ref):
    kv = _dot(_rmsnorm(m_ref[0], g_ref[...]).astype(BF16), w_ref[...])
    k_ref[0] = kv[:, :CROSS_WIDTH].astype(BF16)
    v_ref[0] = kv[:, CROSS_WIDTH:].astype(BF16)


def _memkv(mem, gain, wkv):
    B, M, D = mem.shape
    out = pl.BlockSpec((1, M, CROSS_WIDTH), lambda b: (b, 0, 0))
    return pl.pallas_call(
        _memkv_kernel,
        grid=(B,),
        in_specs=[pl.BlockSpec((1, M, D), lambda b: (b, 0, 0)), _full(gain.shape), _full(wkv.shape)],
        out_specs=[out, out],
        out_shape=[jax.ShapeDtypeStruct((B, M, CROSS_WIDTH), BF16)] * 2,
        compiler_params=_params("arbitrary"),
        name="memkv",
    )(mem, gain, wkv)


def _post_kernel(h_ref, on_ref, oc_ref, oh_ref, wo_ref, gc_ref, wq_ref, km_ref, vm_ref, wco_ref, gf_ref,
                 wr_hi_ref, wr_lo_ref, br_ref, ltri_ref, h2_ref, info_ref, cnt_ref, carry_ref):
    tm = h_ref.shape[1]
    first = (pl.program_id(0) == 0) & (pl.program_id(1) == 0)

    @pl.when(first)
    def _():
        carry_ref[...] = jnp.zeros_like(carry_ref)

    a, b = NSA_WIDTH, NSA_WIDTH + CONV_WIDTH
    mix = _dot(on_ref[0], wo_ref[0:a]) + _dot(oc_ref[0], wo_ref[a:b]) + _dot(oh_ref[0], wo_ref[b:])
    h1 = h_ref[0] + mix

    q = _dot(_rmsnorm(h1, gc_ref[...]).astype(BF16), wq_ref[...]) * QK_SCALE
    km, vm = km_ref[0], vm_ref[0]
    lane_head = lax.shift_right_logical(lax.broadcasted_iota(jnp.int32, (1, CROSS_WIDTH), 1), HEAD_SHIFT)
    o = jnp.zeros((tm, CROSS_WIDTH), F32)
    for hd in range(CROSS_HEADS):
        in_head = lane_head == hd
        s = _dot_nt(jnp.where(in_head, q, 0.0).astype(BF16), km)
        m = jnp.max(s, axis=1, keepdims=True)
        e = jnp.exp(s - m)
        p = e / jnp.sum(e, axis=1, keepdims=True)
        o = o + _dot(p.astype(BF16), jnp.where(in_head, vm, jnp.zeros_like(vm)))
    h2 = h1 + _dot(o.astype(BF16), wco_ref[...])
    h2_ref[0] = h2

    xn = _rmsnorm(h2, gf_ref[...])
    x_hi, x_lo = _split_bf16(xn)
    hi_parts = _dot(x_hi, wr_lo_ref[...])
    logits = hi_parts[:, :LANES] + hi_parts[:, LANES:] + _dot(x_lo, wr_hi_ref[...]) + br_ref[...]
    lane = lax.broadcasted_iota(jnp.int32, (tm, LANES), 1)
    lanef = lane.astype(F32)
    big = float(LANES)

    lg = jnp.where(lane < N_GROUPS, logits, -jnp.inf)
    mg = jnp.max(lg, axis=1, keepdims=True)
    p_grp_sel = 1.0 / jnp.sum(jnp.exp(lg - mg), axis=1, keepdims=True)
    grp = jnp.min(jnp.where(lg == mg, lanef, big), axis=1, keepdims=True)

    lo_lane = EXPERT_LANE0 + EXPERTS_PER_GROUP * grp
    in_grp = (lanef >= lo_lane) & (lanef < lo_lane + EXPERTS_PER_GROUP)
    le = jnp.where(in_grp, logits, -jnp.inf)
    me = jnp.max(le, axis=1, keepdims=True)
    ee = jnp.exp(le - me)
    pe = jnp.where(in_grp, ee / jnp.sum(ee, axis=1, keepdims=True), -1.0)
    p1 = jnp.max(pe, axis=1, keepdims=True)
    l1 = jnp.min(jnp.where(pe == p1, lanef, big), axis=1, keepdims=True)
    pe2 = jnp.where(lanef == l1, -1.0, pe)
    p2 = jnp.max(pe2, axis=1, keepdims=True)
    l2 = jnp.min(jnp.where(pe2 == p2, lanef, big), axis=1, keepdims=True)
    scale = p_grp_sel / (p1 + p2)
    g0, g1 = p1 * scale, p2 * scale

    oh0 = jnp.where(lanef == l1, 1.0, 0.0)
    oh1 = jnp.where(lanef == l2, 1.0, 0.0)
    ltri = ltri_ref[...]
    c0, c1 = carry_ref[0:1], carry_ref[1:2]
    r0 = jnp.sum((_dot(ltri, oh0.astype(BF16)) + c0) * oh0, axis=1, keepdims=True)
    r1 = jnp.sum((_dot(ltri, oh1.astype(BF16)) + c1) * oh1, axis=1, keepdims=True)
    c0 = c0 + jnp.sum(oh0, axis=0, keepdims=True)
    c1 = c1 + jnp.sum(oh1, axis=0, keepdims=True)
    carry_ref[0:1] = c0
    carry_ref[1:2] = c1
    cnt_ref[...] = carry_ref[...]

    info = jnp.zeros((tm, LANES), F32)
    for col, val in ((INFO_E0, l1 - EXPERT_LANE0), (INFO_E1, l2 - EXPERT_LANE0), (INFO_R0, r0), (INFO_R1, r1),
                     (INFO_G0, g0), (INFO_G1, g1)):
        info = jnp.where(lane == col, val, info)
    info_ref[0] = info


def _post(h, o_nsa, o_conv, o_hgrn, w_out, g_cross, wq, kmem, vmem, wco, g_ffn, wr_hi, wr_lo, br, ltri, tm):
    B, T, D = h.shape
    M = kmem.shape[1]
    tok = lambda w: pl.BlockSpec((1, tm, w), lambda b, t: (b, t, 0))
    memspec = pl.BlockSpec((1, M, CROSS_WIDTH), lambda b, t: (b, 0, 0))
    return pl.pallas_call(
        _post_kernel,
        grid=(B, T // tm),
        in_specs=[tok(D), tok(NSA_WIDTH), tok(CONV_WIDTH), tok(HGRN_WIDTH), _full(w_out.shape), _full(g_cross.shape),
                  _full(wq.shape), memspec, memspec, _full(wco.shape), _full(g_ffn.shape), _full(wr_hi.shape),
                  _full(wr_lo.shape), _full(br.shape), _full(ltri.shape)],
        out_specs=[tok(D), tok(LANES), _full((SUBLANES, LANES))],
        out_shape=[jax.ShapeDtypeStruct((B, T, D), F32), jax.ShapeDtypeStruct((B, T, LANES), F32),
                   jax.ShapeDtypeStruct((SUBLANES, LANES), F32)],
        scratch_shapes=[pltpu.VMEM((SUBLANES, LANES), F32)],
        compiler_params=_params("arbitrary", "arbitrary"),
        name="post",
    )(h, o_nsa, o_conv, o_hgrn, w_out, g_cross, wq, kmem, vmem, wco, g_ffn, wr_hi, wr_lo, br, ltri)


ROW_PIECES = D_MODEL // LANES
assert ROW_PIECES == SUBLANES


def _to_token_tiles(x, ref):
    tm = x.shape[0]
    for s in range(ROW_PIECES):
        ref[pl.ds(s, tm, stride=ROW_PIECES), :] = x[:, s * LANES:(s + 1) * LANES]


def _from_token_tiles(ref, tm):
    return jnp.concatenate([ref[pl.ds(s, tm, stride=ROW_PIECES), :] for s in range(ROW_PIECES)], axis=1)


def _token_copy(src, src_tok, dst, dst_tok, sem):
    rows = lambda t: pl.ds(pl.multiple_of(t * ROW_PIECES, ROW_PIECES), ROW_PIECES)
    return pltpu.make_async_copy(src.at[rows(src_tok)], dst.at[rows(dst_tok)], sem)


DMA_UNROLL = 8


def _route_kernel(info_ref, tab_ref, dest_ref):
    info = info_ref[...]
    lanef = lax.broadcasted_iota(jnp.int32, info.shape, 1).astype(F32)
    pick = lambda col, row: jnp.sum(jnp.where(lanef == info[:, col:col + 1] + EXPERT_LANE0, tab_ref[row:row + 1], 0.0),
                                    axis=1, keepdims=True)
    d0 = pick(INFO_E0, 0) + info[:, INFO_R0:INFO_R0 + 1]
    d1 = pick(INFO_E1, 1) + info[:, INFO_R1:INFO_R1 + 1]
    dest_ref[...] = jnp.where(lanef == 0.0, d0, jnp.where(lanef == 1.0, d1, 0.0)).astype(jnp.int32)


def _route(info, tab, tm):
    N = info.shape[0]
    blk = pl.BlockSpec((tm, LANES), lambda i: (i, 0))
    return pl.pallas_call(
        _route_kernel,
        grid=(N // tm,),
        in_specs=[blk, _full(tab.shape)],
        out_specs=blk,
        out_shape=jax.ShapeDtypeStruct((N, LANES), jnp.int32),
        compiler_params=_params("arbitrary"),
        name="route",
    )(info, tab)


def _dispatch_kernel(pad_ref, nu_ref, d0_ref, d1_ref, d0p_ref, d1p_ref, h_ref, g_ref, xs_ref, xn_ref, zero_ref, sem):
    tm = h_ref.shape[0]
    fill_rows = zero_ref.shape[0]

    @pl.when(pl.program_id(0) == 0)
    def _():
        zero_ref[...] = jnp.zeros_like(zero_ref)
        fill = lambda row: pltpu.make_async_copy(
            zero_ref, xs_ref.at[pl.ds(pl.multiple_of(row * ROW_PIECES, ROW_PIECES), fill_rows)], sem.at[0])
        for e in range(N_EXPERTS):
            fill(pad_ref[e]).start()
        for e in range(N_EXPERTS):
            fill(pad_ref[e]).wait()
        n_total = xs_ref.shape[0] // fill_rows
        te = fill_rows // ROW_PIECES

        def tail_start(j, _):
            fill(j * te).start()
            return 0

        def tail_wait(j, _):
            fill(j * te).wait()
            return 0

        lax.fori_loop(nu_ref[0], n_total, tail_start, 0)
        lax.fori_loop(nu_ref[0], n_total, tail_wait, 0)

    i = pl.program_id(0)
    slot = i % 2
    _to_token_tiles(_rmsnorm(h_ref[...], g_ref[...]), xn_ref.at[slot])

    def scatters(da_ref, db_ref, s, go):
        def body(r, _):
            go(_token_copy(xn_ref.at[s], r, xs_ref, da_ref[0, 0, r], sem.at[s]), 0)
            go(_token_copy(xn_ref.at[s], r, xs_ref, db_ref[0, 0, r], sem.at[s]), 1)
            return 0
        lax.fori_loop(0, tm, body, 0, unroll=DMA_UNROLL)

    start = lambda cp, prio: cp.start(priority=prio)
    wait = lambda cp, prio: cp.wait()
    scatters(d0_ref, d1_ref, slot, start)

    @pl.when(i > 0)
    def _():
        scatters(d0p_ref, d1p_ref, 1 - slot, wait)

    @pl.when(i == pl.num_programs(0) - 1)
    def _():
        scatters(d0_ref, d1_ref, slot, wait)


def _dispatch(h2, g_ffn, dest0, dest1, pad_start, n_used, xs_rows, tm, te):
    N, D = h2.shape
    nt = N // tm
    idx = pl.BlockSpec((1, 1, tm), lambda i, pad, nu: (i, 0, 0), memory_space=pltpu.SMEM)
    idx_prev = pl.BlockSpec((1, 1, tm), lambda i, pad, nu: (jnp.maximum(i - 1, 0), 0, 0), memory_space=pltpu.SMEM)
    d0, d1 = dest0.reshape(nt, 1, tm), dest1.reshape(nt, 1, tm)
    grid_spec = pltpu.PrefetchScalarGridSpec(
        num_scalar_prefetch=2,
        grid=(nt,),
        in_specs=[idx, idx, idx_prev, idx_prev, pl.BlockSpec((tm, D), lambda i, pad, nu: (i, 0)),
                  pl.BlockSpec(g_ffn.shape, lambda i, pad, nu: (0, 0))],
        out_specs=pl.BlockSpec(memory_space=pl.ANY),
        scratch_shapes=[pltpu.VMEM((2, tm * ROW_PIECES, LANES), F32), pltpu.VMEM((te * ROW_PIECES, LANES), F32),
                        pltpu.SemaphoreType.DMA((2,))],
    )
    return pl.pallas_call(
        _dispatch_kernel,
        grid_spec=grid_spec,
        out_shape=jax.ShapeDtypeStruct(((xs_rows + te) * ROW_PIECES, LANES), F32),
        compiler_params=_params("arbitrary"),
        name="dispatch",
    )(pad_start, n_used, d0, d1, d0, d1, h2, g_ffn)


def _experts_kernel(te_ref, nu_ref, x_ref, wgu_ref, wdn_ref, y_ref, wgu_bf_ref, wdn_bf_ref):
    j = pl.program_id(0)
    te = x_ref.shape[0] // ROW_PIECES

    @pl.when((j == 0) | (te_ref[j] != te_ref[jnp.maximum(j - 1, 0)]))
    def _():
        wgu_bf_ref[...] = wgu_ref[0].astype(BF16)
        wdn_bf_ref[...] = wdn_ref[0].astype(BF16)

    @pl.when(j < nu_ref[0])
    def _():
        gu = _dot(_from_token_tiles(x_ref, te).astype(BF16), wgu_bf_ref[...])
        gate, up = gu[:, :D_EXPERT], gu[:, D_EXPERT:]
        act = gate / (1.0 + jnp.exp(-gate)) * up
        _to_token_tiles(_dot(act.astype(BF16), wdn_bf_ref[...]), y_ref)

    @pl.when(j >= nu_ref[0])
    def _():
        y_ref[...] = jnp.zeros_like(y_ref)


def _experts(xs, wgu, wdn, layer, tile_expert, n_used, xs_rows, te):
    D = D_MODEL
    tiles = pl.BlockSpec((te * ROW_PIECES, LANES), lambda j, te_, nu: (j, 0))
    grid_spec = pltpu.PrefetchScalarGridSpec(
        num_scalar_prefetch=2,
        grid=(xs_rows // te,),
        in_specs=[tiles,
                  pl.BlockSpec((None, 1, D, 2 * D_EXPERT), lambda j, te_, nu: (layer, te_[j], 0, 0)),
                  pl.BlockSpec((None, 1, D_EXPERT, D), lambda j, te_, nu: (layer, te_[j], 0, 0))],
        out_specs=tiles,
        scratch_shapes=[pltpu.VMEM((D, 2 * D_EXPERT), BF16), pltpu.VMEM((D_EXPERT, D), BF16)],
    )
    return pl.pallas_call(
        _experts_kernel,
        grid_spec=grid_spec,
        out_shape=jax.ShapeDtypeStruct((xs_rows * ROW_PIECES, LANES), F32),
        compiler_params=_params("arbitrary"),
        name="experts",
    )(tile_expert, n_used, xs, wgu, wdn)


def _combine_kernel(d0_ref, d1_ref, d0n_ref, d1n_ref, h_ref, info_ref, gfin_ref, ys_ref, o_ref, y0_ref, y1_ref, sem,
                    *, final_norm):
    tm = h_ref.shape[0]
    i = pl.program_id(0)
    slot = i % 2

    def gathers(da_ref, db_ref, s, go):
        def body(r, _):
            go(_token_copy(ys_ref, da_ref[0, 0, r], y0_ref.at[s], r, sem.at[s]), 0)
            go(_token_copy(ys_ref, db_ref[0, 0, r], y1_ref.at[s], r, sem.at[s]), 1)
            return 0
        lax.fori_loop(0, tm, body, 0, unroll=DMA_UNROLL)

    start = lambda cp, prio: cp.start(priority=prio)
    wait = lambda cp, prio: cp.wait()

    @pl.when(i == 0)
    def _():
        gathers(d0_ref, d1_ref, 0, start)

    @pl.when(i + 1 < pl.num_programs(0))
    def _():
        gathers(d0n_ref, d1n_ref, 1 - slot, start)

    gathers(d0_ref, d1_ref, slot, wait)
    info = info_ref[...]
    out = (h_ref[...] + info[:, INFO_G0:INFO_G0 + 1] * _from_token_tiles(y0_ref.at[slot], tm)
           + info[:, INFO_G1:INFO_G1 + 1] * _from_token_tiles(y1_ref.at[slot], tm))
    if final_norm:
        out = _rmsnorm(out, gfin_ref[...])
    o_ref[...] = out


def _combine(h2, info, g_final, ys, dest0, dest1, tm, final_norm):
    N, D = h2.shape
    nt = N // tm
    idx = pl.BlockSpec((1, 1, tm), lambda i: (i, 0, 0), memory_space=pltpu.SMEM)
    idx_next = pl.BlockSpec((1, 1, tm), lambda i: (jnp.minimum(i + 1, nt - 1), 0, 0), memory_space=pltpu.SMEM)
    d0, d1 = dest0.reshape(nt, 1, tm), dest1.reshape(nt, 1, tm)
    buf = pltpu.VMEM((2, tm * ROW_PIECES, LANES), F32)
    return pl.pallas_call(
        functools.partial(_combine_kernel, final_norm=final_norm),
        grid=(nt,),
        in_specs=[idx, idx, idx_next, idx_next, pl.BlockSpec((tm, D), lambda i: (i, 0)),
                  pl.BlockSpec((tm, LANES), lambda i: (i, 0)), _full(g_final.shape), pl.BlockSpec(memory_space=pl.ANY)],
        out_specs=pl.BlockSpec((tm, D), lambda i: (i, 0)),
        out_shape=jax.ShapeDtypeStruct((N, D), F32),
        scratch_shapes=[buf, buf, pltpu.SemaphoreType.DMA((2,))],
        compiler_params=_params("arbitrary"),
        name="combine",
    )(d0, d1, d0, d1, h2, info, g_final, ys)


IN_SIZES = (NSA_WIDTH,) + (KV_WIDTH,) * 6 + (NSA_HEADS * 3,) + (CONV_WIDTH,) * 3 + (HGRN_WIDTH,) * 4
IN_OFF = tuple(int(v) for v in np.cumsum((0,) + IN_SIZES))


def _rope_tables(T):
    half = HEAD_DIM // 2
    inv = ROPE_THETA ** (-jnp.arange(half, dtype=F32) / half)
    ang = jnp.arange(T).astype(F32)[:, None] * inv[None, :]
    cos, sin = jnp.cos(ang), jnp.sin(ang)
    zero = jnp.zeros_like(sin)
    reps = LANES // HEAD_DIM
    cos_t = jnp.tile(jnp.concatenate([cos, cos], axis=1), (1, reps))
    sin_a = jnp.tile(jnp.concatenate([-sin, zero], axis=1), (1, reps))
    sin_b = jnp.tile(jnp.concatenate([zero, sin], axis=1), (1, reps))
    return cos_t, sin_a, sin_b


def _cmp_to_sel(T):
    nc_pad, ns = T // CMP_STRIDE, T // SEL_BLOCK
    nc = (T - CMP_LEN) // CMP_STRIDE + 1
    cs = np.arange(nc_pad) * CMP_STRIDE
    ss = np.arange(ns) * SEL_BLOCK
    ov = np.clip(np.minimum(cs[:, None] + CMP_LEN, ss[None, :] + SEL_BLOCK)
                 - np.maximum(cs[:, None], ss[None, :]), 0, None) / CMP_LEN
    ov[nc:] = 0.0
    return jnp.asarray(np.pad(ov, ((0, 0), (0, LANES - ns))), dtype=BF16)


def _q_weight(w_in_l):
    wq = w_in_l[:, IN_OFF[0]:IN_OFF[1]].reshape(D_MODEL, NSA_KV_HEADS, NSA_GROUP, 1, HEAD_DIM)
    half = jnp.eye(NSA_KV_HEADS, dtype=wq.dtype)[None, :, None, :, None]
    return (wq * half).reshape(D_MODEL, Q_AUG_WIDTH)


def _compress_weights(pe, w1, w2):
    eye = jnp.eye(NSA_KV_HEADS, dtype=F32)
    half_len = CMP_LEN // 2
    w1r = w1.reshape(2, 2, half_len, HEAD_DIM, CMP_HIDDEN)
    w1_aug = jnp.einsum("sxldm,hg->sxlhdgm", w1r, eye).reshape(2, 2, half_len * KV_WIDTH, NSA_KV_HEADS * CMP_HIDDEN)
    w2_aug = jnp.einsum("smd,hg->shmgd", w2, eye).reshape(2, NSA_KV_HEADS * CMP_HIDDEN, KV_WIDTH)
    pe_r = jnp.broadcast_to(pe.reshape(2, 2, half_len, 1, HEAD_DIM), (2, 2, half_len, NSA_KV_HEADS, HEAD_DIM))
    pe_aug = jnp.pad(pe_r.reshape(2, 2, half_len * KV_WIDTH), ((0, 0), (0, SUBLANES - 2), (0, 0)))
    return pe_aug, w1_aug[:, 0].astype(BF16), w1_aug[:, 1].astype(BF16), w2_aug.astype(BF16)


def _gate_weight(w_in_l):
    wg = w_in_l[:, IN_OFF[7]:IN_OFF[8]].reshape(D_MODEL, NSA_KV_HEADS, NSA_GROUP * 3)
    return jnp.pad(wg, ((0, 0), (0, 0), (0, LANES - NSA_GROUP * 3))).reshape(D_MODEL, NSA_KV_HEADS * LANES)


def _segments(counts, n_tok, te):
    c0 = counts[0, EXPERT_LANE0:EXPERT_LANE0 + N_EXPERTS].astype(jnp.int32)
    c1 = counts[1, EXPERT_LANE0:EXPERT_LANE0 + N_EXPERTS].astype(jnp.int32)
    tiles = (c0 + c1 + te - 1) // te
    tile_end = jnp.cumsum(tiles)
    off = (tile_end - tiles) * te
    tab = jnp.zeros((SUBLANES, LANES), F32)
    tab = tab.at[0, EXPERT_LANE0:EXPERT_LANE0 + N_EXPERTS].set(off.astype(F32))
    tab = tab.at[1, EXPERT_LANE0:EXPERT_LANE0 + N_EXPERTS].set((off + c0).astype(F32))
    n_tiles = (2 * n_tok) // te + N_EXPERTS
    tile_expert = jnp.sum(tile_end[None, :] <= jnp.arange(n_tiles)[:, None], axis=1)
    tile_expert = jnp.minimum(tile_expert, N_EXPERTS - 1).astype(jnp.int32)
    return tab, off + c0 + c1, tile_expert, tile_end[-1:].astype(jnp.int32)


def kernel(x, mem, w_in, conv_w, cmp_pe, cmp_w1, cmp_w2, hgrn_lb_logits, hgrn_norm, w_out, cross_wq, cross_wkv,
           cross_wo, router_group_w, router_group_b, router_expert_w, router_expert_b, expert_w_gate_up,
           expert_w_down, norm_mix, norm_cross, norm_mem, norm_ffn, norm_final):
    B, T, D = x.shape
    depth = w_in.shape[0]
    n_tok = B * T
    tm = min(512, T)
    tm_row = min(256, T)
    te = 512

    cos_t, sin_a, sin_b = _rope_tables(T)
    c2s = _cmp_to_sel(T)
    ltri = jnp.asarray(np.tril(np.ones((tm, tm), np.float32), -1), dtype=BF16)

    p_lb = jax.nn.softmax(hgrn_lb_logits.astype(F32), axis=0)
    lower_bounds = jnp.cumsum(p_lb, axis=0) - p_lb[0:1]
    xs_rows = 2 * n_tok + N_EXPERTS * te

    h = x
    for l in range(depth):
        wl = w_in[l]
        piece = lambda a, b: wl[:, IN_OFF[a]:IN_OFF[b]].astype(BF16)
        cw = jnp.pad(conv_w[l], ((0, SUBLANES - CONV_K), (0, 0)))
        (q, qr, kc, vc, ka, vs, kw, vw, gs, o_conv, hg) = _inproj(
            h, norm_mix[l][None], cos_t, sin_a, sin_b, _q_weight(wl).astype(BF16), piece(1, 7),
            _gate_weight(wl).astype(BF16), piece(8, 11), piece(11, 15), cw, tm)

        chunks = lambda a: a.reshape(B, T // CMP_STRIDE, CMP_STRIDE * KV_WIDTH)
        kvc = _compress(jnp.stack([chunks(kc), chunks(vc)]), *_compress_weights(cmp_pe[l], cmp_w1[l], cmp_w2[l]))
        o_nsa = _nsa(q, qr, kvc, ka, vs, kw, vw, gs, c2s)

        lb = lower_bounds[l][None]
        lb_params = jnp.concatenate([jnp.log(jnp.maximum(lb, LB_FLOOR)), jnp.log1p(-lb), 1.0 - lb,
                                     jnp.zeros((SUBLANES - 3, HGRN_WIDTH), F32)], axis=0)
        o_hgrn = _hgrn(hg, lb_params, hgrn_norm[l][None], tm)

        kmem, vmem = _memkv(mem, norm_mem[l][None], cross_wkv[l].astype(BF16))

        wr = jnp.zeros((D, LANES), F32).at[:, :N_GROUPS].set(router_group_w[l])
        wr = wr.at[:, EXPERT_LANE0:EXPERT_LANE0 + N_EXPERTS].set(router_expert_w[l])
        br = jnp.zeros((1, LANES), F32).at[0, :N_GROUPS].set(router_group_b[l])
        br = br.at[0, EXPERT_LANE0:EXPERT_LANE0 + N_EXPERTS].set(router_expert_b[l])
        wr_hi = wr.astype(BF16)
        wr_lo = jnp.concatenate([wr_hi, (wr - wr_hi.astype(F32)).astype(BF16)], axis=1)
        h2, info, counts = _post(h, o_nsa, o_conv, o_hgrn, w_out[l].astype(BF16), norm_cross[l][None],
                                 cross_wq[l].astype(BF16), kmem, vmem, cross_wo[l].astype(BF16), norm_ffn[l][None],
                                 wr_hi, wr_lo, br, ltri, tm)

        h2 = h2.reshape(n_tok, D)
        info = info.reshape(n_tok, LANES)
        tab, pad_start, tile_expert, n_used = _segments(counts, n_tok, te)
        dest = _route(info, tab, tm)
        dest0, dest1 = dest[:, 0], dest[:, 1]
        xs = _dispatch(h2, norm_ffn[l][None], dest0, dest1, pad_start, n_used, xs_rows, tm_row, te)
        ys = _experts(xs, expert_w_gate_up, expert_w_down, l, tile_expert, n_used, xs_rows, te)
        h = _combine(h2, info, norm_final[None], ys, dest0, dest1, tm_row, l == depth - 1).reshape(B, T, D)
    return h
```

```python
import functools

import numpy as np
import jax
import jax.numpy as jnp
from jax import lax
from jax.experimental import pallas as pl
from jax.experimental.pallas import tpu as pltpu

F32 = jnp.float32
BF16 = jnp.bfloat16

D_MODEL = 1024
HEAD_DIM = 64
NSA_HEADS = 8
NSA_KV_HEADS = 2
NSA_GROUP = NSA_HEADS // NSA_KV_HEADS
NSA_WIDTH = NSA_HEADS * HEAD_DIM
KV_WIDTH = NSA_KV_HEADS * HEAD_DIM
CMP_LEN = 32
CMP_STRIDE = 16
CMP_HIDDEN = 2 * HEAD_DIM
SEL_BLOCK = 64
N_SELECT = 16
WINDOW = 512
Q_BLOCK = 128
NSA_Q_TILE = 256
MASK_VALUE = -1e30
SEL_FORCE = 1e6
CONV_WIDTH = 256
CONV_K = 3
HGRN_HEADS = 4
HGRN_WIDTH = HGRN_HEADS * HEAD_DIM
HGRN_CHUNK = 64
HGRN_SUB = 16
LB_FLOOR = 1e-30
ROPE_THETA = 10000.0
CROSS_HEADS = 4
CROSS_WIDTH = CROSS_HEADS * HEAD_DIM
N_GROUPS = 4
EXPERTS_PER_GROUP = 8
N_EXPERTS = N_GROUPS * EXPERTS_PER_GROUP
D_EXPERT = 512
RMS_EPS = 1e-6
QK_SCALE = HEAD_DIM ** -0.5
LOG2E = 1.4426950408889634
HEAD_SHIFT = HEAD_DIM.bit_length() - 1
SEL_SHIFT = SEL_BLOCK.bit_length() - 1

LANES = 128
SUBLANES = 8
VMEM_LIMIT_BYTES = 48 * 1024 * 1024

Q_AUG_WIDTH = NSA_HEADS * LANES

EXPERT_LANE0 = 32
INFO_E0, INFO_E1, INFO_R0, INFO_R1, INFO_G0, INFO_G1 = 0, 1, 2, 3, 4, 5


def _dot(a, b):
    return jnp.dot(a, b, preferred_element_type=F32)


def _dot_nt(a, b):
    return lax.dot_general(a, b, (((1,), (1,)), ((), ())), preferred_element_type=F32)


def _split_bf16(x):
    hi = x.astype(BF16)
    lo = (x - hi.astype(F32)).astype(BF16)
    return hi, lo


def _rmsnorm(x, g):
    return x * lax.rsqrt(jnp.mean(x * x, axis=-1, keepdims=True) + RMS_EPS) * g


def _params(*sem):
    return pltpu.CompilerParams(dimension_semantics=sem, vmem_limit_bytes=VMEM_LIMIT_BYTES)


def _full(shape):
    nd = len(shape)
    return pl.BlockSpec(shape, lambda *_: (0,) * nd)


def _rope(x, cos, sin_a, sin_b):
    width = x.shape[1]
    reps = width // LANES
    if reps > 1:
        cos = jnp.concatenate([cos] * reps, axis=1)
        sin_a = jnp.concatenate([sin_a] * reps, axis=1)
        sin_b = jnp.concatenate([sin_b] * reps, axis=1)
    half = HEAD_DIM // 2
    return x * cos + pltpu.roll(x, width - half, 1) * sin_a + pltpu.roll(x, half, 1) * sin_b


def _inproj_kernel(h_ref, g_ref, cos_ref, sa_ref, sb_ref, wq_ref, wkv_ref, wg_ref, wc_ref, wh_ref, cw_ref,
                   q_ref, qr_ref, kc_ref, vc_ref, ks_ref, vs_ref, kw_ref, vw_ref, gs_ref, oc_ref, hg_ref,
                   carry_ref):
    tm = h_ref.shape[1]

    @pl.when(pl.program_id(1) == 0)
    def _():
        carry_ref[...] = jnp.zeros_like(carry_ref)

    xb = _rmsnorm(h_ref[0], g_ref[...]).astype(BF16)
    cos, sa, sb = cos_ref[...], sa_ref[...], sb_ref[...]

    q = _dot(xb, wq_ref[...]) * (QK_SCALE * LOG2E)
    q_ref[0] = q.astype(BF16)
    qr_ref[0] = _rope(q, cos, sa, sb).astype(BF16)

    kv = _dot(xb, wkv_ref[...])
    kw_ = KV_WIDTH
    kc_ref[0] = kv[:, 0 * kw_:1 * kw_].astype(BF16)
    vc_ref[0] = kv[:, 1 * kw_:2 * kw_].astype(BF16)
    pos = pl.program_id(1) * tm + lax.broadcasted_iota(jnp.int32, (tm, LANES), 0)
    blk = lax.broadcasted_iota(jnp.int32, (tm, LANES), 1) == lax.shift_right_logical(pos, SEL_SHIFT)
    ks_ref[0, :, 0:kw_] = _rope(kv[:, 2 * kw_:3 * kw_], cos, sa, sb).astype(BF16)
    ks_ref[0, :, kw_:kw_ + LANES] = jnp.where(blk, 1.0, 0.0).astype(BF16)
    vs_ref[0] = kv[:, 3 * kw_:4 * kw_].astype(BF16)
    kw_ref[0] = _rope(kv[:, 4 * kw_:5 * kw_], cos, sa, sb).astype(BF16)
    vw_ref[0] = kv[:, 5 * kw_:6 * kw_].astype(BF16)

    gl = _dot(xb, wg_ref[...])
    gs_ref[0] = 1.0 / (1.0 + jnp.exp(-gl))

    cv = _dot(xb, wc_ref[...])
    cb = cv[:, 0:CONV_WIDTH]
    u = cv[:, CONV_WIDTH:2 * CONV_WIDTH] * cv[:, 2 * CONV_WIDTH:3 * CONV_WIDTH]
    ext = jnp.concatenate([carry_ref[...], u], axis=0)
    u1 = pltpu.roll(ext, 1, 0)[SUBLANES:]
    u2 = pltpu.roll(ext, 2, 0)[SUBLANES:]
    cw = cw_ref[...]
    oc_ref[0] = (cb * (cw[0:1] * u2 + cw[1:2] * u1 + cw[2:3] * u)).astype(BF16)
    carry_ref[...] = u[tm - SUBLANES:tm]

    hg_ref[0] = _dot(xb, wh_ref[...])


def _inproj(h, gain, cos, sin_a, sin_b, wq, wkv, wg, wc, wh, cw, tm):
    B, T, D = h.shape
    nt = T // tm
    tok = lambda w: pl.BlockSpec((1, tm, w), lambda b, t: (b, t, 0))
    tab = pl.BlockSpec((tm, LANES), lambda b, t: (t, 0))
    kvw = KV_WIDTH
    out_w = [(Q_AUG_WIDTH, BF16), (Q_AUG_WIDTH, BF16), (kvw, BF16), (kvw, BF16), (kvw + LANES, BF16), (kvw, BF16),
             (kvw, BF16), (kvw, BF16), (2 * LANES, F32), (CONV_WIDTH, BF16), (4 * HGRN_WIDTH, F32)]
    return pl.pallas_call(
        _inproj_kernel,
        grid=(B, nt),
        in_specs=[tok(D), _full(gain.shape), tab, tab, tab, _full(wq.shape), _full(wkv.shape), _full(wg.shape),
                  _full(wc.shape), _full(wh.shape), _full(cw.shape)],
        out_specs=[tok(w) for w, _ in out_w],
        out_shape=[jax.ShapeDtypeStruct((B, T, w), dt) for w, dt in out_w],
        scratch_shapes=[pltpu.VMEM((SUBLANES, CONV_WIDTH), F32)],
        compiler_params=_params("arbitrary", "arbitrary"),
        name="inproj",
    )(h, gain, cos, sin_a, sin_b, wq, wkv, wg, wc, wh, cw)


def _compress_kernel(x_ref, pe_ref, w1a_ref, w1b_ref, w2_ref, o_ref):
    x = x_ref[0, 0]
    n = x.shape[0]
    w1a, w1b = w1a_ref[0], w1b_ref[0]
    pe_a = jnp.broadcast_to(pe_ref[0, 0:1], (SUBLANES, x.shape[1])).astype(BF16)
    pe_b = jnp.broadcast_to(pe_ref[0, 1:2], (SUBLANES, x.shape[1])).astype(BF16)
    pe_term = (_dot(pe_a, w1a) + _dot(pe_b, w1b))[0:1]
    hid = _dot(x, w1a) + pltpu.roll(_dot(x, w1b), n - 1, 0) + pe_term
    act = hid / (1.0 + jnp.exp(-hid))
    o_ref[0, 0] = _dot(act.astype(BF16), w2_ref[0]).astype(BF16)


def _compress(xkv, pe, w1a, w1b, w2):
    _, B, n, width = xkv.shape
    hid = NSA_KV_HEADS * CMP_HIDDEN
    per_kv = lambda *shape: pl.BlockSpec((1,) + shape, lambda s, b: (s,) + (0,) * len(shape))
    return pl.pallas_call(
        _compress_kernel,
        grid=(2, B),
        in_specs=[pl.BlockSpec((1, 1, n, width), lambda s, b: (s, b, 0, 0)), per_kv(SUBLANES, width),
                  per_kv(width, hid), per_kv(width, hid), per_kv(hid, KV_WIDTH)],
        out_specs=pl.BlockSpec((1, 1, n, KV_WIDTH), lambda s, b: (s, b, 0, 0)),
        out_shape=jax.ShapeDtypeStruct((2, B, n, KV_WIDTH), BF16),
        compiler_params=_params("arbitrary", "arbitrary"),
        name="compress",
    )(xkv, pe, w1a, w1b, w2)


def _masked_softmax2(s, mask):
    s = jnp.where(mask, s, MASK_VALUE)
    m = jnp.max(s, axis=1, keepdims=True)
    e = jnp.exp2(s - m)
    den = jnp.sum(e, axis=1, keepdims=True)
    return e, jnp.where(m > 0.5 * MASK_VALUE, 1.0 / den, 0.0)


def _nsa_kernel(q_ref, qr_ref, kc_ref, vc_ref, ka_ref, vs_ref, kw_ref, vw_ref, gs_ref, c2s_ref, o_ref, *, key_tile):
    i = pl.program_id(1)
    qb, g_, n_h = NSA_Q_TILE, NSA_GROUP, NSA_KV_HEADS
    rows = g_ * qb
    heads = range(n_h)

    def q_groups(ref, h):
        return [ref[0, :, (h * g_ + g) * LANES:(h * g_ + g + 1) * LANES] for g in range(g_)]

    t_row = i * qb + (lax.broadcasted_iota(jnp.int32, (rows, 1), 0) & (qb - 1))
    t_q = i * qb + lax.broadcasted_iota(jnp.int32, (qb, 1), 0)

    kc, vc = kc_ref[0, 0], vc_ref[0, 0]
    c_end = lax.broadcasted_iota(jnp.int32, (1, kc.shape[0]), 1) * CMP_STRIDE + (CMP_LEN - 1)
    o_c, imp = [], []
    for h in heads:
        e_c, inv_c = _masked_softmax2(_dot_nt(jnp.concatenate(q_groups(q_ref, h), axis=0), kc), c_end <= t_row)
        p_c = e_c * inv_c
        o_c.append(_dot(p_c.astype(BF16), vc))
        p_g = p_c[0:qb]
        for g in range(1, g_):
            p_g = p_g + p_c[g * qb:(g + 1) * qb]
        imp.append(_dot(p_g.astype(BF16), c2s_ref[...]))

    wk = WINDOW + qb
    w_start = pl.multiple_of(jnp.maximum(i * qb - WINDOW, 0), qb)
    kwin = kw_ref[0, pl.ds(w_start, wk), :]
    vwin = vw_ref[0, pl.ds(w_start, wk), :]
    dist = t_row - (w_start + lax.broadcasted_iota(jnp.int32, (1, wk), 1))
    w_mask = (dist >= 0) & (dist < WINDOW)
    o_w = []
    for h in heads:
        e_w, inv_w = _masked_softmax2(_dot_nt(jnp.concatenate(q_groups(qr_ref, h), axis=0), kwin), w_mask)
        o_w.append(_dot(e_w.astype(BF16), vwin) * inv_w)

    cur = lax.shift_right_logical(t_q, SEL_SHIFT)
    j = lax.broadcasted_iota(jnp.int32, (qb, LANES), 1)
    jf = j.astype(F32)
    forced = (j == 0) | (j == cur) | (j == cur - 1)
    q_aug = []
    for h in heads:
        work = jnp.where(forced | (j > cur), -jnp.inf, imp[h])
        sel = jnp.where(forced, 1.0, 0.0)
        for _ in range(N_SELECT - 3):
            mx = jnp.max(work, axis=1, keepdims=True)
            first = jnp.min(jnp.where(work == mx, jf, float(LANES)), axis=1, keepdims=True)
            pick = jf == first
            sel = jnp.where(pick, 1.0, sel)
            work = jnp.where(pick, -jnp.inf, work)
        sel_bias = jnp.where(sel > 0, 0.0, MASK_VALUE).astype(BF16)
        q_aug.append(jnp.concatenate([jnp.concatenate([qg, sel_bias], axis=1) for qg in q_groups(qr_ref, h)],
                                     axis=0))

    def sel_tile(tile, width, carry, causal):
        start = pl.multiple_of(tile * width, width)
        ka = ka_ref[0, pl.ds(start, width), :]
        v = vs_ref[0, pl.ds(start, width), :]
        out = []
        for h in heads:
            m, l, acc = carry[h]
            s = _dot_nt(q_aug[h], ka)
            if causal:
                s = jnp.where(start + lax.broadcasted_iota(jnp.int32, (1, width), 1) <= t_row, s, MASK_VALUE)
            m_new = jnp.maximum(m, jnp.max(s, axis=1, keepdims=True))
            alpha = jnp.exp2(m - m_new)
            p = jnp.exp2(s - m_new)
            l = alpha * l + jnp.sum(p, axis=1, keepdims=True)
            acc = alpha * acc + _dot(p.astype(BF16), v)
            out.append((m_new, l, acc))
        return tuple(out)

    n_full = (i * qb) // key_tile
    init = tuple((jnp.full((rows, 1), MASK_VALUE, F32), jnp.zeros((rows, 1), F32), jnp.zeros((rows, LANES), F32))
                 for _ in heads)
    carry = lax.fori_loop(0, n_full, lambda kt, c: sel_tile(kt, key_tile, c, False), init)
    carry = sel_tile(n_full, key_tile, carry, True)
    o_s = [acc * (1.0 / l) for _, l, acc in carry]

    gs = gs_ref[0]
    low_half = lax.broadcasted_iota(jnp.int32, (qb, LANES), 1) < HEAD_DIM
    out_groups = []
    for h in heads:
        mixed = []
        for g in range(g_):
            r = slice(g * qb, (g + 1) * qb)
            c = h * LANES + 3 * g
            mixed.append(gs[:, c:c + 1] * o_c[h][r] + gs[:, c + 1:c + 2] * o_s[h][r] + gs[:, c + 2:c + 3] * o_w[h][r])
        for pair in range(g_ // 2):
            a, b = mixed[2 * pair], mixed[2 * pair + 1]
            if h == 0:
                out_groups.append(jnp.where(low_half, a, pltpu.roll(b, HEAD_DIM, 1)))
            else:
                out_groups.append(jnp.where(low_half, pltpu.roll(a, HEAD_DIM, 1), b))
    o_ref[0] = jnp.concatenate(out_groups, axis=1).astype(o_ref.dtype)


def _nsa(q, qr, kvc, ka, vs, kw, vw, gs, c2s):
    B, T, _ = q.shape
    assert T // SEL_BLOCK <= LANES
    nc = kvc.shape[2]
    key_tile = min(1024, T)
    blk = lambda w: pl.BlockSpec((1, NSA_Q_TILE, w), lambda b, i: (b, i, 0))
    seq = lambda w: pl.BlockSpec((1, T, w), lambda b, i: (b, 0, 0))
    cmp_spec = lambda s: pl.BlockSpec((1, 1, nc, KV_WIDTH), lambda b, i: (s, b, 0, 0))
    return pl.pallas_call(
        functools.partial(_nsa_kernel, key_tile=key_tile),
        grid=(B, T // NSA_Q_TILE),
        in_specs=[blk(Q_AUG_WIDTH), blk(Q_AUG_WIDTH), cmp_spec(0), cmp_spec(1), seq(KV_WIDTH + LANES), seq(KV_WIDTH),
                  seq(KV_WIDTH), seq(KV_WIDTH), blk(2 * LANES), _full(c2s.shape)],
        out_specs=blk(NSA_WIDTH),
        out_shape=jax.ShapeDtypeStruct((B, T, NSA_WIDTH), BF16),
        compiler_params=_params("arbitrary", "arbitrary"),
        name="nsa",
    )(q, qr, kvc, kvc, ka, vs, kw, vw, gs, c2s)


def _hgrn_kernel(x_ref, lbp_ref, ng_ref, o_ref, st_ref):
    w = HGRN_WIDTH
    L = HGRN_CHUNK
    sub = HGRN_SUB
    n_chunks = x_ref.shape[1] // L
    n_seq = x_ref.shape[0]
    rows = n_seq * L
    seqs = range(n_seq)

    @pl.when(pl.program_id(1) == 0)
    def _():
        st_ref[...] = jnp.zeros_like(st_ref)

    log_lb, log_1m_lb, one_m_lb = lbp_ref[0:1], lbp_ref[1:2], lbp_ref[2:3]
    ng = ng_ref[...]
    ri = lax.broadcasted_iota(jnp.int32, (rows, rows), 0)
    ci = lax.broadcasted_iota(jnp.int32, (rows, rows), 1)
    chunk_shift = L.bit_length() - 1
    same_chunk = lax.shift_right_logical(ri, chunk_shift) == lax.shift_right_logical(ci, chunk_shift)
    tril = jnp.where((ci <= ri) & same_chunk, 1.0, 0.0).astype(BF16)
    hr = lax.shift_right_logical(lax.broadcasted_iota(jnp.int32, (w, w), 0), HEAD_SHIFT)
    hc = lax.shift_right_logical(lax.broadcasted_iota(jnp.int32, (w, w), 1), HEAD_SHIFT)
    same_head = hr == hc
    head_ones = jnp.where(same_head, 1.0, 0.0).astype(BF16)
    sub_row = lax.broadcasted_iota(jnp.int32, (rows, 1), 0) & (sub - 1)
    lane_head = lax.shift_right_logical(lax.broadcasted_iota(jnp.int32, (1, w), 1), HEAD_SHIFT)
    stack = lambda pieces: jnp.concatenate(pieces, axis=0)

    def seq_index(idx, per_seq):
        out = jnp.zeros(idx.shape, jnp.int32)
        for s in range(1, n_seq):
            out = out + jnp.where(idx >= s * per_seq, 1, 0)
        return out

    def chunk(c, _):
        r0 = pl.multiple_of(c * L, L)
        load = lambda col: x_ref[:, pl.ds(r0, L), col * w:(col + 1) * w].reshape(rows, w)
        q = load(0) * QK_SCALE
        z, v, g = load(1), load(2), load(3)

        log_sig = jnp.minimum(z, 0.0) - jnp.log(1.0 + jnp.exp(-jnp.abs(z)))
        bb = log_1m_lb + log_sig
        log_f = jnp.maximum(log_lb, bb) + jnp.log(1.0 + jnp.exp(-jnp.abs(log_lb - bb)))
        k = one_m_lb / (1.0 + jnp.exp(z))

        lf_hi, lf_lo = _split_bf16(log_f)
        b = _dot(tril, lf_hi) + _dot(tril, lf_lo)
        seq_rows = lambda a, s, lo, n: a[s * L + lo:s * L + lo + n]
        q_dec = (q * jnp.exp(b)).astype(BF16)
        o = stack([_dot_nt(seq_rows(q_dec, s, 0, L), st_ref[s].astype(BF16)) for s in seqs])
        for d in range(sub):
            if d == 0:
                wgt = q * k
                vs = v
            else:
                wgt = jnp.where(sub_row >= d, q * pltpu.roll(k, d, 0) * jnp.exp(b - pltpu.roll(b, d, 0)), 0.0)
                vs = pltpu.roll(v, d, 0)
            o = o + _dot(wgt.astype(BF16), head_ones) * vs
        parts = [[jnp.zeros((sub, w), F32)] for _ in seqs]
        for lo in range(sub, L, sub):
            qs = stack([seq_rows(q, s, lo, sub) * jnp.exp(seq_rows(b, s, lo, sub) - seq_rows(b, s, lo - 1, 1))
                        for s in seqs])
            ks = stack([seq_rows(k, s, 0, lo) * jnp.exp(seq_rows(b, s, lo - 1, 1) - seq_rows(b, s, 0, lo))
                        for s in seqs]).astype(BF16)
            vv = stack([seq_rows(v, s, 0, lo) for s in seqs])
            same_seq = (seq_index(lax.broadcasted_iota(jnp.int32, (n_seq * sub, n_seq * lo), 0), sub)
                        == seq_index(lax.broadcasted_iota(jnp.int32, (n_seq * sub, n_seq * lo), 1), lo))
            acc = jnp.zeros((n_seq * sub, w), F32)
            for hd in range(HGRN_HEADS):
                in_head = lane_head == hd
                a = jnp.where(same_seq, _dot_nt(jnp.where(in_head, qs, 0.0).astype(BF16), ks), 0.0)
                acc = acc + _dot(a.astype(BF16), jnp.where(in_head, vv, 0.0).astype(BF16))
            for s in seqs:
                parts[s].append(acc[s * sub:(s + 1) * sub])
        o = o + stack([piece for s in seqs for piece in parts[s]])
        for s in seqs:
            b_last = seq_rows(b, s, L - 1, 1)
            kd = seq_rows(k, s, 0, L) * jnp.exp(b_last - seq_rows(b, s, 0, L))
            upd = _dot(seq_rows(v, s, 0, L).T.astype(BF16), kd.astype(BF16))
            st_ref[s] = st_ref[s] * jnp.exp(b_last) + jnp.where(same_head, upd, 0.0)

        o2_hi, o2_lo = _split_bf16(o * o)
        ms = (_dot(o2_hi, head_ones) + _dot(o2_lo, head_ones)) * (1.0 / HEAD_DIM)
        y = o * lax.rsqrt(ms + RMS_EPS) * ng
        o_ref[:, pl.ds(r0, L), :] = (y * (g / (1.0 + jnp.exp(-g)))).astype(o_ref.dtype).reshape(n_seq, L, w)
        return 0

    lax.fori_loop(0, n_chunks, chunk, 0, unroll=2)


def _hgrn(hg, lb_params, norm_g, rows):
    B, T, width = hg.shape
    nb = 4 if B % 4 == 0 else (2 if B % 2 == 0 else 1)
    return pl.pallas_call(
        _hgrn_kernel,
        grid=(B // nb, T // rows),
        in_specs=[pl.BlockSpec((nb, rows, width), lambda b, t: (b, t, 0)), _full(lb_params.shape), _full(norm_g.shape)],
        out_specs=pl.BlockSpec((nb, rows, HGRN_WIDTH), lambda b, t: (b, t, 0)),
        out_shape=jax.ShapeDtypeStruct((B, T, HGRN_WIDTH), BF16),
        scratch_shapes=[pltpu.VMEM((nb, HGRN_WIDTH, HGRN_WIDTH), F32)],
        compiler_params=_params("arbitrary", "arbitrary"),
        name="hgrn",
    )(hg, lb_params, norm_g)


def _memkv_kernel(m_ref, g_ref, w_ref, k_ref, v_ref):
    kv = _dot(_rmsnorm(m_ref[0], g_ref[...]).astype(BF16), w_ref[...])
    k_ref[0] = kv[:, :CROSS_WIDTH].astype(BF16)
    v_ref[0] = kv[:, CROSS_WIDTH:].astype(BF16)


def _memkv(mem, gain, wkv):
    B, M, D = mem.shape
    out = pl.BlockSpec((1, M, CROSS_WIDTH), lambda b: (b, 0, 0))
    return pl.pallas_call(
        _memkv_kernel,
        grid=(B,),
        in_specs=[pl.BlockSpec((1, M, D), lambda b: (b, 0, 0)), _full(gain.shape), _full(wkv.shape)],
        out_specs=[out, out],
        out_shape=[jax.ShapeDtypeStruct((B, M, CROSS_WIDTH), BF16)] * 2,
        compiler_params=_params("arbitrary"),
        name="memkv",
    )(mem, gain, wkv)


def _post_kernel(h_ref, on_ref, oc_ref, oh_ref, wo_ref, gc_ref, wq_ref, km_ref, vm_ref, wco_ref, gf_ref,
                 wr_hi_ref, wr_lo_ref, br_ref, ltri_ref, h2_ref, info_ref, cnt_ref, carry_ref):
    tm = h_ref.shape[1]
    first = (pl.program_id(0) == 0) & (pl.program_id(1) == 0)

    @pl.when(first)
    def _():
        carry_ref[...] = jnp.zeros_like(carry_ref)

    a, b = NSA_WIDTH, NSA_WIDTH + CONV_WIDTH
    mix = _dot(on_ref[0], wo_ref[0:a]) + _dot(oc_ref[0], wo_ref[a:b]) + _dot(oh_ref[0], wo_ref[b:])
    h1 = h_ref[0] + mix

    q = _dot(_rmsnorm(h1, gc_ref[...]).astype(BF16), wq_ref[...]) * QK_SCALE
    km, vm = km_ref[0], vm_ref[0]
    lane_head = lax.shift_right_logical(lax.broadcasted_iota(jnp.int32, (1, CROSS_WIDTH), 1), HEAD_SHIFT)
    o = jnp.zeros((tm, CROSS_WIDTH), F32)
    for hd in range(CROSS_HEADS):
        in_head = lane_head == hd
        s = _dot_nt(jnp.where(in_head, q, 0.0).astype(BF16), km)
        m = jnp.max(s, axis=1, keepdims=True)
        e = jnp.exp(s - m)
        p = e / jnp.sum(e, axis=1, keepdims=True)
        o = o + _dot(p.astype(BF16), jnp.where(in_head, vm, jnp.zeros_like(vm)))
    h2 = h1 + _dot(o.astype(BF16), wco_ref[...])
    h2_ref[0] = h2

    xn = _rmsnorm(h2, gf_ref[...])
    x_hi, x_lo = _split_bf16(xn)
    hi_parts = _dot(x_hi, wr_lo_ref[...])
    logits = hi_parts[:, :LANES] + hi_parts[:, LANES:] + _dot(x_lo, wr_hi_ref[...]) + br_ref[...]
    lane = lax.broadcasted_iota(jnp.int32, (tm, LANES), 1)
    lanef = lane.astype(F32)
    big = float(LANES)

    lg = jnp.where(lane < N_GROUPS, logits, -jnp.inf)
    mg = jnp.max(lg, axis=1, keepdims=True)
    p_grp_sel = 1.0 / jnp.sum(jnp.exp(lg - mg), axis=1, keepdims=True)
    grp = jnp.min(jnp.where(lg == mg, lanef, big), axis=1, keepdims=True)

    lo_lane = EXPERT_LANE0 + EXPERTS_PER_GROUP * grp
    in_grp = (lanef >= lo_lane) & (lanef < lo_lane + EXPERTS_PER_GROUP)
    le = jnp.where(in_grp, logits, -jnp.inf)
    me = jnp.max(le, axis=1, keepdims=True)
    ee = jnp.exp(le - me)
    pe = jnp.where(in_grp, ee / jnp.sum(ee, axis=1, keepdims=True), -1.0)
    p1 = jnp.max(pe, axis=1, keepdims=True)
    l1 = jnp.min(jnp.where(pe == p1, lanef, big), axis=1, keepdims=True)
    pe2 = jnp.where(lanef == l1, -1.0, pe)
    p2 = jnp.max(pe2, axis=1, keepdims=True)
    l2 = jnp.min(jnp.where(pe2 == p2, lanef, big), axis=1, keepdims=True)
    scale = p_grp_sel / (p1 + p2)
    g0, g1 = p1 * scale, p2 * scale

    oh0 = jnp.where(lanef == l1, 1.0, 0.0)
    oh1 = jnp.where(lanef == l2, 1.0, 0.0)
    ltri = ltri_ref[...]
    c0, c1 = carry_ref[0:1], carry_ref[1:2]
    r0 = jnp.sum((_dot(ltri, oh0.astype(BF16)) + c0) * oh0, axis=1, keepdims=True)
    r1 = jnp.sum((_dot(ltri, oh1.astype(BF16)) + c1) * oh1, axis=1, keepdims=True)
    c0 = c0 + jnp.sum(oh0, axis=0, keepdims=True)
    c1 = c1 + jnp.sum(oh1, axis=0, keepdims=True)
    carry_ref[0:1] = c0
    carry_ref[1:2] = c1
    cnt_ref[...] = carry_ref[...]

    info = jnp.zeros((tm, LANES), F32)
    for col, val in ((INFO_E0, l1 - EXPERT_LANE0), (INFO_E1, l2 - EXPERT_LANE0), (INFO_R0, r0), (INFO_R1, r1),
                     (INFO_G0, g0), (INFO_G1, g1)):
        info = jnp.where(lane == col, val, info)
    info_ref[0] = info


def _post(h, o_nsa, o_conv, o_hgrn, w_out, g_cross, wq, kmem, vmem, wco, g_ffn, wr_hi, wr_lo, br, ltri, tm):
    B, T, D = h.shape
    M = kmem.shape[1]
    tok = lambda w: pl.BlockSpec((1, tm, w), lambda b, t: (b, t, 0))
    memspec = pl.BlockSpec((1, M, CROSS_WIDTH), lambda b, t: (b, 0, 0))
    return pl.pallas_call(
        _post_kernel,
        grid=(B, T // tm),
        in_specs=[tok(D), tok(NSA_WIDTH), tok(CONV_WIDTH), tok(HGRN_WIDTH), _full(w_out.shape), _full(g_cross.shape),
                  _full(wq.shape), memspec, memspec, _full(wco.shape), _full(g_ffn.shape), _full(wr_hi.shape),
                  _full(wr_lo.shape), _full(br.shape), _full(ltri.shape)],
        out_specs=[tok(D), tok(LANES), _full((SUBLANES, LANES))],
        out_shape=[jax.ShapeDtypeStruct((B, T, D), F32), jax.ShapeDtypeStruct((B, T, LANES), F32),
                   jax.ShapeDtypeStruct((SUBLANES, LANES), F32)],
        scratch_shapes=[pltpu.VMEM((SUBLANES, LANES), F32)],
        compiler_params=_params("arbitrary", "arbitrary"),
        name="post",
    )(h, o_nsa, o_conv, o_hgrn, w_out, g_cross, wq, kmem, vmem, wco, g_ffn, wr_hi, wr_lo, br, ltri)


ROW_PIECES = D_MODEL // LANES
assert ROW_PIECES == SUBLANES


def _to_token_tiles(x, ref):
    tm = x.shape[0]
    for s in range(ROW_PIECES):
        ref[pl.ds(s, tm, stride=ROW_PIECES), :] = x[:, s * LANES:(s + 1) * LANES]


def _from_token_tiles(ref, tm):
    return jnp.concatenate([ref[pl.ds(s, tm, stride=ROW_PIECES), :] for s in range(ROW_PIECES)], axis=1)


def _token_copy(src, src_tok, dst, dst_tok, sem):
    rows = lambda t: pl.ds(pl.multiple_of(t * ROW_PIECES, ROW_PIECES), ROW_PIECES)
    return pltpu.make_async_copy(src.at[rows(src_tok)], dst.at[rows(dst_tok)], sem)


DMA_UNROLL = 8


def _route_kernel(info_ref, tab_ref, dest_ref):
    info = info_ref[...]
    lanef = lax.broadcasted_iota(jnp.int32, info.shape, 1).astype(F32)
    pick = lambda col, row: jnp.sum(jnp.where(lanef == info[:, col:col + 1] + EXPERT_LANE0, tab_ref[row:row + 1], 0.0),
                                    axis=1, keepdims=True)
    d0 = pick(INFO_E0, 0) + info[:, INFO_R0:INFO_R0 + 1]
    d1 = pick(INFO_E1, 1) + info[:, INFO_R1:INFO_R1 + 1]
    dest_ref[...] = jnp.where(lanef == 0.0, d0, jnp.where(lanef == 1.0, d1, 0.0)).astype(jnp.int32)


def _route(info, tab, tm):
    N = info.shape[0]
    blk = pl.BlockSpec((tm, LANES), lambda i: (i, 0))
    return pl.pallas_call(
        _route_kernel,
        grid=(N // tm,),
        in_specs=[blk, _full(tab.shape)],
        out_specs=blk,
        out_shape=jax.ShapeDtypeStruct((N, LANES), jnp.int32),
        compiler_params=_params("arbitrary"),
        name="route",
    )(info, tab)


def _dispatch_kernel(pad_ref, nu_ref, d0_ref, d1_ref, d0p_ref, d1p_ref, h_ref, g_ref, xs_ref, xn_ref, zero_ref, sem):
    tm = h_ref.shape[0]
    fill_rows = zero_ref.shape[0]

    @pl.when(pl.program_id(0) == 0)
    def _():
        zero_ref[...] = jnp.zeros_like(zero_ref)
        fill = lambda row: pltpu.make_async_copy(
            zero_ref, xs_ref.at[pl.ds(pl.multiple_of(row * ROW_PIECES, ROW_PIECES), fill_rows)], sem.at[0])
        for e in range(N_EXPERTS):
            fill(pad_ref[e]).start()
        for e in range(N_EXPERTS):
            fill(pad_ref[e]).wait()
        n_total = xs_ref.shape[0] // fill_rows
        te = fill_rows // ROW_PIECES

        def tail_start(j, _):
            fill(j * te).start()
            return 0

        def tail_wait(j, _):
            fill(j * te).wait()
            return 0

        lax.fori_loop(nu_ref[0], n_total, tail_start, 0)
        lax.fori_loop(nu_ref[0], n_total, tail_wait, 0)

    i = pl.program_id(0)
    slot = i % 2
    _to_token_tiles(_rmsnorm(h_ref[...], g_ref[...]), xn_ref.at[slot])

    def scatters(da_ref, db_ref, s, go):
        def body(r, _):
            go(_token_copy(xn_ref.at[s], r, xs_ref, da_ref[0, 0, r], sem.at[s]), 0)
            go(_token_copy(xn_ref.at[s], r, xs_ref, db_ref[0, 0, r], sem.at[s]), 1)
            return 0
        lax.fori_loop(0, tm, body, 0, unroll=DMA_UNROLL)

    start = lambda cp, prio: cp.start(priority=prio)
    wait = lambda cp, prio: cp.wait()
    scatters(d0_ref, d1_ref, slot, start)

    @pl.when(i > 0)
    def _():
        scatters(d0p_ref, d1p_ref, 1 - slot, wait)

    @pl.when(i == pl.num_programs(0) - 1)
    def _():
        scatters(d0_ref, d1_ref, slot, wait)


def _dispatch(h2, g_ffn, dest0, dest1, pad_start, n_used, xs_rows, tm, te):
    N, D = h2.shape
    nt = N // tm
    idx = pl.BlockSpec((1, 1, tm), lambda i, pad, nu: (i, 0, 0), memory_space=pltpu.SMEM)
    idx_prev = pl.BlockSpec((1, 1, tm), lambda i, pad, nu: (jnp.maximum(i - 1, 0), 0, 0), memory_space=pltpu.SMEM)
    d0, d1 = dest0.reshape(nt, 1, tm), dest1.reshape(nt, 1, tm)
    grid_spec = pltpu.PrefetchScalarGridSpec(
        num_scalar_prefetch=2,
        grid=(nt,),
        in_specs=[idx, idx, idx_prev, idx_prev, pl.BlockSpec((tm, D), lambda i, pad, nu: (i, 0)),
                  pl.BlockSpec(g_ffn.shape, lambda i, pad, nu: (0, 0))],
        out_specs=pl.BlockSpec(memory_space=pl.ANY),
        scratch_shapes=[pltpu.VMEM((2, tm * ROW_PIECES, LANES), F32), pltpu.VMEM((te * ROW_PIECES, LANES), F32),
                        pltpu.SemaphoreType.DMA((2,))],
    )
    return pl.pallas_call(
        _dispatch_kernel,
        grid_spec=grid_spec,
        out_shape=jax.ShapeDtypeStruct(((xs_rows + te) * ROW_PIECES, LANES), F32),
        compiler_params=_params("arbitrary"),
        name="dispatch",
    )(pad_start, n_used, d0, d1, d0, d1, h2, g_ffn)


def _experts_kernel(te_ref, nu_ref, x_ref, wgu_ref, wdn_ref, y_ref, wgu_bf_ref, wdn_bf_ref):
    j = pl.program_id(0)
    te = x_ref.shape[0] // ROW_PIECES

    @pl.when((j == 0) | (te_ref[j] != te_ref[jnp.maximum(j - 1, 0)]))
    def _():
        wgu_bf_ref[...] = wgu_ref[0].astype(BF16)
        wdn_bf_ref[...] = wdn_ref[0].astype(BF16)

    @pl.when(j < nu_ref[0])
    def _():
        gu = _dot(_from_token_tiles(x_ref, te).astype(BF16), wgu_bf_ref[...])
        gate, up = gu[:, :D_EXPERT], gu[:, D_EXPERT:]
        act = gate / (1.0 + jnp.exp(-gate)) * up
        _to_token_tiles(_dot(act.astype(BF16), wdn_bf_ref[...]), y_ref)

    @pl.when(j >= nu_ref[0])
    def _():
        y_ref[...] = jnp.zeros_like(y_ref)


def _experts(xs, wgu, wdn, layer, tile_expert, n_used, xs_rows, te):
    D = D_MODEL
    tiles = pl.BlockSpec((te * ROW_PIECES, LANES), lambda j, te_, nu: (j, 0))
    grid_spec = pltpu.PrefetchScalarGridSpec(
        num_scalar_prefetch=2,
        grid=(xs_rows // te,),
        in_specs=[tiles,
                  pl.BlockSpec((None, 1, D, 2 * D_EXPERT), lambda j, te_, nu: (layer, te_[j], 0, 0)),
                  pl.BlockSpec((None, 1, D_EXPERT, D), lambda j, te_, nu: (layer, te_[j], 0, 0))],
        out_specs=tiles,
        scratch_shapes=[pltpu.VMEM((D, 2 * D_EXPERT), BF16), pltpu.VMEM((D_EXPERT, D), BF16)],
    )
    return pl.pallas_call(
        _experts_kernel,
        grid_spec=grid_spec,
        out_shape=jax.ShapeDtypeStruct((xs_rows * ROW_PIECES, LANES), F32),
        compiler_params=_params("arbitrary"),
        name="experts",
    )(tile_expert, n_used, xs, wgu, wdn)


def _combine_kernel(d0_ref, d1_ref, d0n_ref, d1n_ref, h_ref, info_ref, gfin_ref, ys_ref, o_ref, y0_ref, y1_ref, sem,
                    *, final_norm):
    tm = h_ref.shape[0]
    i = pl.program_id(0)
    slot = i % 2

    def gathers(da_ref, db_ref, s, go):
        def body(r, _):
            go(_token_copy(ys_ref, da_ref[0, 0, r], y0_ref.at[s], r, sem.at[s]), 0)
            go(_token_copy(ys_ref, db_ref[0, 0, r], y1_ref.at[s], r, sem.at[s]), 1)
            return 0
        lax.fori_loop(0, tm, body, 0, unroll=DMA_UNROLL)

    start = lambda cp, prio: cp.start(priority=prio)
    wait = lambda cp, prio: cp.wait()

    @pl.when(i == 0)
    def _():
        gathers(d0_ref, d1_ref, 0, start)

    @pl.when(i + 1 < pl.num_programs(0))
    def _():
        gathers(d0n_ref, d1n_ref, 1 - slot, start)

    gathers(d0_ref, d1_ref, slot, wait)
    info = info_ref[...]
    out = (h_ref[...] + info[:, INFO_G0:INFO_G0 + 1] * _from_token_tiles(y0_ref.at[slot], tm)
           + info[:, INFO_G1:INFO_G1 + 1] * _from_token_tiles(y1_ref.at[slot], tm))
    if final_norm:
        out = _rmsnorm(out, gfin_ref[...])
    o_ref[...] = out


def _combine(h2, info, g_final, ys, dest0, dest1, tm, final_norm):
    N, D = h2.shape
    nt = N // tm
    idx = pl.BlockSpec((1, 1, tm), lambda i: (i, 0, 0), memory_space=pltpu.SMEM)
    idx_next = pl.BlockSpec((1, 1, tm), lambda i: (jnp.minimum(i + 1, nt - 1), 0, 0), memory_space=pltpu.SMEM)
    d0, d1 = dest0.reshape(nt, 1, tm), dest1.reshape(nt, 1, tm)
    buf = pltpu.VMEM((2, tm * ROW_PIECES, LANES), F32)
    return pl.pallas_call(
        functools.partial(_combine_kernel, final_norm=final_norm),
        grid=(nt,),
        in_specs=[idx, idx, idx_next, idx_next, pl.BlockSpec((tm, D), lambda i: (i, 0)),
                  pl.BlockSpec((tm, LANES), lambda i: (i, 0)), _full(g_final.shape), pl.BlockSpec(memory_space=pl.ANY)],
        out_specs=pl.BlockSpec((tm, D), lambda i: (i, 0)),
        out_shape=jax.ShapeDtypeStruct((N, D), F32),
        scratch_shapes=[buf, buf, pltpu.SemaphoreType.DMA((2,))],
        compiler_params=_params("arbitrary"),
        name="combine",
    )(d0, d1, d0, d1, h2, info, g_final, ys)


IN_SIZES = (NSA_WIDTH,) + (KV_WIDTH,) * 6 + (NSA_HEADS * 3,) + (CONV_WIDTH,) * 3 + (HGRN_WIDTH,) * 4
IN_OFF = tuple(int(v) for v in np.cumsum((0,) + IN_SIZES))


def _rope_tables(T):
    half = HEAD_DIM // 2
    inv = ROPE_THETA ** (-jnp.arange(half, dtype=F32) / half)
    ang = jnp.arange(T).astype(F32)[:, None] * inv[None, :]
    cos, sin = jnp.cos(ang), jnp.sin(ang)
    zero = jnp.zeros_like(sin)
    reps = LANES // HEAD_DIM
    cos_t = jnp.tile(jnp.concatenate([cos, cos], axis=1), (1, reps))
    sin_a = jnp.tile(jnp.concatenate([-sin, zero], axis=1), (1, reps))
    sin_b = jnp.tile(jnp.concatenate([zero, sin], axis=1), (1, reps))
    return cos_t, sin_a, sin_b


def _cmp_to_sel(T):
    nc_pad, ns = T // CMP_STRIDE, T // SEL_BLOCK
    nc = (T - CMP_LEN) // CMP_STRIDE + 1
    cs = np.arange(nc_pad) * CMP_STRIDE
    ss = np.arange(ns) * SEL_BLOCK
    ov = np.clip(np.minimum(cs[:, None] + CMP_LEN, ss[None, :] + SEL_BLOCK)
                 - np.maximum(cs[:, None], ss[None, :]), 0, None) / CMP_LEN
    ov[nc:] = 0.0
    return jnp.asarray(np.pad(ov, ((0, 0), (0, LANES - ns))), dtype=BF16)


def _q_weight(w_in_l):
    wq = w_in_l[:, IN_OFF[0]:IN_OFF[1]].reshape(D_MODEL, NSA_KV_HEADS, NSA_GROUP, 1, HEAD_DIM)
    half = jnp.eye(NSA_KV_HEADS, dtype=wq.dtype)[None, :, None, :, None]
    return (wq * half).reshape(D_MODEL, Q_AUG_WIDTH)


def _compress_weights(pe, w1, w2):
    eye = jnp.eye(NSA_KV_HEADS, dtype=F32)
    half_len = CMP_LEN // 2
    w1r = w1.reshape(2, 2, half_len, HEAD_DIM, CMP_HIDDEN)
    w1_aug = jnp.einsum("sxldm,hg->sxlhdgm", w1r, eye).reshape(2, 2, half_len * KV_WIDTH, NSA_KV_HEADS * CMP_HIDDEN)
    w2_aug = jnp.einsum("smd,hg->shmgd", w2, eye).reshape(2, NSA_KV_HEADS * CMP_HIDDEN, KV_WIDTH)
    pe_r = jnp.broadcast_to(pe.reshape(2, 2, half_len, 1, HEAD_DIM), (2, 2, half_len, NSA_KV_HEADS, HEAD_DIM))
    pe_aug = jnp.pad(pe_r.reshape(2, 2, half_len * KV_WIDTH), ((0, 0), (0, SUBLANES - 2), (0, 0)))
    return pe_aug, w1_aug[:, 0].astype(BF16), w1_aug[:, 1].astype(BF16), w2_aug.astype(BF16)


def _gate_weight(w_in_l):
    wg = w_in_l[:, IN_OFF[7]:IN_OFF[8]].reshape(D_MODEL, NSA_KV_HEADS, NSA_GROUP * 3)
    return jnp.pad(wg, ((0, 0), (0, 0), (0, LANES - NSA_GROUP * 3))).reshape(D_MODEL, NSA_KV_HEADS * LANES)


def _segments(counts, n_tok, te):
    c0 = counts[0, EXPERT_LANE0:EXPERT_LANE0 + N_EXPERTS].astype(jnp.int32)
    c1 = counts[1, EXPERT_LANE0:EXPERT_LANE0 + N_EXPERTS].astype(jnp.int32)
    tiles = (c0 + c1 + te - 1) // te
    tile_end = jnp.cumsum(tiles)
    off = (tile_end - tiles) * te
    tab = jnp.zeros((SUBLANES, LANES), F32)
    tab = tab.at[0, EXPERT_LANE0:EXPERT_LANE0 + N_EXPERTS].set(off.astype(F32))
    tab = tab.at[1, EXPERT_LANE0:EXPERT_LANE0 + N_EXPERTS].set((off + c0).astype(F32))
    n_tiles = (2 * n_tok) // te + N_EXPERTS
    tile_expert = jnp.sum(tile_end[None, :] <= jnp.arange(n_tiles)[:, None], axis=1)
    tile_expert = jnp.minimum(tile_expert, N_EXPERTS - 1).astype(jnp.int32)
    return tab, off + c0 + c1, tile_expert, tile_end[-1:].astype(jnp.int32)


def kernel(x, mem, w_in, conv_w, cmp_pe, cmp_w1, cmp_w2, hgrn_lb_logits, hgrn_norm, w_out, cross_wq, cross_wkv,
           cross_wo, router_group_w, router_group_b, router_expert_w, router_expert_b, expert_w_gate_up,
           expert_w_down, norm_mix, norm_cross, norm_mem, norm_ffn, norm_final):
    B, T, D = x.shape
    depth = w_in.shape[0]
    n_tok = B * T
    tm = min(512, T)
    tm_row = min(256, T)
    te = 512

    cos_t, sin_a, sin_b = _rope_tables(T)
    c2s = _cmp_to_sel(T)
    ltri = jnp.asarray(np.tril(np.ones((tm, tm), np.float32), -1), dtype=BF16)

    p_lb = jax.nn.softmax(hgrn_lb_logits.astype(F32), axis=0)
    lower_bounds = jnp.cumsum(p_lb, axis=0) - p_lb[0:1]
    xs_rows = 2 * n_tok + N_EXPERTS * te

    h = x
    for l in range(depth):
        wl = w_in[l]
        piece = lambda a, b: wl[:, IN_OFF[a]:IN_OFF[b]].astype(BF16)
        cw = jnp.pad(conv_w[l], ((0, SUBLANES - CONV_K), (0, 0)))
        (q, qr, kc, vc, ka, vs, kw, vw, gs, o_conv, hg) = _inproj(
            h, norm_mix[l][None], cos_t, sin_a, sin_b, _q_weight(wl).astype(BF16), piece(1, 7),
            _gate_weight(wl).astype(BF16), piece(8, 11), piece(11, 15), cw, tm)

        chunks = lambda a: a.reshape(B, T // CMP_STRIDE, CMP_STRIDE * KV_WIDTH)
        kvc = _compress(jnp.stack([chunks(kc), chunks(vc)]), *_compress_weights(cmp_pe[l], cmp_w1[l], cmp_w2[l]))
        o_nsa = _nsa(q, qr, kvc, ka, vs, kw, vw, gs, c2s)

        lb = lower_bounds[l][None]
        lb_params = jnp.concatenate([jnp.log(jnp.maximum(lb, LB_FLOOR)), jnp.log1p(-lb), 1.0 - lb,
                                     jnp.zeros((SUBLANES - 3, HGRN_WIDTH), F32)], axis=0)
        o_hgrn = _hgrn(hg, lb_params, hgrn_norm[l][None], tm)

        kmem, vmem = _memkv(mem, norm_mem[l][None], cross_wkv[l].astype(BF16))

        wr = jnp.zeros((D, LANES), F32).at[:, :N_GROUPS].set(router_group_w[l])
        wr = wr.at[:, EXPERT_LANE0:EXPERT_LANE0 + N_EXPERTS].set(router_expert_w[l])
        br = jnp.zeros((1, LANES), F32).at[0, :N_GROUPS].set(router_group_b[l])
        br = br.at[0, EXPERT_LANE0:EXPERT_LANE0 + N_EXPERTS].set(router_expert_b[l])
        wr_hi = wr.astype(BF16)
        wr_lo = jnp.concatenate([wr_hi, (wr - wr_hi.astype(F32)).astype(BF16)], axis=1)
        h2, info, counts = _post(h, o_nsa, o_conv, o_hgrn, w_out[l].astype(BF16), norm_cross[l][None],
                                 cross_wq[l].astype(BF16), kmem, vmem, cross_wo[l].astype(BF16), norm_ffn[l][None],
                                 wr_hi, wr_lo, br, ltri, tm)

        h2 = h2.reshape(n_tok, D)
        info = info.reshape(n_tok, LANES)
        tab, pad_start, tile_expert, n_used = _segments(counts, n_tok, te)
        dest = _route(info, tab, tm)
        dest0, dest1 = dest[:, 0], dest[:, 1]
        xs = _dispatch(h2, norm_ffn[l][None], dest0, dest1, pad_start, n_used, xs_rows, tm_row, te)
        ys = _experts(xs, expert_w_gate_up, expert_w_down, l, tile_expert, n_used, xs_rows, te)
        h = _combine(h2, info, norm_final[None], ys, dest0, dest1, tm_row, l == depth - 1).reshape(B, T, D)
    return h
```

```python
import functools

import numpy as np
import jax
import jax.numpy as jnp
from jax import lax
from jax.experimental import pallas as pl
from jax.experimental.pallas import tpu as pltpu

F32 = jnp.float32
BF16 = jnp.bfloat16

D_MODEL = 1024
HEAD_DIM = 64
NSA_HEADS = 8
NSA_KV_HEADS = 2
NSA_GROUP = NSA_HEADS // NSA_KV_HEADS
NSA_WIDTH = NSA_HEADS * HEAD_DIM
KV_WIDTH = NSA_KV_HEADS * HEAD_DIM
CMP_LEN = 32
CMP_STRIDE = 16
CMP_HIDDEN = 2 * HEAD_DIM
SEL_BLOCK = 64
N_SELECT = 16
WINDOW = 512
Q_BLOCK = 128
NSA_Q_TILE = 256
MASK_VALUE = -1e30
SEL_FORCE = 1e6
CONV_WIDTH = 256
CONV_K = 3
HGRN_HEADS = 4
HGRN_WIDTH = HGRN_HEADS * HEAD_DIM
HGRN_CHUNK = 64
HGRN_SUB = 16
LB_FLOOR = 1e-30
ROPE_THETA = 10000.0
CROSS_HEADS = 4
CROSS_WIDTH = CROSS_HEADS * HEAD_DIM
N_GROUPS = 4
EXPERTS_PER_GROUP = 8
N_EXPERTS = N_GROUPS * EXPERTS_PER_GROUP
D_EXPERT = 512
RMS_EPS = 1e-6
QK_SCALE = HEAD_DIM ** -0.5
LOG2E = 1.4426950408889634
HEAD_SHIFT = HEAD_DIM.bit_length() - 1
SEL_SHIFT = SEL_BLOCK.bit_length() - 1

LANES = 128
SUBLANES = 8
VMEM_LIMIT_BYTES = 48 * 1024 * 1024

Q_AUG_WIDTH = NSA_HEADS * LANES

EXPERT_LANE0 = 32
INFO_E0, INFO_E1, INFO_R0, INFO_R1, INFO_G0, INFO_G1 = 0, 1, 2, 3, 4, 5


def _dot(a, b):
    return jnp.dot(a, b, preferred_element_type=F32)


def _dot_nt(a, b):
    return lax.dot_general(a, b, (((1,), (1,)), ((), ())), preferred_element_type=F32)


def _split_bf16(x):
    hi = x.astype(BF16)
    lo = (x - hi.astype(F32)).astype(BF16)
    return hi, lo


def _rmsnorm(x, g):
    return x * lax.rsqrt(jnp.mean(x * x, axis=-1, keepdims=True) + RMS_EPS) * g


def _params(*sem):
    return pltpu.CompilerParams(dimension_semantics=sem, vmem_limit_bytes=VMEM_LIMIT_BYTES)


def _full(shape):
    nd = len(shape)
    return pl.BlockSpec(shape, lambda *_: (0,) * nd)


def _rope(x, cos, sin_a, sin_b):
    width = x.shape[1]
    reps = width // LANES
    if reps > 1:
        cos = jnp.concatenate([cos] * reps, axis=1)
        sin_a = jnp.concatenate([sin_a] * reps, axis=1)
        sin_b = jnp.concatenate([sin_b] * reps, axis=1)
    half = HEAD_DIM // 2
    return x * cos + pltpu.roll(x, width - half, 1) * sin_a + pltpu.roll(x, half, 1) * sin_b


def _inproj_kernel(h_ref, g_ref, cos_ref, sa_ref, sb_ref, wq_ref, wkv_ref, wg_ref, wc_ref, wh_ref, cw_ref,
                   q_ref, qr_ref, kc_ref, vc_ref, ks_ref, vs_ref, kw_ref, vw_ref, gs_ref, oc_ref, hg_ref,
                   carry_ref):
    tm = h_ref.shape[1]

    @pl.when(pl.program_id(1) == 0)
    def _():
        carry_ref[...] = jnp.zeros_like(carry_ref)

    xb = _rmsnorm(h_ref[0], g_ref[...]).astype(BF16)
    cos, sa, sb = cos_ref[...], sa_ref[...], sb_ref[...]

    q = _dot(xb, wq_ref[...]) * (QK_SCALE * LOG2E)
    q_ref[0] = q.astype(BF16)
    qr_ref[0] = _rope(q, cos, sa, sb).astype(BF16)

    kv = _dot(xb, wkv_ref[...])
    kw_ = KV_WIDTH
    kc_ref[0] = kv[:, 0 * kw_:1 * kw_].astype(BF16)
    vc_ref[0] = kv[:, 1 * kw_:2 * kw_].astype(BF16)
    pos = pl.program_id(1) * tm + lax.broadcasted_iota(jnp.int32, (tm, LANES), 0)
    blk = lax.broadcasted_iota(jnp.int32, (tm, LANES), 1) == lax.shift_right_logical(pos, SEL_SHIFT)
    ks_ref[0, :, 0:kw_] = _rope(kv[:, 2 * kw_:3 * kw_], cos, sa, sb).astype(BF16)
    ks_ref[0, :, kw_:kw_ + LANES] = jnp.where(blk, 1.0, 0.0).astype(BF16)
    vs_ref[0] = kv[:, 3 * kw_:4 * kw_].astype(BF16)
    kw_ref[0] = _rope(kv[:, 4 * kw_:5 * kw_], cos, sa, sb).astype(BF16)
    vw_ref[0] = kv[:, 5 * kw_:6 * kw_].astype(BF16)

    gl = _dot(xb, wg_ref[...])
    gs_ref[0] = 1.0 / (1.0 + jnp.exp(-gl))

    cv = _dot(xb, wc_ref[...])
    cb = cv[:, 0:CONV_WIDTH]
    u = cv[:, CONV_WIDTH:2 * CONV_WIDTH] * cv[:, 2 * CONV_WIDTH:3 * CONV_WIDTH]
    ext = jnp.concatenate([carry_ref[...], u], axis=0)
    u1 = pltpu.roll(ext, 1, 0)[SUBLANES:]
    u2 = pltpu.roll(ext, 2, 0)[SUBLANES:]
    cw = cw_ref[...]
    oc_ref[0] = (cb * (cw[0:1] * u2 + cw[1:2] * u1 + cw[2:3] * u)).astype(BF16)
    carry_ref[...] = u[tm - SUBLANES:tm]

    hg_ref[0] = _dot(xb, wh_ref[...])


def _inproj(h, gain, cos, sin_a, sin_b, wq, wkv, wg, wc, wh, cw, tm):
    B, T, D = h.shape
    nt = T // tm
    tok = lambda w: pl.BlockSpec((1, tm, w), lambda b, t: (b, t, 0))
    tab = pl.BlockSpec((tm, LANES), lambda b, t: (t, 0))
    kvw = KV_WIDTH
    out_w = [(Q_AUG_WIDTH, BF16), (Q_AUG_WIDTH, BF16), (kvw, BF16), (kvw, BF16), (kvw + LANES, BF16), (kvw, BF16),
             (kvw, BF16), (kvw, BF16), (2 * LANES, F32), (CONV_WIDTH, BF16), (4 * HGRN_WIDTH, F32)]
    return pl.pallas_call(
        _inproj_kernel,
        grid=(B, nt),
        in_specs=[tok(D), _full(gain.shape), tab, tab, tab, _full(wq.shape), _full(wkv.shape), _full(wg.shape),
                  _full(wc.shape), _full(wh.shape), _full(cw.shape)],
        out_specs=[tok(w) for w, _ in out_w],
        out_shape=[jax.ShapeDtypeStruct((B, T, w), dt) for w, dt in out_w],
        scratch_shapes=[pltpu.VMEM((SUBLANES, CONV_WIDTH), F32)],
        compiler_params=_params("arbitrary", "arbitrary"),
        name="inproj",
    )(h, gain, cos, sin_a, sin_b, wq, wkv, wg, wc, wh, cw)


def _compress_kernel(x_ref, pe_ref, w1a_ref, w1b_ref, w2_ref, o_ref):
    x = x_ref[0, 0]
    n = x.shape[0]
    w1a, w1b = w1a_ref[0], w1b_ref[0]
    pe_a = jnp.broadcast_to(pe_ref[0, 0:1], (SUBLANES, x.shape[1])).astype(BF16)
    pe_b = jnp.broadcast_to(pe_ref[0, 1:2], (SUBLANES, x.shape[1])).astype(BF16)
    pe_term = (_dot(pe_a, w1a) + _dot(pe_b, w1b))[0:1]
    hid = _dot(x, w1a) + pltpu.roll(_dot(x, w1b), n - 1, 0) + pe_term
    act = hid / (1.0 + jnp.exp(-hid))
    o_ref[0, 0] = _dot(act.astype(BF16), w2_ref[0]).astype(BF16)


def _compress(xkv, pe, w1a, w1b, w2):
    _, B, n, width = xkv.shape
    hid = NSA_KV_HEADS * CMP_HIDDEN
    per_kv = lambda *shape: pl.BlockSpec((1,) + shape, lambda s, b: (s,) + (0,) * len(shape))
    return pl.pallas_call(
        _compress_kernel,
        grid=(2, B),
        in_specs=[pl.BlockSpec((1, 1, n, width), lambda s, b: (s, b, 0, 0)), per_kv(SUBLANES, width),
                  per_kv(width, hid), per_kv(width, hid), per_kv(hid, KV_WIDTH)],
        out_specs=pl.BlockSpec((1, 1, n, KV_WIDTH), lambda s, b: (s, b, 0, 0)),
        out_shape=jax.ShapeDtypeStruct((2, B, n, KV_WIDTH), BF16),
        compiler_params=_params("arbitrary", "arbitrary"),
        name="compress",
    )(xkv, pe, w1a, w1b, w2)


def _masked_softmax2(s, mask):
    s = jnp.where(mask, s, MASK_VALUE)
    m = jnp.max(s, axis=1, keepdims=True)
    e = jnp.exp2(s - m)
    den = jnp.sum(e, axis=1, keepdims=True)
    return e, jnp.where(m > 0.5 * MASK_VALUE, 1.0 / den, 0.0)


def _nsa_kernel(q_ref, qr_ref, kc_ref, vc_ref, ka_ref, vs_ref, kw_ref, vw_ref, gs_ref, c2s_ref, o_ref, *, key_tile):
    i = pl.program_id(1)
    qb, g_, n_h = NSA_Q_TILE, NSA_GROUP, NSA_KV_HEADS
    n_grp = n_h * g_
    rows = n_grp * qb
    sel_rows = n_h * qb
    grp_rows = lambda a, hg: a[hg * qb:(hg + 1) * qb]

    def q_all(ref):
        return jnp.concatenate([ref[0, :, hg * LANES:(hg + 1) * LANES] for hg in range(n_grp)], axis=0)

    t_row = i * qb + (lax.broadcasted_iota(jnp.int32, (rows, 1), 0) & (qb - 1))
    t_sel = i * qb + (lax.broadcasted_iota(jnp.int32, (sel_rows, 1), 0) & (qb - 1))

    kc, vc = kc_ref[0, 0], vc_ref[0, 0]
    c_end = lax.broadcasted_iota(jnp.int32, (1, kc.shape[0]), 1) * CMP_STRIDE + (CMP_LEN - 1)
    e_c, inv_c = _masked_softmax2(_dot_nt(q_all(q_ref), kc), c_end <= t_row)
    p_c = e_c * inv_c
    o_c = _dot(p_c.astype(BF16), vc)
    p_g = []
    for h in range(n_h):
        acc_g = grp_rows(p_c, h * g_)
        for g in range(1, g_):
            acc_g = acc_g + grp_rows(p_c, h * g_ + g)
        p_g.append(acc_g)
    imp = _dot(jnp.concatenate(p_g, axis=0).astype(BF16), c2s_ref[...])

    qr = q_all(qr_ref)
    wk = WINDOW + qb
    w_start = pl.multiple_of(jnp.maximum(i * qb - WINDOW, 0), qb)
    kwin = kw_ref[0, pl.ds(w_start, wk), :]
    vwin = vw_ref[0, pl.ds(w_start, wk), :]
    dist = t_row - (w_start + lax.broadcasted_iota(jnp.int32, (1, wk), 1))
    e_w, inv_w = _masked_softmax2(_dot_nt(qr, kwin), (dist >= 0) & (dist < WINDOW))
    o_w = _dot(e_w.astype(BF16), vwin) * inv_w

    cur = lax.shift_right_logical(t_sel, SEL_SHIFT)
    j = lax.broadcasted_iota(jnp.int32, (sel_rows, LANES), 1)
    jf = j.astype(F32)
    forced = (j == 0) | (j == cur) | (j == cur - 1)
    work = jnp.where(forced | (j > cur), -jnp.inf, imp)
    sel = jnp.where(forced, 1.0, 0.0)
    for _ in range(N_SELECT - 3):
        mx = jnp.max(work, axis=1, keepdims=True)
        first = jnp.min(jnp.where(work == mx, jf, float(LANES)), axis=1, keepdims=True)
        pick = jf == first
        sel = jnp.where(pick, 1.0, sel)
        work = jnp.where(pick, -jnp.inf, work)
    sel_bias = jnp.where(sel > 0, 0.0, MASK_VALUE).astype(BF16)
    q_aug = jnp.concatenate([jnp.concatenate([grp_rows(qr, hg), grp_rows(sel_bias, hg // g_)], axis=1)
                             for hg in range(n_grp)], axis=0)

    def sel_tile(q_rows, t_rows, tile, width, carry, causal):
        start = pl.multiple_of(tile * width, width)
        m, l, acc = carry
        s = _dot_nt(q_rows, ka_ref[0, pl.ds(start, width), :])
        if causal:
            s = jnp.where(start + lax.broadcasted_iota(jnp.int32, (1, width), 1) <= t_rows, s, MASK_VALUE)
        m_new = jnp.maximum(m, jnp.max(s, axis=1, keepdims=True))
        alpha = jnp.exp2(m - m_new)
        p = jnp.exp2(s - m_new)
        l = alpha * l + jnp.sum(p, axis=1, keepdims=True)
        acc = alpha * acc + _dot(p.astype(BF16), vs_ref[0, pl.ds(start, width), :])
        return m_new, l, acc

    n_full = (i * qb) // key_tile
    init = (jnp.full((rows, 1), MASK_VALUE, F32), jnp.zeros((rows, 1), F32), jnp.zeros((rows, LANES), F32))
    carry = lax.fori_loop(0, n_full, lambda kt, c: sel_tile(q_aug, t_row, kt, key_tile, c, False), init)
    _, l_s, acc_s = sel_tile(q_aug, t_row, n_full, key_tile, carry, True)
    o_s = acc_s * (1.0 / l_s)

    gs = gs_ref[0]
    low_half = lax.broadcasted_iota(jnp.int32, (qb, LANES), 1) < HEAD_DIM
    out_groups = []
    for h in range(n_h):
        mixed = []
        for g in range(g_):
            hg = h * g_ + g
            c = h * LANES + 3 * g
            mixed.append(gs[:, c:c + 1] * grp_rows(o_c, hg) + gs[:, c + 1:c + 2] * grp_rows(o_s, hg)
                         + gs[:, c + 2:c + 3] * grp_rows(o_w, hg))
        for pair in range(g_ // 2):
            a, b = mixed[2 * pair], mixed[2 * pair + 1]
            if h == 0:
                out_groups.append(jnp.where(low_half, a, pltpu.roll(b, HEAD_DIM, 1)))
            else:
                out_groups.append(jnp.where(low_half, pltpu.roll(a, HEAD_DIM, 1), b))
    o_ref[0] = jnp.concatenate(out_groups, axis=1).astype(o_ref.dtype)


def _nsa(q, qr, kvc, ka, vs, kw, vw, gs, c2s):
    B, T, _ = q.shape
    assert T // SEL_BLOCK <= LANES
    nc = kvc.shape[2]
    key_tile = min(1024, T)
    blk = lambda w: pl.BlockSpec((1, NSA_Q_TILE, w), lambda b, i: (b, i, 0))
    seq = lambda w: pl.BlockSpec((1, T, w), lambda b, i: (b, 0, 0), pipeline_mode=pl.Buffered(1))
    cmp_spec = lambda s: pl.BlockSpec((1, 1, nc, KV_WIDTH), lambda b, i: (s, b, 0, 0))
    return pl.pallas_call(
        functools.partial(_nsa_kernel, key_tile=key_tile),
        grid=(B, T // NSA_Q_TILE),
        in_specs=[blk(Q_AUG_WIDTH), blk(Q_AUG_WIDTH), cmp_spec(0), cmp_spec(1), seq(KV_WIDTH + LANES), seq(KV_WIDTH),
                  seq(KV_WIDTH), seq(KV_WIDTH), blk(2 * LANES), _full(c2s.shape)],
        out_specs=blk(NSA_WIDTH),
        out_shape=jax.ShapeDtypeStruct((B, T, NSA_WIDTH), BF16),
        compiler_params=_params("arbitrary", "arbitrary"),
        name="nsa",
    )(q, qr, kvc, kvc, ka, vs, kw, vw, gs, c2s)


def _hgrn_kernel(x_ref, lbp_ref, ng_ref, o_ref, st_ref):
    w = HGRN_WIDTH
    L = HGRN_CHUNK
    sub = HGRN_SUB
    n_chunks = x_ref.shape[1] // L
    n_seq = x_ref.shape[0]
    rows = n_seq * L
    seqs = range(n_seq)

    @pl.when(pl.program_id(1) == 0)
    def _():
        st_ref[...] = jnp.zeros_like(st_ref)

    log_lb, log_1m_lb, one_m_lb = lbp_ref[0:1], lbp_ref[1:2], lbp_ref[2:3]
    ng = ng_ref[...]
    ri = lax.broadcasted_iota(jnp.int32, (rows, rows), 0)
    ci = lax.broadcasted_iota(jnp.int32, (rows, rows), 1)
    chunk_shift = L.bit_length() - 1
    same_chunk = lax.shift_right_logical(ri, chunk_shift) == lax.shift_right_logical(ci, chunk_shift)
    tril = jnp.where((ci <= ri) & same_chunk, 1.0, 0.0).astype(BF16)
    hr = lax.shift_right_logical(lax.broadcasted_iota(jnp.int32, (w, w), 0), HEAD_SHIFT)
    hc = lax.shift_right_logical(lax.broadcasted_iota(jnp.int32, (w, w), 1), HEAD_SHIFT)
    same_head = hr == hc
    head_ones = jnp.where(same_head, 1.0, 0.0).astype(BF16)
    sub_row = lax.broadcasted_iota(jnp.int32, (rows, 1), 0) & (sub - 1)
    lane_head = lax.shift_right_logical(lax.broadcasted_iota(jnp.int32, (1, w), 1), HEAD_SHIFT)
    stack = lambda pieces: jnp.concatenate(pieces, axis=0)

    def seq_index(idx, per_seq):
        out = jnp.zeros(idx.shape, jnp.int32)
        for s in range(1, n_seq):
            out = out + jnp.where(idx >= s * per_seq, 1, 0)
        return out

    def chunk(c, _):
        r0 = pl.multiple_of(c * L, L)
        load = lambda col: x_ref[:, pl.ds(r0, L), col * w:(col + 1) * w].reshape(rows, w)
        q = load(0) * QK_SCALE
        z, v, g = load(1), load(2), load(3)

        log_sig = jnp.minimum(z, 0.0) - jnp.log(1.0 + jnp.exp(-jnp.abs(z)))
        bb = log_1m_lb + log_sig
        log_f = jnp.maximum(log_lb, bb) + jnp.log(1.0 + jnp.exp(-jnp.abs(log_lb - bb)))
        k = one_m_lb / (1.0 + jnp.exp(z))

        lf_hi, lf_lo = _split_bf16(log_f)
        b = _dot(tril, lf_hi) + _dot(tril, lf_lo)
        seq_rows = lambda a, s, lo, n: a[s * L + lo:s * L + lo + n]
        q_dec = (q * jnp.exp(b)).astype(BF16)
        o = stack([_dot_nt(seq_rows(q_dec, s, 0, L), st_ref[s].astype(BF16)) for s in seqs])
        for d in range(sub):
            if d == 0:
                wgt = q * k
                vs = v
            else:
                wgt = jnp.where(sub_row >= d, q * pltpu.roll(k, d, 0) * jnp.exp(b - pltpu.roll(b, d, 0)), 0.0)
                vs = pltpu.roll(v, d, 0)
            o = o + _dot(wgt.astype(BF16), head_ones) * vs
        parts = [[jnp.zeros((sub, w), F32)] for _ in seqs]
        for lo in range(sub, L, sub):
            qs = stack([seq_rows(q, s, lo, sub) * jnp.exp(seq_rows(b, s, lo, sub) - seq_rows(b, s, lo - 1, 1))
                        for s in seqs])
            ks = stack([seq_rows(k, s, 0, lo) * jnp.exp(seq_rows(b, s, lo - 1, 1) - seq_rows(b, s, 0, lo))
                        for s in seqs]).astype(BF16)
            vv = stack([seq_rows(v, s, 0, lo) for s in seqs])
            same_seq = (seq_index(lax.broadcasted_iota(jnp.int32, (n_seq * sub, n_seq * lo), 0), sub)
                        == seq_index(lax.broadcasted_iota(jnp.int32, (n_seq * sub, n_seq * lo), 1), lo))
            acc = jnp.zeros((n_seq * sub, w), F32)
            for hd in range(HGRN_HEADS):
                in_head = lane_head == hd
                a = jnp.where(same_seq, _dot_nt(jnp.where(in_head, qs, 0.0).astype(BF16), ks), 0.0)
                acc = acc + _dot(a.astype(BF16), jnp.where(in_head, vv, 0.0).astype(BF16))
            for s in seqs:
                parts[s].append(acc[s * sub:(s + 1) * sub])
        o = o + stack([piece for s in seqs for piece in parts[s]])
        for s in seqs:
            b_last = seq_rows(b, s, L - 1, 1)
            kd = seq_rows(k, s, 0, L) * jnp.exp(b_last - seq_rows(b, s, 0, L))
            upd = _dot(seq_rows(v, s, 0, L).T.astype(BF16), kd.astype(BF16))
            st_ref[s] = st_ref[s] * jnp.exp(b_last) + jnp.where(same_head, upd, 0.0)

        o2_hi, o2_lo = _split_bf16(o * o)
        ms = (_dot(o2_hi, head_ones) + _dot(o2_lo, head_ones)) * (1.0 / HEAD_DIM)
        y = o * lax.rsqrt(ms + RMS_EPS) * ng
        o_ref[:, pl.ds(r0, L), :] = (y * (g / (1.0 + jnp.exp(-g)))).astype(o_ref.dtype).reshape(n_seq, L, w)
        return 0

    lax.fori_loop(0, n_chunks, chunk, 0, unroll=2)


def _hgrn(hg, lb_params, norm_g, rows):
    B, T, width = hg.shape
    nb = 4 if B % 4 == 0 else (2 if B % 2 == 0 else 1)
    return pl.pallas_call(
        _hgrn_kernel,
        grid=(B // nb, T // rows),
        in_specs=[pl.BlockSpec((nb, rows, width), lambda b, t: (b, t, 0)), _full(lb_params.shape), _full(norm_g.shape)],
        out_specs=pl.BlockSpec((nb, rows, HGRN_WIDTH), lambda b, t: (b, t, 0)),
        out_shape=jax.ShapeDtypeStruct((B, T, HGRN_WIDTH), BF16),
        scratch_shapes=[pltpu.VMEM((nb, HGRN_WIDTH, HGRN_WIDTH), F32)],
        compiler_params=_params("arbitrary", "arbitrary"),
        name="hgrn",
    )(hg, lb_params, norm_g)


def _memkv_kernel(m_ref, g_ref, w_ref, k_ref, v_ref):
    kv = _dot(_rmsnorm(m_ref[0], g_ref[...]).astype(BF16), w_ref[...])
    k_ref[0] = kv[:, :CROSS_WIDTH].astype(BF16)
    v_ref[0] = kv[:, CROSS_WIDTH:].astype(BF16)


def _memkv(mem, gain, wkv):
    B, M, D = mem.shape
    out = pl.BlockSpec((1, M, CROSS_WIDTH), lambda b: (b, 0, 0))
    return pl.pallas_call(
        _memkv_kernel,
        grid=(B,),
        in_specs=[pl.BlockSpec((1, M, D), lambda b: (b, 0, 0)), _full(gain.shape), _full(wkv.shape)],
        out_specs=[out, out],
        out_shape=[jax.ShapeDtypeStruct((B, M, CROSS_WIDTH), BF16)] * 2,
        compiler_params=_params("arbitrary"),
        name="memkv",
    )(mem, gain, wkv)


def _post_kernel(h_ref, on_ref, oc_ref, oh_ref, wo_ref, gc_ref, wq_ref, km_ref, vm_ref, wco_ref, gf_ref,
                 wr_hi_ref, wr_lo_ref, br_ref, ltri_ref, h2_ref, info_ref, cnt_ref, carry_ref):
    tm = h_ref.shape[1]
    first = (pl.program_id(0) == 0) & (pl.program_id(1) == 0)

    @pl.when(first)
    def _():
        carry_ref[...] = jnp.zeros_like(carry_ref)

    a, b = NSA_WIDTH, NSA_WIDTH + CONV_WIDTH
    mix = _dot(on_ref[0], wo_ref[0:a]) + _dot(oc_ref[0], wo_ref[a:b]) + _dot(oh_ref[0], wo_ref[b:])
    h1 = h_ref[0] + mix

    q = _dot(_rmsnorm(h1, gc_ref[...]).astype(BF16), wq_ref[...]) * QK_SCALE
    km, vm = km_ref[0], vm_ref[0]
    lane_head = lax.shift_right_logical(lax.broadcasted_iota(jnp.int32, (1, CROSS_WIDTH), 1), HEAD_SHIFT)
    o = jnp.zeros((tm, CROSS_WIDTH), F32)
    for hd in range(CROSS_HEADS):
        in_head = lane_head == hd
        s = _dot_nt(jnp.where(in_head, q, 0.0).astype(BF16), km)
        m = jnp.max(s, axis=1, keepdims=True)
        e = jnp.exp(s - m)
        p = e / jnp.sum(e, axis=1, keepdims=True)
        o = o + _dot(p.astype(BF16), jnp.where(in_head, vm, jnp.zeros_like(vm)))
    h2 = h1 + _dot(o.astype(BF16), wco_ref[...])
    h2_ref[0] = h2

    xn = _rmsnorm(h2, gf_ref[...])
    x_hi, x_lo = _split_bf16(xn)
    hi_parts = _dot(x_hi, wr_lo_ref[...])
    logits = hi_parts[:, :LANES] + hi_parts[:, LANES:] + _dot(x_lo, wr_hi_ref[...]) + br_ref[...]
    lane = lax.broadcasted_iota(jnp.int32, (tm, LANES), 1)
    lanef = lane.astype(F32)
    big = float(LANES)

    lg = jnp.where(lane < N_GROUPS, logits, -jnp.inf)
    mg = jnp.max(lg, axis=1, keepdims=True)
    p_grp_sel = 1.0 / jnp.sum(jnp.exp(lg - mg), axis=1, keepdims=True)
    grp = jnp.min(jnp.where(lg == mg, lanef, big), axis=1, keepdims=True)

    lo_lane = EXPERT_LANE0 + EXPERTS_PER_GROUP * grp
    in_grp = (lanef >= lo_lane) & (lanef < lo_lane + EXPERTS_PER_GROUP)
    le = jnp.where(in_grp, logits, -jnp.inf)
    me = jnp.max(le, axis=1, keepdims=True)
    ee = jnp.exp(le - me)
    pe = jnp.where(in_grp, ee / jnp.sum(ee, axis=1, keepdims=True), -1.0)
    p1 = jnp.max(pe, axis=1, keepdims=True)
    l1 = jnp.min(jnp.where(pe == p1, lanef, big), axis=1, keepdims=True)
    pe2 = jnp.where(lanef == l1, -1.0, pe)
    p2 = jnp.max(pe2, axis=1, keepdims=True)
    l2 = jnp.min(jnp.where(pe2 == p2, lanef, big), axis=1, keepdims=True)
    scale = p_grp_sel / (p1 + p2)
    g0, g1 = p1 * scale, p2 * scale

    oh0 = jnp.where(lanef == l1, 1.0, 0.0)
    oh1 = jnp.where(lanef == l2, 1.0, 0.0)
    ltri = ltri_ref[...]
    c0, c1 = carry_ref[0:1], carry_ref[1:2]
    r0 = jnp.sum((_dot(ltri, oh0.astype(BF16)) + c0) * oh0, axis=1, keepdims=True)
    r1 = jnp.sum((_dot(ltri, oh1.astype(BF16)) + c1) * oh1, axis=1, keepdims=True)
    c0 = c0 + jnp.sum(oh0, axis=0, keepdims=True)
    c1 = c1 + jnp.sum(oh1, axis=0, keepdims=True)
    carry_ref[0:1] = c0
    carry_ref[1:2] = c1
    cnt_ref[...] = carry_ref[...]

    info = jnp.zeros((tm, LANES), F32)
    for col, val in ((INFO_E0, l1 - EXPERT_LANE0), (INFO_E1, l2 - EXPERT_LANE0), (INFO_R0, r0), (INFO_R1, r1),
                     (INFO_G0, g0), (INFO_G1, g1)):
        info = jnp.where(lane == col, val, info)
    info_ref[0] = info


def _post(h, o_nsa, o_conv, o_hgrn, w_out, g_cross, wq, kmem, vmem, wco, g_ffn, wr_hi, wr_lo, br, ltri, tm):
    B, T, D = h.shape
    M = kmem.shape[1]
    tok = lambda w: pl.BlockSpec((1, tm, w), lambda b, t: (b, t, 0))
    memspec = pl.BlockSpec((1, M, CROSS_WIDTH), lambda b, t: (b, 0, 0))
    return pl.pallas_call(
        _post_kernel,
        grid=(B, T // tm),
        in_specs=[tok(D), tok(NSA_WIDTH), tok(CONV_WIDTH), tok(HGRN_WIDTH), _full(w_out.shape), _full(g_cross.shape),
                  _full(wq.shape), memspec, memspec, _full(wco.shape), _full(g_ffn.shape), _full(wr_hi.shape),
                  _full(wr_lo.shape), _full(br.shape), _full(ltri.shape)],
        out_specs=[tok(D), tok(LANES), _full((SUBLANES, LANES))],
        out_shape=[jax.ShapeDtypeStruct((B, T, D), F32), jax.ShapeDtypeStruct((B, T, LANES), F32),
                   jax.ShapeDtypeStruct((SUBLANES, LANES), F32)],
        scratch_shapes=[pltpu.VMEM((SUBLANES, LANES), F32)],
        compiler_params=_params("arbitrary", "arbitrary"),
        name="post",
    )(h, o_nsa, o_conv, o_hgrn, w_out, g_cross, wq, kmem, vmem, wco, g_ffn, wr_hi, wr_lo, br, ltri)


ROW_PIECES = D_MODEL // LANES
assert ROW_PIECES == SUBLANES


def _to_token_tiles(x, ref):
    tm = x.shape[0]
    for s in range(ROW_PIECES):
        ref[pl.ds(s, tm, stride=ROW_PIECES), :] = x[:, s * LANES:(s + 1) * LANES]


def _from_token_tiles(ref, tm):
    return jnp.concatenate([ref[pl.ds(s, tm, stride=ROW_PIECES), :] for s in range(ROW_PIECES)], axis=1)


def _token_copy(src, src_tok, dst, dst_tok, sem):
    rows = lambda t: pl.ds(pl.multiple_of(t * ROW_PIECES, ROW_PIECES), ROW_PIECES)
    return pltpu.make_async_copy(src.at[rows(src_tok)], dst.at[rows(dst_tok)], sem)


DMA_UNROLL = 8


def _route_kernel(info_ref, tab_ref, dest_ref):
    info = info_ref[...]
    lanef = lax.broadcasted_iota(jnp.int32, info.shape, 1).astype(F32)
    pick = lambda col, row: jnp.sum(jnp.where(lanef == info[:, col:col + 1] + EXPERT_LANE0, tab_ref[row:row + 1], 0.0),
                                    axis=1, keepdims=True)
    d0 = pick(INFO_E0, 0) + info[:, INFO_R0:INFO_R0 + 1]
    d1 = pick(INFO_E1, 1) + info[:, INFO_R1:INFO_R1 + 1]
    dest_ref[...] = jnp.where(lanef == 0.0, d0, jnp.where(lanef == 1.0, d1, 0.0)).astype(jnp.int32)


def _route(info, tab, tm):
    N = info.shape[0]
    blk = pl.BlockSpec((tm, LANES), lambda i: (i, 0))
    return pl.pallas_call(
        _route_kernel,
        grid=(N // tm,),
        in_specs=[blk, _full(tab.shape)],
        out_specs=blk,
        out_shape=jax.ShapeDtypeStruct((N, LANES), jnp.int32),
        compiler_params=_params("arbitrary"),
        name="route",
    )(info, tab)


def _dispatch_kernel(pad_ref, nu_ref, d0_ref, d1_ref, d0p_ref, d1p_ref, h_ref, g_ref, xs_ref, xn_ref, zero_ref, sem):
    tm = h_ref.shape[0]
    fill_rows = zero_ref.shape[0]

    @pl.when(pl.program_id(0) == 0)
    def _():
        zero_ref[...] = jnp.zeros_like(zero_ref)
        fill = lambda row: pltpu.make_async_copy(
            zero_ref, xs_ref.at[pl.ds(pl.multiple_of(row * ROW_PIECES, ROW_PIECES), fill_rows)], sem.at[0])
        for e in range(N_EXPERTS):
            fill(pad_ref[e]).start()
        for e in range(N_EXPERTS):
            fill(pad_ref[e]).wait()
        n_total = xs_ref.shape[0] // fill_rows
        te = fill_rows // ROW_PIECES

        def tail_start(j, _):
            fill(j * te).start()
            return 0

        def tail_wait(j, _):
            fill(j * te).wait()
            return 0

        lax.fori_loop(nu_ref[0], n_total, tail_start, 0)
        lax.fori_loop(nu_ref[0], n_total, tail_wait, 0)

    i = pl.program_id(0)
    slot = i % 2
    _to_token_tiles(_rmsnorm(h_ref[...], g_ref[...]), xn_ref.at[slot])

    def scatters(da_ref, db_ref, s, go):
        def body(r, _):
            go(_token_copy(xn_ref.at[s], r, xs_ref, da_ref[0, 0, r], sem.at[s]), 0)
            go(_token_copy(xn_ref.at[s], r, xs_ref, db_ref[0, 0, r], sem.at[s]), 1)
            return 0
        lax.fori_loop(0, tm, body, 0, unroll=DMA_UNROLL)

    start = lambda cp, prio: cp.start(priority=prio)
    wait = lambda cp, prio: cp.wait()
    scatters(d0_ref, d1_ref, slot, start)

    @pl.when(i > 0)
    def _():
        scatters(d0p_ref, d1p_ref, 1 - slot, wait)

    @pl.when(i == pl.num_programs(0) - 1)
    def _():
        scatters(d0_ref, d1_ref, slot, wait)


def _dispatch(h2, g_ffn, dest0, dest1, pad_start, n_used, xs_rows, tm, te):
    N, D = h2.shape
    nt = N // tm
    idx = pl.BlockSpec((1, 1, tm), lambda i, pad, nu: (i, 0, 0), memory_space=pltpu.SMEM)
    idx_prev = pl.BlockSpec((1, 1, tm), lambda i, pad, nu: (jnp.maximum(i - 1, 0), 0, 0), memory_space=pltpu.SMEM)
    d0, d1 = dest0.reshape(nt, 1, tm), dest1.reshape(nt, 1, tm)
    grid_spec = pltpu.PrefetchScalarGridSpec(
        num_scalar_prefetch=2,
        grid=(nt,),
        in_specs=[idx, idx, idx_prev, idx_prev, pl.BlockSpec((tm, D), lambda i, pad, nu: (i, 0)),
                  pl.BlockSpec(g_ffn.shape, lambda i, pad, nu: (0, 0))],
        out_specs=pl.BlockSpec(memory_space=pl.ANY),
        scratch_shapes=[pltpu.VMEM((2, tm * ROW_PIECES, LANES), F32), pltpu.VMEM((te * ROW_PIECES, LANES), F32),
                        pltpu.SemaphoreType.DMA((2,))],
    )
    return pl.pallas_call(
        _dispatch_kernel,
        grid_spec=grid_spec,
        out_shape=jax.ShapeDtypeStruct(((xs_rows + te) * ROW_PIECES, LANES), F32),
        compiler_params=_params("arbitrary"),
        name="dispatch",
    )(pad_start, n_used, d0, d1, d0, d1, h2, g_ffn)


def _experts_kernel(te_ref, nu_ref, x_ref, wgu_ref, wdn_ref, y_ref, wgu_bf_ref, wdn_bf_ref):
    j = pl.program_id(0)
    te = x_ref.shape[0] // ROW_PIECES

    @pl.when((j == 0) | (te_ref[j] != te_ref[jnp.maximum(j - 1, 0)]))
    def _():
        wgu_bf_ref[...] = wgu_ref[0].astype(BF16)
        wdn_bf_ref[...] = wdn_ref[0].astype(BF16)

    @pl.when(j < nu_ref[0])
    def _():
        gu = _dot(_from_token_tiles(x_ref, te).astype(BF16), wgu_bf_ref[...])
        gate, up = gu[:, :D_EXPERT], gu[:, D_EXPERT:]
        act = gate / (1.0 + jnp.exp(-gate)) * up
        _to_token_tiles(_dot(act.astype(BF16), wdn_bf_ref[...]), y_ref)

    @pl.when(j >= nu_ref[0])
    def _():
        y_ref[...] = jnp.zeros_like(y_ref)


def _experts(xs, wgu, wdn, layer, tile_expert, n_used, xs_rows, te):
    D = D_MODEL
    tiles = pl.BlockSpec((te * ROW_PIECES, LANES), lambda j, te_, nu: (j, 0))
    grid_spec = pltpu.PrefetchScalarGridSpec(
        num_scalar_prefetch=2,
        grid=(xs_rows // te,),
        in_specs=[tiles,
                  pl.BlockSpec((None, 1, D, 2 * D_EXPERT), lambda j, te_, nu: (layer, te_[j], 0, 0)),
                  pl.BlockSpec((None, 1, D_EXPERT, D), lambda j, te_, nu: (layer, te_[j], 0, 0))],
        out_specs=tiles,
        scratch_shapes=[pltpu.VMEM((D, 2 * D_EXPERT), BF16), pltpu.VMEM((D_EXPERT, D), BF16)],
    )
    return pl.pallas_call(
        _experts_kernel,
        grid_spec=grid_spec,
        out_shape=jax.ShapeDtypeStruct((xs_rows * ROW_PIECES, LANES), F32),
        compiler_params=_params("arbitrary"),
        name="experts",
    )(tile_expert, n_used, xs, wgu, wdn)


def _combine_kernel(d0_ref, d1_ref, d0n_ref, d1n_ref, h_ref, info_ref, gfin_ref, ys_ref, o_ref, y0_ref, y1_ref, sem,
                    *, final_norm):
    tm = h_ref.shape[0]
    i = pl.program_id(0)
    slot = i % 2

    def gathers(da_ref, db_ref, s, go):
        def body(r, _):
            go(_token_copy(ys_ref, da_ref[0, 0, r], y0_ref.at[s], r, sem.at[s]), 0)
            go(_token_copy(ys_ref, db_ref[0, 0, r], y1_ref.at[s], r, sem.at[s]), 1)
            return 0
        lax.fori_loop(0, tm, body, 0, unroll=DMA_UNROLL)

    start = lambda cp, prio: cp.start(priority=prio)
    wait = lambda cp, prio: cp.wait()

    @pl.when(i == 0)
    def _():
        gathers(d0_ref, d1_ref, 0, start)

    @pl.when(i + 1 < pl.num_programs(0))
    def _():
        gathers(d0n_ref, d1n_ref, 1 - slot, start)

    gathers(d0_ref, d1_ref, slot, wait)
    info = info_ref[...]
    out = (h_ref[...] + info[:, INFO_G0:INFO_G0 + 1] * _from_token_tiles(y0_ref.at[slot], tm)
           + info[:, INFO_G1:INFO_G1 + 1] * _from_token_tiles(y1_ref.at[slot], tm))
    if final_norm:
        out = _rmsnorm(out, gfin_ref[...])
    o_ref[...] = out


def _combine(h2, info, g_final, ys, dest0, dest1, tm, final_norm):
    N, D = h2.shape
    nt = N // tm
    idx = pl.BlockSpec((1, 1, tm), lambda i: (i, 0, 0), memory_space=pltpu.SMEM)
    idx_next = pl.BlockSpec((1, 1, tm), lambda i: (jnp.minimum(i + 1, nt - 1), 0, 0), memory_space=pltpu.SMEM)
    d0, d1 = dest0.reshape(nt, 1, tm), dest1.reshape(nt, 1, tm)
    buf = pltpu.VMEM((2, tm * ROW_PIECES, LANES), F32)
    return pl.pallas_call(
        functools.partial(_combine_kernel, final_norm=final_norm),
        grid=(nt,),
        in_specs=[idx, idx, idx_next, idx_next, pl.BlockSpec((tm, D), lambda i: (i, 0)),
                  pl.BlockSpec((tm, LANES), lambda i: (i, 0)), _full(g_final.shape), pl.BlockSpec(memory_space=pl.ANY)],
        out_specs=pl.BlockSpec((tm, D), lambda i: (i, 0)),
        out_shape=jax.ShapeDtypeStruct((N, D), F32),
        scratch_shapes=[buf, buf, pltpu.SemaphoreType.DMA((2,))],
        compiler_params=_params("arbitrary"),
        name="combine",
    )(d0, d1, d0, d1, h2, info, g_final, ys)


IN_SIZES = (NSA_WIDTH,) + (KV_WIDTH,) * 6 + (NSA_HEADS * 3,) + (CONV_WIDTH,) * 3 + (HGRN_WIDTH,) * 4
IN_OFF = tuple(int(v) for v in np.cumsum((0,) + IN_SIZES))


def _rope_tables(T):
    half = HEAD_DIM // 2
    inv = ROPE_THETA ** (-jnp.arange(half, dtype=F32) / half)
    ang = jnp.arange(T).astype(F32)[:, None] * inv[None, :]
    cos, sin = jnp.cos(ang), jnp.sin(ang)
    zero = jnp.zeros_like(sin)
    reps = LANES // HEAD_DIM
    cos_t = jnp.tile(jnp.concatenate([cos, cos], axis=1), (1, reps))
    sin_a = jnp.tile(jnp.concatenate([-sin, zero], axis=1), (1, reps))
    sin_b = jnp.tile(jnp.concatenate([zero, sin], axis=1), (1, reps))
    return cos_t, sin_a, sin_b


def _cmp_to_sel(T):
    nc_pad, ns = T // CMP_STRIDE, T // SEL_BLOCK
    nc = (T - CMP_LEN) // CMP_STRIDE + 1
    cs = np.arange(nc_pad) * CMP_STRIDE
    ss = np.arange(ns) * SEL_BLOCK
    ov = np.clip(np.minimum(cs[:, None] + CMP_LEN, ss[None, :] + SEL_BLOCK)
                 - np.maximum(cs[:, None], ss[None, :]), 0, None) / CMP_LEN
    ov[nc:] = 0.0
    return jnp.asarray(np.pad(ov, ((0, 0), (0, LANES - ns))), dtype=BF16)


def _q_weight(w_in_l):
    wq = w_in_l[:, IN_OFF[0]:IN_OFF[1]].reshape(D_MODEL, NSA_KV_HEADS, NSA_GROUP, 1, HEAD_DIM)
    half = jnp.eye(NSA_KV_HEADS, dtype=wq.dtype)[None, :, None, :, None]
    return (wq * half).reshape(D_MODEL, Q_AUG_WIDTH)


def _compress_weights(pe, w1, w2):
    eye = jnp.eye(NSA_KV_HEADS, dtype=F32)
    half_len = CMP_LEN // 2
    w1r = w1.reshape(2, 2, half_len, HEAD_DIM, CMP_HIDDEN)
    w1_aug = jnp.einsum("sxldm,hg->sxlhdgm", w1r, eye).reshape(2, 2, half_len * KV_WIDTH, NSA_KV_HEADS * CMP_HIDDEN)
    w2_aug = jnp.einsum("smd,hg->shmgd", w2, eye).reshape(2, NSA_KV_HEADS * CMP_HIDDEN, KV_WIDTH)
    pe_r = jnp.broadcast_to(pe.reshape(2, 2, half_len, 1, HEAD_DIM), (2, 2, half_len, NSA_KV_HEADS, HEAD_DIM))
    pe_aug = jnp.pad(pe_r.reshape(2, 2, half_len * KV_WIDTH), ((0, 0), (0, SUBLANES - 2), (0, 0)))
    return pe_aug, w1_aug[:, 0].astype(BF16), w1_aug[:, 1].astype(BF16), w2_aug.astype(BF16)


def _gate_weight(w_in_l):
    wg = w_in_l[:, IN_OFF[7]:IN_OFF[8]].reshape(D_MODEL, NSA_KV_HEADS, NSA_GROUP * 3)
    return jnp.pad(wg, ((0, 0), (0, 0), (0, LANES - NSA_GROUP * 3))).reshape(D_MODEL, NSA_KV_HEADS * LANES)


def _segments(counts, n_tok, te):
    c0 = counts[0, EXPERT_LANE0:EXPERT_LANE0 + N_EXPERTS].astype(jnp.int32)
    c1 = counts[1, EXPERT_LANE0:EXPERT_LANE0 + N_EXPERTS].astype(jnp.int32)
    tiles = (c0 + c1 + te - 1) // te
    tile_end = jnp.cumsum(tiles)
    off = (tile_end - tiles) * te
    tab = jnp.zeros((SUBLANES, LANES), F32)
    tab = tab.at[0, EXPERT_LANE0:EXPERT_LANE0 + N_EXPERTS].set(off.astype(F32))
    tab = tab.at[1, EXPERT_LANE0:EXPERT_LANE0 + N_EXPERTS].set((off + c0).astype(F32))
    n_tiles = (2 * n_tok) // te + N_EXPERTS
    tile_expert = jnp.sum(tile_end[None, :] <= jnp.arange(n_tiles)[:, None], axis=1)
    tile_expert = jnp.minimum(tile_expert, N_EXPERTS - 1).astype(jnp.int32)
    return tab, off + c0 + c1, tile_expert, tile_end[-1:].astype(jnp.int32)


def kernel(x, mem, w_in, conv_w, cmp_pe, cmp_w1, cmp_w2, hgrn_lb_logits, hgrn_norm, w_out, cross_wq, cross_wkv,
           cross_wo, router_group_w, router_group_b, router_expert_w, router_expert_b, expert_w_gate_up,
           expert_w_down, norm_mix, norm_cross, norm_mem, norm_ffn, norm_final):
    B, T, D = x.shape
    depth = w_in.shape[0]
    n_tok = B * T
    tm = min(512, T)
    tm_row = min(512, T)
    te = 512

    cos_t, sin_a, sin_b = _rope_tables(T)
    c2s = _cmp_to_sel(T)
    ltri = jnp.asarray(np.tril(np.ones((tm, tm), np.float32), -1), dtype=BF16)

    p_lb = jax.nn.softmax(hgrn_lb_logits.astype(F32), axis=0)
    lower_bounds = jnp.cumsum(p_lb, axis=0) - p_lb[0:1]
    xs_rows = 2 * n_tok + N_EXPERTS * te

    h = x
    for l in range(depth):
        wl = w_in[l]
        piece = lambda a, b: wl[:, IN_OFF[a]:IN_OFF[b]].astype(BF16)
        cw = jnp.pad(conv_w[l], ((0, SUBLANES - CONV_K), (0, 0)))
        (q, qr, kc, vc, ka, vs, kw, vw, gs, o_conv, hg) = _inproj(
            h, norm_mix[l][None], cos_t, sin_a, sin_b, _q_weight(wl).astype(BF16), piece(1, 7),
            _gate_weight(wl).astype(BF16), piece(8, 11), piece(11, 15), cw, tm)

        chunks = lambda a: a.reshape(B, T // CMP_STRIDE, CMP_STRIDE * KV_WIDTH)
        kvc = _compress(jnp.stack([chunks(kc), chunks(vc)]), *_compress_weights(cmp_pe[l], cmp_w1[l], cmp_w2[l]))
        o_nsa = _nsa(q, qr, kvc, ka, vs, kw, vw, gs, c2s)

        lb = lower_bounds[l][None]
        lb_params = jnp.concatenate([jnp.log(jnp.maximum(lb, LB_FLOOR)), jnp.log1p(-lb), 1.0 - lb,
                                     jnp.zeros((SUBLANES - 3, HGRN_WIDTH), F32)], axis=0)
        o_hgrn = _hgrn(hg, lb_params, hgrn_norm[l][None], tm)

        kmem, vmem = _memkv(mem, norm_mem[l][None], cross_wkv[l].astype(BF16))

        wr = jnp.zeros((D, LANES), F32).at[:, :N_GROUPS].set(router_group_w[l])
        wr = wr.at[:, EXPERT_LANE0:EXPERT_LANE0 + N_EXPERTS].set(router_expert_w[l])
        br = jnp.zeros((1, LANES), F32).at[0, :N_GROUPS].set(router_group_b[l])
        br = br.at[0, EXPERT_LANE0:EXPERT_LANE0 + N_EXPERTS].set(router_expert_b[l])
        wr_hi = wr.astype(BF16)
        wr_lo = jnp.concatenate([wr_hi, (wr - wr_hi.astype(F32)).astype(BF16)], axis=1)
        h2, info, counts = _post(h, o_nsa, o_conv, o_hgrn, w_out[l].astype(BF16), norm_cross[l][None],
                                 cross_wq[l].astype(BF16), kmem, vmem, cross_wo[l].astype(BF16), norm_ffn[l][None],
                                 wr_hi, wr_lo, br, ltri, tm)

        h2 = h2.reshape(n_tok, D)
        info = info.reshape(n_tok, LANES)
        tab, pad_start, tile_expert, n_used = _segments(counts, n_tok, te)
        dest = _route(info, tab, tm)
        dest0, dest1 = dest[:, 0], dest[:, 1]
        xs = _dispatch(h2, norm_ffn[l][None], dest0, dest1, pad_start, n_used, xs_rows, tm_row, te)
        ys = _experts(xs, expert_w_gate_up, expert_w_down, l, tile_expert, n_used, xs_rows, te)
        h = _combine(h2, info, norm_final[None], ys, dest0, dest1, tm_row, l == depth - 1).reshape(B, T, D)
    return h
```

```python
import functools

import numpy as np
import jax
import jax.numpy as jnp
from jax import lax
from jax.experimental import pallas as pl
from jax.experimental.pallas import tpu as pltpu

F32 = jnp.float32
BF16 = jnp.bfloat16

D_MODEL = 1024
HEAD_DIM = 64
NSA_HEADS = 8
NSA_KV_HEADS = 2
NSA_GROUP = NSA_HEADS // NSA_KV_HEADS
NSA_WIDTH = NSA_HEADS * HEAD_DIM
KV_WIDTH = NSA_KV_HEADS * HEAD_DIM
CMP_LEN = 32
CMP_STRIDE = 16
CMP_HIDDEN = 2 * HEAD_DIM
SEL_BLOCK = 64
N_SELECT = 16
WINDOW = 512
Q_BLOCK = 128
NSA_Q_TILE = 256
MASK_VALUE = -1e30
SEL_FORCE = 1e6
CONV_WIDTH = 256
CONV_K = 3
HGRN_HEADS = 4
HGRN_WIDTH = HGRN_HEADS * HEAD_DIM
HGRN_CHUNK = 64
HGRN_SUB = 16
LB_FLOOR = 1e-30
ROPE_THETA = 10000.0
CROSS_HEADS = 4
CROSS_WIDTH = CROSS_HEADS * HEAD_DIM
N_GROUPS = 4
EXPERTS_PER_GROUP = 8
N_EXPERTS = N_GROUPS * EXPERTS_PER_GROUP
D_EXPERT = 512
RMS_EPS = 1e-6
QK_SCALE = HEAD_DIM ** -0.5
LOG2E = 1.4426950408889634
HEAD_SHIFT = HEAD_DIM.bit_length() - 1
SEL_SHIFT = SEL_BLOCK.bit_length() - 1

LANES = 128
SUBLANES = 8
VMEM_LIMIT_BYTES = 48 * 1024 * 1024

Q_AUG_WIDTH = NSA_HEADS * LANES

EXPERT_LANE0 = 32
INFO_E0, INFO_E1, INFO_R0, INFO_R1, INFO_G0, INFO_G1 = 0, 1, 2, 3, 4, 5


def _dot(a, b):
    return jnp.dot(a, b, preferred_element_type=F32)


def _dot_nt(a, b):
    return lax.dot_general(a, b, (((1,), (1,)), ((), ())), preferred_element_type=F32)


def _split_bf16(x):
    hi = x.astype(BF16)
    lo = (x - hi.astype(F32)).astype(BF16)
    return hi, lo


def _rmsnorm(x, g):
    return x * lax.rsqrt(jnp.mean(x * x, axis=-1, keepdims=True) + RMS_EPS) * g


def _params(*sem):
    return pltpu.CompilerParams(dimension_semantics=sem, vmem_limit_bytes=VMEM_LIMIT_BYTES)


def _full(shape):
    nd = len(shape)
    return pl.BlockSpec(shape, lambda *_: (0,) * nd)


def _rope(x, cos, sin_a, sin_b):
    width = x.shape[1]
    reps = width // LANES
    if reps > 1:
        cos = jnp.concatenate([cos] * reps, axis=1)
        sin_a = jnp.concatenate([sin_a] * reps, axis=1)
        sin_b = jnp.concatenate([sin_b] * reps, axis=1)
    half = HEAD_DIM // 2
    return x * cos + pltpu.roll(x, width - half, 1) * sin_a + pltpu.roll(x, half, 1) * sin_b


def _inproj_kernel(h_ref, g_ref, cos_ref, sa_ref, sb_ref, wq_ref, wkv_ref, wg_ref, wc_ref, wh_ref, cw_ref,
                   q_ref, qr_ref, kc_ref, vc_ref, ks_ref, vs_ref, kw_ref, vw_ref, gs_ref, oc_ref, hg_ref,
                   carry_ref):
    tm = h_ref.shape[1]

    @pl.when(pl.program_id(1) == 0)
    def _():
        carry_ref[...] = jnp.zeros_like(carry_ref)

    xb = _rmsnorm(h_ref[0], g_ref[...]).astype(BF16)
    cos, sa, sb = cos_ref[...], sa_ref[...], sb_ref[...]

    q = _dot(xb, wq_ref[...]) * (QK_SCALE * LOG2E)
    q_ref[0] = q.astype(BF16)
    qr_ref[0] = _rope(q, cos, sa, sb).astype(BF16)

    kv = _dot(xb, wkv_ref[...])
    kw_ = KV_WIDTH
    kc_ref[0] = kv[:, 0 * kw_:1 * kw_].astype(BF16)
    vc_ref[0] = kv[:, 1 * kw_:2 * kw_].astype(BF16)
    pos = pl.program_id(1) * tm + lax.broadcasted_iota(jnp.int32, (tm, LANES), 0)
    blk = lax.broadcasted_iota(jnp.int32, (tm, LANES), 1) == lax.shift_right_logical(pos, SEL_SHIFT)
    ks_ref[0, :, 0:kw_] = _rope(kv[:, 2 * kw_:3 * kw_], cos, sa, sb).astype(BF16)
    ks_ref[0, :, kw_:kw_ + LANES] = jnp.where(blk, 1.0, 0.0).astype(BF16)
    vs_ref[0] = kv[:, 3 * kw_:4 * kw_].astype(BF16)
    kw_ref[0] = _rope(kv[:, 4 * kw_:5 * kw_], cos, sa, sb).astype(BF16)
    vw_ref[0] = kv[:, 5 * kw_:6 * kw_].astype(BF16)

    gl = _dot(xb, wg_ref[...])
    gs_ref[0] = 1.0 / (1.0 + jnp.exp(-gl))

    cv = _dot(xb, wc_ref[...])
    cb = cv[:, 0:CONV_WIDTH]
    u = cv[:, CONV_WIDTH:2 * CONV_WIDTH] * cv[:, 2 * CONV_WIDTH:3 * CONV_WIDTH]
    ext = jnp.concatenate([carry_ref[...], u], axis=0)
    u1 = pltpu.roll(ext, 1, 0)[SUBLANES:]
    u2 = pltpu.roll(ext, 2, 0)[SUBLANES:]
    cw = cw_ref[...]
    oc_ref[0] = (cb * (cw[0:1] * u2 + cw[1:2] * u1 + cw[2:3] * u)).astype(BF16)
    carry_ref[...] = u[tm - SUBLANES:tm]

    hg_ref[0] = _dot(xb, wh_ref[...])


def _inproj(h, gain, cos, sin_a, sin_b, wq, wkv, wg, wc, wh, cw, tm):
    B, T, D = h.shape
    nt = T // tm
    tok = lambda w: pl.BlockSpec((1, tm, w), lambda b, t: (b, t, 0))
    tab = pl.BlockSpec((tm, LANES), lambda b, t: (t, 0))
    kvw = KV_WIDTH
    out_w = [(Q_AUG_WIDTH, BF16), (Q_AUG_WIDTH, BF16), (kvw, BF16), (kvw, BF16), (kvw + LANES, BF16), (kvw, BF16),
             (kvw, BF16), (kvw, BF16), (2 * LANES, F32), (CONV_WIDTH, BF16), (4 * HGRN_WIDTH, F32)]
    return pl.pallas_call(
        _inproj_kernel,
        grid=(B, nt),
        in_specs=[tok(D), _full(gain.shape), tab, tab, tab, _full(wq.shape), _full(wkv.shape), _full(wg.shape),
                  _full(wc.shape), _full(wh.shape), _full(cw.shape)],
        out_specs=[tok(w) for w, _ in out_w],
        out_shape=[jax.ShapeDtypeStruct((B, T, w), dt) for w, dt in out_w],
        scratch_shapes=[pltpu.VMEM((SUBLANES, CONV_WIDTH), F32)],
        compiler_params=_params("arbitrary", "arbitrary"),
        name="inproj",
    )(h, gain, cos, sin_a, sin_b, wq, wkv, wg, wc, wh, cw)


def _compress_kernel(x_ref, pe_ref, w1a_ref, w1b_ref, w2_ref, o_ref):
    x = x_ref[0, 0]
    n = x.shape[0]
    w1a, w1b = w1a_ref[0], w1b_ref[0]
    pe_a = jnp.broadcast_to(pe_ref[0, 0:1], (SUBLANES, x.shape[1])).astype(BF16)
    pe_b = jnp.broadcast_to(pe_ref[0, 1:2], (SUBLANES, x.shape[1])).astype(BF16)
    pe_term = (_dot(pe_a, w1a) + _dot(pe_b, w1b))[0:1]
    hid = _dot(x, w1a) + pltpu.roll(_dot(x, w1b), n - 1, 0) + pe_term
    act = hid / (1.0 + jnp.exp(-hid))
    o_ref[0, 0] = _dot(act.astype(BF16), w2_ref[0]).astype(BF16)


def _compress(xkv, pe, w1a, w1b, w2):
    _, B, n, width = xkv.shape
    hid = NSA_KV_HEADS * CMP_HIDDEN
    per_kv = lambda *shape: pl.BlockSpec((1,) + shape, lambda s, b: (s,) + (0,) * len(shape))
    return pl.pallas_call(
        _compress_kernel,
        grid=(2, B),
        in_specs=[pl.BlockSpec((1, 1, n, width), lambda s, b: (s, b, 0, 0)), per_kv(SUBLANES, width),
                  per_kv(width, hid), per_kv(width, hid), per_kv(hid, KV_WIDTH)],
        out_specs=pl.BlockSpec((1, 1, n, KV_WIDTH), lambda s, b: (s, b, 0, 0)),
        out_shape=jax.ShapeDtypeStruct((2, B, n, KV_WIDTH), BF16),
        compiler_params=_params("arbitrary", "arbitrary"),
        name="compress",
    )(xkv, pe, w1a, w1b, w2)


def _masked_softmax2(s, mask):
    s = jnp.where(mask, s, MASK_VALUE)
    m = jnp.max(s, axis=1, keepdims=True)
    e = jnp.exp2(s - m)
    den = jnp.sum(e, axis=1, keepdims=True)
    return e, jnp.where(m > 0.5 * MASK_VALUE, 1.0 / den, 0.0)


def _nsa_kernel(q_ref, qr_ref, kc_ref, vc_ref, ka_ref, vs_ref, kw_ref, vw_ref, gs_ref, c2s_ref, o_ref, *, key_tile):
    i = pl.program_id(1)
    qb, g_, n_h = NSA_Q_TILE, NSA_GROUP, NSA_KV_HEADS
    n_grp = n_h * g_
    rows = n_grp * qb
    sel_rows = n_h * qb
    grp_rows = lambda a, hg: a[hg * qb:(hg + 1) * qb]

    def q_all(ref):
        return jnp.concatenate([ref[0, :, hg * LANES:(hg + 1) * LANES] for hg in range(n_grp)], axis=0)

    t_row = i * qb + (lax.broadcasted_iota(jnp.int32, (rows, 1), 0) & (qb - 1))
    t_sel = i * qb + (lax.broadcasted_iota(jnp.int32, (sel_rows, 1), 0) & (qb - 1))

    kc, vc = kc_ref[0, 0], vc_ref[0, 0]
    c_end = lax.broadcasted_iota(jnp.int32, (1, kc.shape[0]), 1) * CMP_STRIDE + (CMP_LEN - 1)
    e_c, inv_c = _masked_softmax2(_dot_nt(q_all(q_ref), kc), c_end <= t_row)
    p_c = e_c * inv_c
    o_c = _dot(p_c.astype(BF16), vc)
    p_g = []
    for h in range(n_h):
        acc_g = grp_rows(p_c, h * g_)
        for g in range(1, g_):
            acc_g = acc_g + grp_rows(p_c, h * g_ + g)
        p_g.append(acc_g)
    imp = _dot(jnp.concatenate(p_g, axis=0).astype(BF16), c2s_ref[...])

    qr = q_all(qr_ref)
    wk = WINDOW + qb
    w_start = pl.multiple_of(jnp.maximum(i * qb - WINDOW, 0), qb)
    kwin = kw_ref[0, pl.ds(w_start, wk), :]
    vwin = vw_ref[0, pl.ds(w_start, wk), :]
    dist = t_row - (w_start + lax.broadcasted_iota(jnp.int32, (1, wk), 1))
    e_w, inv_w = _masked_softmax2(_dot_nt(qr, kwin), (dist >= 0) & (dist < WINDOW))
    o_w = _dot(e_w.astype(BF16), vwin) * inv_w

    cur = lax.shift_right_logical(t_sel, SEL_SHIFT)
    j = lax.broadcasted_iota(jnp.int32, (sel_rows, LANES), 1)
    jf = j.astype(F32)
    forced = (j == 0) | (j == cur) | (j == cur - 1)
    work = jnp.where(forced | (j > cur), -jnp.inf, imp)
    sel = jnp.where(forced, 1.0, 0.0)
    for _ in range(N_SELECT - 3):
        mx = jnp.max(work, axis=1, keepdims=True)
        first = jnp.min(jnp.where(work == mx, jf, float(LANES)), axis=1, keepdims=True)
        pick = jf == first
        sel = jnp.where(pick, 1.0, sel)
        work = jnp.where(pick, -jnp.inf, work)
    sel_bias = jnp.where(sel > 0, 0.0, MASK_VALUE).astype(BF16)
    q_aug = jnp.concatenate([jnp.concatenate([grp_rows(qr, hg), grp_rows(sel_bias, hg // g_)], axis=1)
                             for hg in range(n_grp)], axis=0)

    def sel_tile(q_rows, t_rows, tile, width, carry, causal):
        start = pl.multiple_of(tile * width, width)
        m, l, acc = carry
        s = _dot_nt(q_rows, ka_ref[0, pl.ds(start, width), :])
        if causal:
            s = jnp.where(start + lax.broadcasted_iota(jnp.int32, (1, width), 1) <= t_rows, s, MASK_VALUE)
        m_new = jnp.maximum(m, jnp.max(s, axis=1, keepdims=True))
        alpha = jnp.exp2(m - m_new)
        p = jnp.exp2(s - m_new)
        l = alpha * l + jnp.sum(p, axis=1, keepdims=True)
        acc = alpha * acc + _dot(p.astype(BF16), vs_ref[0, pl.ds(start, width), :])
        return m_new, l, acc

    n_full = (i * qb) // key_tile
    init = (jnp.full((rows, 1), MASK_VALUE, F32), jnp.zeros((rows, 1), F32), jnp.zeros((rows, LANES), F32))
    carry = lax.fori_loop(0, n_full, lambda kt, c: sel_tile(q_aug, t_row, kt, key_tile, c, False), init)
    _, l_s, acc_s = sel_tile(q_aug, t_row, n_full, key_tile, carry, True)
    o_s = acc_s * (1.0 / l_s)

    gs = gs_ref[0]
    low_half = lax.broadcasted_iota(jnp.int32, (qb, LANES), 1) < HEAD_DIM
    out_groups = []
    for h in range(n_h):
        mixed = []
        for g in range(g_):
            hg = h * g_ + g
            c = h * LANES + 3 * g
            mixed.append(gs[:, c:c + 1] * grp_rows(o_c, hg) + gs[:, c + 1:c + 2] * grp_rows(o_s, hg)
                         + gs[:, c + 2:c + 3] * grp_rows(o_w, hg))
        for pair in range(g_ // 2):
            a, b = mixed[2 * pair], mixed[2 * pair + 1]
            if h == 0:
                out_groups.append(jnp.where(low_half, a, pltpu.roll(b, HEAD_DIM, 1)))
            else:
                out_groups.append(jnp.where(low_half, pltpu.roll(a, HEAD_DIM, 1), b))
    o_ref[0] = jnp.concatenate(out_groups, axis=1).astype(o_ref.dtype)


def _nsa(q, qr, kvc, ka, vs, kw, vw, gs, c2s):
    B, T, _ = q.shape
    assert T // SEL_BLOCK <= LANES
    nc = kvc.shape[2]
    key_tile = min(1024, T)
    blk = lambda w: pl.BlockSpec((1, NSA_Q_TILE, w), lambda b, i: (b, i, 0))
    seq = lambda w: pl.BlockSpec((1, T, w), lambda b, i: (b, 0, 0), pipeline_mode=pl.Buffered(1))
    cmp_spec = lambda s: pl.BlockSpec((1, 1, nc, KV_WIDTH), lambda b, i: (s, b, 0, 0))
    return pl.pallas_call(
        functools.partial(_nsa_kernel, key_tile=key_tile),
        grid=(B, T // NSA_Q_TILE),
        in_specs=[blk(Q_AUG_WIDTH), blk(Q_AUG_WIDTH), cmp_spec(0), cmp_spec(1), seq(KV_WIDTH + LANES), seq(KV_WIDTH),
                  seq(KV_WIDTH), seq(KV_WIDTH), blk(2 * LANES), _full(c2s.shape)],
        out_specs=blk(NSA_WIDTH),
        out_shape=jax.ShapeDtypeStruct((B, T, NSA_WIDTH), BF16),
        compiler_params=_params("arbitrary", "arbitrary"),
        name="nsa",
    )(q, qr, kvc, kvc, ka, vs, kw, vw, gs, c2s)


def _hgrn_kernel(x_ref, lbp_ref, ng_ref, o_ref, st_ref):
    w = HGRN_WIDTH
    L = HGRN_CHUNK
    sub = HGRN_SUB
    n_chunks = x_ref.shape[1] // L
    n_seq = x_ref.shape[0]
    rows = n_seq * L
    seqs = range(n_seq)

    @pl.when(pl.program_id(1) == 0)
    def _():
        st_ref[...] = jnp.zeros_like(st_ref)

    log_lb, log_1m_lb, one_m_lb = lbp_ref[0:1], lbp_ref[1:2], lbp_ref[2:3]
    ng = ng_ref[...]
    ri = lax.broadcasted_iota(jnp.int32, (rows, rows), 0)
    ci = lax.broadcasted_iota(jnp.int32, (rows, rows), 1)
    chunk_shift = L.bit_length() - 1
    same_chunk = lax.shift_right_logical(ri, chunk_shift) == lax.shift_right_logical(ci, chunk_shift)
    tril = jnp.where((ci <= ri) & same_chunk, 1.0, 0.0).astype(BF16)
    hr = lax.shift_right_logical(lax.broadcasted_iota(jnp.int32, (w, w), 0), HEAD_SHIFT)
    hc = lax.shift_right_logical(lax.broadcasted_iota(jnp.int32, (w, w), 1), HEAD_SHIFT)
    same_head = hr == hc
    head_ones = jnp.where(same_head, 1.0, 0.0).astype(BF16)
    sub_row = lax.broadcasted_iota(jnp.int32, (rows, 1), 0) & (sub - 1)
    lane_head = lax.shift_right_logical(lax.broadcasted_iota(jnp.int32, (1, w), 1), HEAD_SHIFT)
    stack = lambda pieces: jnp.concatenate(pieces, axis=0)

    def seq_index(idx, per_seq):
        out = jnp.zeros(idx.shape, jnp.int32)
        for s in range(1, n_seq):
            out = out + jnp.where(idx >= s * per_seq, 1, 0)
        return out

    def chunk(c, _):
        r0 = pl.multiple_of(c * L, L)
        load = lambda col: x_ref[:, pl.ds(r0, L), col * w:(col + 1) * w].reshape(rows, w)
        q = load(0) * QK_SCALE
        z, v, g = load(1), load(2), load(3)

        log_sig = jnp.minimum(z, 0.0) - jnp.log(1.0 + jnp.exp(-jnp.abs(z)))
        bb = log_1m_lb + log_sig
        log_f = jnp.maximum(log_lb, bb) + jnp.log(1.0 + jnp.exp(-jnp.abs(log_lb - bb)))
        k = one_m_lb / (1.0 + jnp.exp(z))

        lf_hi, lf_lo = _split_bf16(log_f)
        b = _dot(tril, lf_hi) + _dot(tril, lf_lo)
        seq_rows = lambda a, s, lo, n: a[s * L + lo:s * L + lo + n]
        q_dec = (q * jnp.exp(b)).astype(BF16)
        o = stack([_dot_nt(seq_rows(q_dec, s, 0, L), st_ref[s].astype(BF16)) for s in seqs])
        for d in range(sub):
            if d == 0:
                wgt = q * k
                vs = v
            else:
                wgt = jnp.where(sub_row >= d, q * pltpu.roll(k, d, 0) * jnp.exp(b - pltpu.roll(b, d, 0)), 0.0)
                vs = pltpu.roll(v, d, 0)
            o = o + _dot(wgt.astype(BF16), head_ones) * vs
        parts = [[jnp.zeros((sub, w), F32)] for _ in seqs]
        for lo in range(sub, L, sub):
            qs = stack([seq_rows(q, s, lo, sub) * jnp.exp(seq_rows(b, s, lo, sub) - seq_rows(b, s, lo - 1, 1))
                        for s in seqs])
            ks = stack([seq_rows(k, s, 0, lo) * jnp.exp(seq_rows(b, s, lo - 1, 1) - seq_rows(b, s, 0, lo))
                        for s in seqs]).astype(BF16)
            vv = stack([seq_rows(v, s, 0, lo) for s in seqs])
            same_seq = (seq_index(lax.broadcasted_iota(jnp.int32, (n_seq * sub, n_seq * lo), 0), sub)
                        == seq_index(lax.broadcasted_iota(jnp.int32, (n_seq * sub, n_seq * lo), 1), lo))
            acc = jnp.zeros((n_seq * sub, w), F32)
            for hd in range(HGRN_HEADS):
                in_head = lane_head == hd
                a = jnp.where(same_seq, _dot_nt(jnp.where(in_head, qs, 0.0).astype(BF16), ks), 0.0)
                acc = acc + _dot(a.astype(BF16), jnp.where(in_head, vv, 0.0).astype(BF16))
            for s in seqs:
                parts[s].append(acc[s * sub:(s + 1) * sub])
        o = o + stack([piece for s in seqs for piece in parts[s]])
        for s in seqs:
            b_last = seq_rows(b, s, L - 1, 1)
            kd = seq_rows(k, s, 0, L) * jnp.exp(b_last - seq_rows(b, s, 0, L))
            upd = _dot(seq_rows(v, s, 0, L).T.astype(BF16), kd.astype(BF16))
            st_ref[s] = st_ref[s] * jnp.exp(b_last) + jnp.where(same_head, upd, 0.0)

        o2_hi, o2_lo = _split_bf16(o * o)
        ms = (_dot(o2_hi, head_ones) + _dot(o2_lo, head_ones)) * (1.0 / HEAD_DIM)
        y = o * lax.rsqrt(ms + RMS_EPS) * ng
        o_ref[:, pl.ds(r0, L), :] = (y * (g / (1.0 + jnp.exp(-g)))).astype(o_ref.dtype).reshape(n_seq, L, w)
        return 0

    lax.fori_loop(0, n_chunks, chunk, 0, unroll=2)


def _hgrn(hg, lb_params, norm_g, rows):
    B, T, width = hg.shape
    nb = 4 if B % 4 == 0 else (2 if B % 2 == 0 else 1)
    return pl.pallas_call(
        _hgrn_kernel,
        grid=(B // nb, T // rows),
        in_specs=[pl.BlockSpec((nb, rows, width), lambda b, t: (b, t, 0)), _full(lb_params.shape), _full(norm_g.shape)],
        out_specs=pl.BlockSpec((nb, rows, HGRN_WIDTH), lambda b, t: (b, t, 0)),
        out_shape=jax.ShapeDtypeStruct((B, T, HGRN_WIDTH), BF16),
        scratch_shapes=[pltpu.VMEM((nb, HGRN_WIDTH, HGRN_WIDTH), F32)],
        compiler_params=_params("arbitrary", "arbitrary"),
        name="hgrn",
    )(hg, lb_params, norm_g)


def _memkv_kernel(m_ref, g_ref, w_ref, k_ref, v_ref):
    kv = _dot(_rmsnorm(m_ref[0], g_ref[...]).astype(BF16), w_ref[...])
    k_ref[0] = kv[:, :CROSS_WIDTH].astype(BF16)
    v_ref[0] = kv[:, CROSS_WIDTH:].astype(BF16)


def _memkv(mem, gain, wkv):
    B, M, D = mem.shape
    out = pl.BlockSpec((1, M, CROSS_WIDTH), lambda b: (b, 0, 0))
    return pl.pallas_call(
        _memkv_kernel,
        grid=(B,),
        in_specs=[pl.BlockSpec((1, M, D), lambda b: (b, 0, 0)), _full(gain.shape), _full(wkv.shape)],
        out_specs=[out, out],
        out_shape=[jax.ShapeDtypeStruct((B, M, CROSS_WIDTH), BF16)] * 2,
        compiler_params=_params("arbitrary"),
        name="memkv",
    )(mem, gain, wkv)


def _post_kernel(h_ref, on_ref, oc_ref, oh_ref, wo_ref, gc_ref, wq_ref, km_ref, vm_ref, wco_ref, gf_ref,
                 wr_hi_ref, wr_lo_ref, br_ref, ltri_ref, h2_ref, info_ref, cnt_ref, carry_ref):
    tm = h_ref.shape[1]
    first = (pl.program_id(0) == 0) & (pl.program_id(1) == 0)

    @pl.when(first)
    def _():
        carry_ref[...] = jnp.zeros_like(carry_ref)

    a, b = NSA_WIDTH, NSA_WIDTH + CONV_WIDTH
    mix = _dot(on_ref[0], wo_ref[0:a]) + _dot(oc_ref[0], wo_ref[a:b]) + _dot(oh_ref[0], wo_ref[b:])
    h1 = h_ref[0] + mix

    q = _dot(_rmsnorm(h1, gc_ref[...]).astype(BF16), wq_ref[...]) * QK_SCALE
    km, vm = km_ref[0], vm_ref[0]
    lane_head = lax.shift_right_logical(lax.broadcasted_iota(jnp.int32, (1, CROSS_WIDTH), 1), HEAD_SHIFT)
    o = jnp.zeros((tm, CROSS_WIDTH), F32)
    for hd in range(CROSS_HEADS):
        in_head = lane_head == hd
        s = _dot_nt(jnp.where(in_head, q, 0.0).astype(BF16), km)
        m = jnp.max(s, axis=1, keepdims=True)
        e = jnp.exp(s - m)
        p = e / jnp.sum(e, axis=1, keepdims=True)
        o = o + _dot(p.astype(BF16), jnp.where(in_head, vm, jnp.zeros_like(vm)))
    h2 = h1 + _dot(o.astype(BF16), wco_ref[...])
    h2_ref[0] = h2

    xn = _rmsnorm(h2, gf_ref[...])
    x_hi, x_lo = _split_bf16(xn)
    hi_parts = _dot(x_hi, wr_lo_ref[...])
    logits = hi_parts[:, :LANES] + hi_parts[:, LANES:] + _dot(x_lo, wr_hi_ref[...]) + br_ref[...]
    lane = lax.broadcasted_iota(jnp.int32, (tm, LANES), 1)
    lanef = lane.astype(F32)
    big = float(LANES)

    lg = jnp.where(lane < N_GROUPS, logits, -jnp.inf)
    mg = jnp.max(lg, axis=1, keepdims=True)
    p_grp_sel = 1.0 / jnp.sum(jnp.exp(lg - mg), axis=1, keepdims=True)
    grp = jnp.min(jnp.where(lg == mg, lanef, big), axis=1, keepdims=True)

    lo_lane = EXPERT_LANE0 + EXPERTS_PER_GROUP * grp
    in_grp = (lanef >= lo_lane) & (lanef < lo_lane + EXPERTS_PER_GROUP)
    le = jnp.where(in_grp, logits, -jnp.inf)
    me = jnp.max(le, axis=1, keepdims=True)
    ee = jnp.exp(le - me)
    pe = jnp.where(in_grp, ee / jnp.sum(ee, axis=1, keepdims=True), -1.0)
    p1 = jnp.max(pe, axis=1, keepdims=True)
    l1 = jnp.min(jnp.where(pe == p1, lanef, big), axis=1, keepdims=True)
    pe2 = jnp.where(lanef == l1, -1.0, pe)
    p2 = jnp.max(pe2, axis=1, keepdims=True)
    l2 = jnp.min(jnp.where(pe2 == p2, lanef, big), axis=1, keepdims=True)
    scale = p_grp_sel / (p1 + p2)
    g0, g1 = p1 * scale, p2 * scale

    oh0 = jnp.where(lanef == l1, 1.0, 0.0)
    oh1 = jnp.where(lanef == l2, 1.0, 0.0)
    ltri = ltri_ref[...]
    c0, c1 = carry_ref[0:1], carry_ref[1:2]
    r0 = jnp.sum((_dot(ltri, oh0.astype(BF16)) + c0) * oh0, axis=1, keepdims=True)
    r1 = jnp.sum((_dot(ltri, oh1.astype(BF16)) + c1) * oh1, axis=1, keepdims=True)
    c0 = c0 + jnp.sum(oh0, axis=0, keepdims=True)
    c1 = c1 + jnp.sum(oh1, axis=0, keepdims=True)
    carry_ref[0:1] = c0
    carry_ref[1:2] = c1
    cnt_ref[...] = carry_ref[...]

    info = jnp.zeros((tm, LANES), F32)
    for col, val in ((INFO_E0, l1 - EXPERT_LANE0), (INFO_E1, l2 - EXPERT_LANE0), (INFO_R0, r0), (INFO_R1, r1),
                     (INFO_G0, g0), (INFO_G1, g1)):
        info = jnp.where(lane == col, val, info)
    info_ref[0] = info


def _post(h, o_nsa, o_conv, o_hgrn, w_out, g_cross, wq, kmem, vmem, wco, g_ffn, wr_hi, wr_lo, br, ltri, tm):
    B, T, D = h.shape
    M = kmem.shape[1]
    tok = lambda w: pl.BlockSpec((1, tm, w), lambda b, t: (b, t, 0))
    memspec = pl.BlockSpec((1, M, CROSS_WIDTH), lambda b, t: (b, 0, 0))
    return pl.pallas_call(
        _post_kernel,
        grid=(B, T // tm),
        in_specs=[tok(D), tok(NSA_WIDTH), tok(CONV_WIDTH), tok(HGRN_WIDTH), _full(w_out.shape), _full(g_cross.shape),
                  _full(wq.shape), memspec, memspec, _full(wco.shape), _full(g_ffn.shape), _full(wr_hi.shape),
                  _full(wr_lo.shape), _full(br.shape), _full(ltri.shape)],
        out_specs=[tok(D), tok(LANES), _full((SUBLANES, LANES))],
        out_shape=[jax.ShapeDtypeStruct((B, T, D), F32), jax.ShapeDtypeStruct((B, T, LANES), F32),
                   jax.ShapeDtypeStruct((SUBLANES, LANES), F32)],
        scratch_shapes=[pltpu.VMEM((SUBLANES, LANES), F32)],
        compiler_params=_params("arbitrary", "arbitrary"),
        name="post",
    )(h, o_nsa, o_conv, o_hgrn, w_out, g_cross, wq, kmem, vmem, wco, g_ffn, wr_hi, wr_lo, br, ltri)


ROW_PIECES = D_MODEL // LANES
assert ROW_PIECES == SUBLANES


def _to_token_tiles(x, ref):
    tm = x.shape[0]
    for s in range(ROW_PIECES):
        ref[pl.ds(s, tm, stride=ROW_PIECES), :] = x[:, s * LANES:(s + 1) * LANES]


def _from_token_tiles(ref, tm):
    return jnp.concatenate([ref[pl.ds(s, tm, stride=ROW_PIECES), :] for s in range(ROW_PIECES)], axis=1)


def _token_copy(src, src_tok, dst, dst_tok, sem):
    rows = lambda t: pl.ds(pl.multiple_of(t * ROW_PIECES, ROW_PIECES), ROW_PIECES)
    return pltpu.make_async_copy(src.at[rows(src_tok)], dst.at[rows(dst_tok)], sem)


DMA_UNROLL = 8


def _route_kernel(info_ref, tab_ref, dest_ref):
    info = info_ref[...]
    lanef = lax.broadcasted_iota(jnp.int32, info.shape, 1).astype(F32)
    pick = lambda col, row: jnp.sum(jnp.where(lanef == info[:, col:col + 1] + EXPERT_LANE0, tab_ref[row:row + 1], 0.0),
                                    axis=1, keepdims=True)
    d0 = pick(INFO_E0, 0) + info[:, INFO_R0:INFO_R0 + 1]
    d1 = pick(INFO_E1, 1) + info[:, INFO_R1:INFO_R1 + 1]
    dest_ref[...] = jnp.where(lanef == 0.0, d0, jnp.where(lanef == 1.0, d1, 0.0)).astype(jnp.int32)


def _route(info, tab, tm):
    N = info.shape[0]
    blk = pl.BlockSpec((tm, LANES), lambda i: (i, 0))
    return pl.pallas_call(
        _route_kernel,
        grid=(N // tm,),
        in_specs=[blk, _full(tab.shape)],
        out_specs=blk,
        out_shape=jax.ShapeDtypeStruct((N, LANES), jnp.int32),
        compiler_params=_params("arbitrary"),
        name="route",
    )(info, tab)


def _dispatch_kernel(pad_ref, nu_ref, d0_ref, d1_ref, d0p_ref, d1p_ref, h_ref, g_ref, xs_ref, xn_ref, zero_ref, sem):
    tm = h_ref.shape[0]
    fill_rows = zero_ref.shape[0]

    @pl.when(pl.program_id(0) == 0)
    def _():
        zero_ref[...] = jnp.zeros_like(zero_ref)
        fill = lambda row: pltpu.make_async_copy(
            zero_ref, xs_ref.at[pl.ds(pl.multiple_of(row * ROW_PIECES, ROW_PIECES), fill_rows)], sem.at[0])
        for e in range(N_EXPERTS):
            fill(pad_ref[e]).start()
        for e in range(N_EXPERTS):
            fill(pad_ref[e]).wait()
        n_total = xs_ref.shape[0] // fill_rows
        te = fill_rows // ROW_PIECES

        def tail_start(j, _):
            fill(j * te).start()
            return 0

        def tail_wait(j, _):
            fill(j * te).wait()
            return 0

        lax.fori_loop(nu_ref[0], n_total, tail_start, 0)
        lax.fori_loop(nu_ref[0], n_total, tail_wait, 0)

    i = pl.program_id(0)
    slot = i % 2
    _to_token_tiles(_rmsnorm(h_ref[...], g_ref[...]), xn_ref.at[slot])

    def scatters(da_ref, db_ref, s, go):
        def body(r, _):
            go(_token_copy(xn_ref.at[s], r, xs_ref, da_ref[0, 0, r], sem.at[s]), 0)
            go(_token_copy(xn_ref.at[s], r, xs_ref, db_ref[0, 0, r], sem.at[s]), 1)
            return 0
        lax.fori_loop(0, tm, body, 0, unroll=DMA_UNROLL)

    start = lambda cp, prio: cp.start(priority=prio)
    wait = lambda cp, prio: cp.wait()
    scatters(d0_ref, d1_ref, slot, start)

    @pl.when(i > 0)
    def _():
        scatters(d0p_ref, d1p_ref, 1 - slot, wait)

    @pl.when(i == pl.num_programs(0) - 1)
    def _():
        scatters(d0_ref, d1_ref, slot, wait)


def _dispatch(h2, g_ffn, dest0, dest1, pad_start, n_used, xs_rows, tm, te):
    N, D = h2.shape
    nt = N // tm
    idx = pl.BlockSpec((1, 1, tm), lambda i, pad, nu: (i, 0, 0), memory_space=pltpu.SMEM)
    idx_prev = pl.BlockSpec((1, 1, tm), lambda i, pad, nu: (jnp.maximum(i - 1, 0), 0, 0), memory_space=pltpu.SMEM)
    d0, d1 = dest0.reshape(nt, 1, tm), dest1.reshape(nt, 1, tm)
    grid_spec = pltpu.PrefetchScalarGridSpec(
        num_scalar_prefetch=2,
        grid=(nt,),
        in_specs=[idx, idx, idx_prev, idx_prev, pl.BlockSpec((tm, D), lambda i, pad, nu: (i, 0)),
                  pl.BlockSpec(g_ffn.shape, lambda i, pad, nu: (0, 0))],
        out_specs=pl.BlockSpec(memory_space=pl.ANY),
        scratch_shapes=[pltpu.VMEM((2, tm * ROW_PIECES, LANES), F32), pltpu.VMEM((te * ROW_PIECES, LANES), F32),
                        pltpu.SemaphoreType.DMA((2,))],
    )
    return pl.pallas_call(
        _dispatch_kernel,
        grid_spec=grid_spec,
        out_shape=jax.ShapeDtypeStruct(((xs_rows + te) * ROW_PIECES, LANES), F32),
        compiler_params=_params("arbitrary"),
        name="dispatch",
    )(pad_start, n_used, d0, d1, d0, d1, h2, g_ffn)


def _experts_kernel(te_ref, nu_ref, x_ref, wgu_ref, wdn_ref, y_ref, wgu_bf_ref, wdn_bf_ref):
    j = pl.program_id(0)
    te = x_ref.shape[0] // ROW_PIECES

    @pl.when((j == 0) | (te_ref[j] != te_ref[jnp.maximum(j - 1, 0)]))
    def _():
        wgu_bf_ref[...] = wgu_ref[0].astype(BF16)
        wdn_bf_ref[...] = wdn_ref[0].astype(BF16)

    @pl.when(j < nu_ref[0])
    def _():
        gu = _dot(_from_token_tiles(x_ref, te).astype(BF16), wgu_bf_ref[...])
        gate, up = gu[:, :D_EXPERT], gu[:, D_EXPERT:]
        act = gate / (1.0 + jnp.exp(-gate)) * up
        _to_token_tiles(_dot(act.astype(BF16), wdn_bf_ref[...]), y_ref)

    @pl.when(j >= nu_ref[0])
    def _():
        y_ref[...] = jnp.zeros_like(y_ref)


def _experts(xs, wgu, wdn, layer, tile_expert, n_used, xs_rows, te):
    D = D_MODEL
    tiles = pl.BlockSpec((te * ROW_PIECES, LANES), lambda j, te_, nu: (j, 0))
    grid_spec = pltpu.PrefetchScalarGridSpec(
        num_scalar_prefetch=2,
        grid=(xs_rows // te,),
        in_specs=[tiles,
                  pl.BlockSpec((None, 1, D, 2 * D_EXPERT), lambda j, te_, nu: (layer, te_[j], 0, 0)),
                  pl.BlockSpec((None, 1, D_EXPERT, D), lambda j, te_, nu: (layer, te_[j], 0, 0))],
        out_specs=tiles,
        scratch_shapes=[pltpu.VMEM((D, 2 * D_EXPERT), BF16), pltpu.VMEM((D_EXPERT, D), BF16)],
    )
    return pl.pallas_call(
        _experts_kernel,
        grid_spec=grid_spec,
        out_shape=jax.ShapeDtypeStruct((xs_rows * ROW_PIECES, LANES), F32),
        compiler_params=_params("arbitrary"),
        name="experts",
    )(tile_expert, n_used, xs, wgu, wdn)


def _combine_kernel(d0_ref, d1_ref, d0n_ref, d1n_ref, h_ref, info_ref, gfin_ref, ys_ref, o_ref, y0_ref, y1_ref, sem,
                    *, final_norm):
    tm = h_ref.shape[0]
    i = pl.program_id(0)
    slot = i % 2

    def gathers(da_ref, db_ref, s, go):
        def body(r, _):
            go(_token_copy(ys_ref, da_ref[0, 0, r], y0_ref.at[s], r, sem.at[s]), 0)
            go(_token_copy(ys_ref, db_ref[0, 0, r], y1_ref.at[s], r, sem.at[s]), 1)
            return 0
        lax.fori_loop(0, tm, body, 0, unroll=DMA_UNROLL)

    start = lambda cp, prio: cp.start(priority=prio)
    wait = lambda cp, prio: cp.wait()

    @pl.when(i == 0)
    def _():
        gathers(d0_ref, d1_ref, 0, start)

    @pl.when(i + 1 < pl.num_programs(0))
    def _():
        gathers(d0n_ref, d1n_ref, 1 - slot, start)

    gathers(d0_ref, d1_ref, slot, wait)
    info = info_ref[...]
    out = (h_ref[...] + info[:, INFO_G0:INFO_G0 + 1] * _from_token_tiles(y0_ref.at[slot], tm)
           + info[:, INFO_G1:INFO_G1 + 1] * _from_token_tiles(y1_ref.at[slot], tm))
    if final_norm:
        out = _rmsnorm(out, gfin_ref[...])
    o_ref[...] = out


def _combine(h2, info, g_final, ys, dest0, dest1, tm, final_norm):
    N, D = h2.shape
    nt = N // tm
    idx = pl.BlockSpec((1, 1, tm), lambda i: (i, 0, 0), memory_space=pltpu.SMEM)
    idx_next = pl.BlockSpec((1, 1, tm), lambda i: (jnp.minimum(i + 1, nt - 1), 0, 0), memory_space=pltpu.SMEM)
    d0, d1 = dest0.reshape(nt, 1, tm), dest1.reshape(nt, 1, tm)
    buf = pltpu.VMEM((2, tm * ROW_PIECES, LANES), F32)
    return pl.pallas_call(
        functools.partial(_combine_kernel, final_norm=final_norm),
        grid=(nt,),
        in_specs=[idx, idx, idx_next, idx_next, pl.BlockSpec((tm, D), lambda i: (i, 0)),
                  pl.BlockSpec((tm, LANES), lambda i: (i, 0)), _full(g_final.shape), pl.BlockSpec(memory_space=pl.ANY)],
        out_specs=pl.BlockSpec((tm, D), lambda i: (i, 0)),
        out_shape=jax.ShapeDtypeStruct((N, D), F32),
        scratch_shapes=[buf, buf, pltpu.SemaphoreType.DMA((2,))],
        compiler_params=_params("arbitrary"),
        name="combine",
    )(d0, d1, d0, d1, h2, info, g_final, ys)


IN_SIZES = (NSA_WIDTH,) + (KV_WIDTH,) * 6 + (NSA_HEADS * 3,) + (CONV_WIDTH,) * 3 + (HGRN_WIDTH,) * 4
IN_OFF = tuple(int(v) for v in np.cumsum((0,) + IN_SIZES))


def _rope_tables(T):
    half = HEAD_DIM // 2
    inv = ROPE_THETA ** (-jnp.arange(half, dtype=F32) / half)
    ang = jnp.arange(T).astype(F32)[:, None] * inv[None, :]
    cos, sin = jnp.cos(ang), jnp.sin(ang)
    zero = jnp.zeros_like(sin)
    reps = LANES // HEAD_DIM
    cos_t = jnp.tile(jnp.concatenate([cos, cos], axis=1), (1, reps))
    sin_a = jnp.tile(jnp.concatenate([-sin, zero], axis=1), (1, reps))
    sin_b = jnp.tile(jnp.concatenate([zero, sin], axis=1), (1, reps))
    return cos_t, sin_a, sin_b


def _cmp_to_sel(T):
    nc_pad, ns = T // CMP_STRIDE, T // SEL_BLOCK
    nc = (T - CMP_LEN) // CMP_STRIDE + 1
    cs = np.arange(nc_pad) * CMP_STRIDE
    ss = np.arange(ns) * SEL_BLOCK
    ov = np.clip(np.minimum(cs[:, None] + CMP_LEN, ss[None, :] + SEL_BLOCK)
                 - np.maximum(cs[:, None], ss[None, :]), 0, None) / CMP_LEN
    ov[nc:] = 0.0
    return jnp.asarray(np.pad(ov, ((0, 0), (0, LANES - ns))), dtype=BF16)


def _q_weight(w_in_l):
    wq = w_in_l[:, IN_OFF[0]:IN_OFF[1]].reshape(D_MODEL, NSA_KV_HEADS, NSA_GROUP, 1, HEAD_DIM)
    half = jnp.eye(NSA_KV_HEADS, dtype=wq.dtype)[None, :, None, :, None]
    return (wq * half).reshape(D_MODEL, Q_AUG_WIDTH)


def _compress_weights(pe, w1, w2):
    eye = jnp.eye(NSA_KV_HEADS, dtype=F32)
    half_len = CMP_LEN // 2
    w1r = w1.reshape(2, 2, half_len, HEAD_DIM, CMP_HIDDEN)
    w1_aug = jnp.einsum("sxldm,hg->sxlhdgm", w1r, eye).reshape(2, 2, half_len * KV_WIDTH, NSA_KV_HEADS * CMP_HIDDEN)
    w2_aug = jnp.einsum("smd,hg->shmgd", w2, eye).reshape(2, NSA_KV_HEADS * CMP_HIDDEN, KV_WIDTH)
    pe_r = jnp.broadcast_to(pe.reshape(2, 2, half_len, 1, HEAD_DIM), (2, 2, half_len, NSA_KV_HEADS, HEAD_DIM))
    pe_aug = jnp.pad(pe_r.reshape(2, 2, half_len * KV_WIDTH), ((0, 0), (0, SUBLANES - 2), (0, 0)))
    return pe_aug, w1_aug[:, 0].astype(BF16), w1_aug[:, 1].astype(BF16), w2_aug.astype(BF16)


def _gate_weight(w_in_l):
    wg = w_in_l[:, IN_OFF[7]:IN_OFF[8]].reshape(D_MODEL, NSA_KV_HEADS, NSA_GROUP * 3)
    return jnp.pad(wg, ((0, 0), (0, 0), (0, LANES - NSA_GROUP * 3))).reshape(D_MODEL, NSA_KV_HEADS * LANES)


def _segments(counts, n_tok, te):
    c0 = counts[0, EXPERT_LANE0:EXPERT_LANE0 + N_EXPERTS].astype(jnp.int32)
    c1 = counts[1, EXPERT_LANE0:EXPERT_LANE0 + N_EXPERTS].astype(jnp.int32)
    tiles = (c0 + c1 + te - 1) // te
    tile_end = jnp.cumsum(tiles)
    off = (tile_end - tiles) * te
    tab = jnp.zeros((SUBLANES, LANES), F32)
    tab = tab.at[0, EXPERT_LANE0:EXPERT_LANE0 + N_EXPERTS].set(off.astype(F32))
    tab = tab.at[1, EXPERT_LANE0:EXPERT_LANE0 + N_EXPERTS].set((off + c0).astype(F32))
    n_tiles = (2 * n_tok) // te + N_EXPERTS
    tile_expert = jnp.sum(tile_end[None, :] <= jnp.arange(n_tiles)[:, None], axis=1)
    tile_expert = jnp.minimum(tile_expert, N_EXPERTS - 1).astype(jnp.int32)
    return tab, off + c0 + c1, tile_expert, tile_end[-1:].astype(jnp.int32)


def kernel(x, mem, w_in, conv_w, cmp_pe, cmp_w1, cmp_w2, hgrn_lb_logits, hgrn_norm, w_out, cross_wq, cross_wkv,
           cross_wo, router_group_w, router_group_b, router_expert_w, router_expert_b, expert_w_gate_up,
           expert_w_down, norm_mix, norm_cross, norm_mem, norm_ffn, norm_final):
    B, T, D = x.shape
    depth = w_in.shape[0]
    n_tok = B * T
    tm = min(512, T)
    tm_row = min(512, T)
    tm_gather = min(256, T)
    te = 512

    cos_t, sin_a, sin_b = _rope_tables(T)
    c2s = _cmp_to_sel(T)
    ltri = jnp.asarray(np.tril(np.ones((tm, tm), np.float32), -1), dtype=BF16)

    p_lb = jax.nn.softmax(hgrn_lb_logits.astype(F32), axis=0)
    lower_bounds = jnp.cumsum(p_lb, axis=0) - p_lb[0:1]
    xs_rows = 2 * n_tok + N_EXPERTS * te

    h = x
    for l in range(depth):
        wl = w_in[l]
        piece = lambda a, b: wl[:, IN_OFF[a]:IN_OFF[b]].astype(BF16)
        cw = jnp.pad(conv_w[l], ((0, SUBLANES - CONV_K), (0, 0)))
        (q, qr, kc, vc, ka, vs, kw, vw, gs, o_conv, hg) = _inproj(
            h, norm_mix[l][None], cos_t, sin_a, sin_b, _q_weight(wl).astype(BF16), piece(1, 7),
            _gate_weight(wl).astype(BF16), piece(8, 11), piece(11, 15), cw, tm)

        chunks = lambda a: a.reshape(B, T // CMP_STRIDE, CMP_STRIDE * KV_WIDTH)
        kvc = _compress(jnp.stack([chunks(kc), chunks(vc)]), *_compress_weights(cmp_pe[l], cmp_w1[l], cmp_w2[l]))
        o_nsa = _nsa(q, qr, kvc, ka, vs, kw, vw, gs, c2s)

        lb = lower_bounds[l][None]
        lb_params = jnp.concatenate([jnp.log(jnp.maximum(lb, LB_FLOOR)), jnp.log1p(-lb), 1.0 - lb,
                                     jnp.zeros((SUBLANES - 3, HGRN_WIDTH), F32)], axis=0)
        o_hgrn = _hgrn(hg, lb_params, hgrn_norm[l][None], tm)

        kmem, vmem = _memkv(mem, norm_mem[l][None], cross_wkv[l].astype(BF16))

        wr = jnp.zeros((D, LANES), F32).at[:, :N_GROUPS].set(router_group_w[l])
        wr = wr.at[:, EXPERT_LANE0:EXPERT_LANE0 + N_EXPERTS].set(router_expert_w[l])
        br = jnp.zeros((1, LANES), F32).at[0, :N_GROUPS].set(router_group_b[l])
        br = br.at[0, EXPERT_LANE0:EXPERT_LANE0 + N_EXPERTS].set(router_expert_b[l])
        wr_hi = wr.astype(BF16)
        wr_lo = jnp.concatenate([wr_hi, (wr - wr_hi.astype(F32)).astype(BF16)], axis=1)
        h2, info, counts = _post(h, o_nsa, o_conv, o_hgrn, w_out[l].astype(BF16), norm_cross[l][None],
                                 cross_wq[l].astype(BF16), kmem, vmem, cross_wo[l].astype(BF16), norm_ffn[l][None],
                                 wr_hi, wr_lo, br, ltri, tm)

        h2 = h2.reshape(n_tok, D)
        info = info.reshape(n_tok, LANES)
        tab, pad_start, tile_expert, n_used = _segments(counts, n_tok, te)
        dest = _route(info, tab, tm)
        dest0, dest1 = dest[:, 0], dest[:, 1]
        xs = _dispatch(h2, norm_ffn[l][None], dest0, dest1, pad_start, n_used, xs_rows, tm_row, te)
        ys = _experts(xs, expert_w_gate_up, expert_w_down, l, tile_expert, n_used, xs_rows, te)
        h = _combine(h2, info, norm_final[None], ys, dest0, dest1, tm_gather, l == depth - 1).reshape(B, T, D)
    return h
```
